```python
import jax, jax.numpy as jnp
from jax import lax
import numpy as np

D_MODEL = 4096
BATCH = 2
SEQ = 4096
DEPTH = 2

PLE_DIM = 256
M_HEADS = 8
M_V_DIM = D_MODEL // M_HEADS
M_QK_DIM = M_V_DIM // 2
F_GROUPS = 4
F_GROUP_DIM = D_MODEL // F_GROUPS
QK_W = M_HEADS * M_QK_DIM
V_W = M_HEADS * M_V_DIM
GATE_W = 4 * M_HEADS
F_W = F_GROUPS * F_GROUP_DIM
N_IN = 2 * QK_W + 2 * V_W + GATE_W + F_W
D_FF = 4 * D_MODEL
CHUNK = 128
GATE_CAP = 15.0
EPS = 1e-6

kernel_name = "hybrid_mlstm_fnet_gated_encoder"


def _rms_norm(x, g):
    xf = x.astype(jnp.float32)
    y = xf * lax.rsqrt(jnp.mean(xf * xf, axis=-1, keepdims=True) + EPS)
    return (y * g.astype(jnp.float32)).astype(x.dtype)


def _softcap(a):
    return GATE_CAP * jnp.tanh(a / GATE_CAP)


def _mlstm_scan(q, k, v, log_i, log_f):
    b_, h_, s_, dk = q.shape
    dv = v.shape[-1]
    nc = s_ // CHUNK

    def to_chunks(a):
        a = a.reshape(a.shape[:2] + (nc, CHUNK) + a.shape[3:])
        return jnp.moveaxis(a, 2, 0)

    xs = (to_chunks(q), to_chunks(k), to_chunks(v), to_chunks(log_i), to_chunks(log_f))
    tril = jnp.tril(jnp.ones((CHUNK, CHUNK), dtype=bool))

    def step(carry, xc):
        c_prev, n_prev, m_prev = carry
        qc, kc, vc, ic, fc = xc
        b = jnp.cumsum(fc, axis=-1)
        log_d = b[..., :, None] - b[..., None, :] + ic[..., None, :]
        log_d = jnp.where(tril, log_d, -jnp.inf)
        m_inter = b + m_prev[..., None]
        m = jnp.maximum(m_inter, jnp.max(log_d, axis=-1))
        w_intra = jnp.exp(log_d - m[..., None])
        w_inter = jnp.exp(m_inter - m)
        s = jnp.einsum('bhjd,bhsd->bhjs', qc, kc) * w_intra
        num = jnp.einsum('bhjs,bhsv->bhjv', s, vc) + w_inter[..., None] * jnp.einsum('bhjd,bhdv->bhjv', qc, c_prev)
        den = jnp.sum(s, axis=-1) + w_inter * jnp.einsum('bhjd,bhd->bhj', qc, n_prev)
        h = num / jnp.maximum(jnp.abs(den), jnp.exp(-m))[..., None]
        b_last = b[..., -1]
        a = b_last[..., None] - b + ic
        m_new = jnp.maximum(b_last + m_prev, jnp.max(a, axis=-1))
        w_state = jnp.exp(a - m_new[..., None])
        decay = jnp.exp(b_last + m_prev - m_new)
        kw = kc * w_state[..., None]
        c_new = decay[..., None, None] * c_prev + jnp.einsum('bhsd,bhsv->bhdv', kw, vc)
        n_new = decay[..., None] * n_prev + jnp.sum(kw, axis=2)
        return (c_new, n_new, m_new), h

    init = (jnp.zeros((b_, h_, dk, dv), jnp.float32),
            jnp.zeros((b_, h_, dk), jnp.float32),
            jnp.zeros((b_, h_), jnp.float32))
    _, hs = lax.scan(step, init, xs)
    return jnp.moveaxis(hs, 0, 2).reshape(b_, h_, s_, dv)


def _bidir_mlstm(q, k, v, gates, b_igate, b_fgate):
    bsz, seq = gates.shape[0], gates.shape[1]
    g = gates.astype(jnp.float32).reshape(bsz, seq, 2, 2, M_HEADS)
    log_i = _softcap(g[:, :, :, 0] + b_igate.astype(jnp.float32))
    log_f = jax.nn.log_sigmoid(_softcap(g[:, :, :, 1] + b_fgate.astype(jnp.float32)))
    log_i = jnp.transpose(log_i, (2, 0, 3, 1))
    log_f = jnp.transpose(log_f, (2, 0, 3, 1))
    flip = lambda a: jnp.flip(a, axis=2)
    h_fwd = _mlstm_scan(q, k, v, log_i[0], log_f[0])
    h_bwd = flip(_mlstm_scan(flip(q), flip(k), flip(v), flip(log_i[1]), flip(log_f[1])))
    return h_fwd + h_bwd


def _layer(x, p_i, g_mix_pre, g_mix_post, g_mlp_pre, g_mlp_post, g_ple_post, w_in, b_igate, b_fgate,
           g_head, w_branch_m, w_branch_f, w_merge, b_merge, w_out, w_up, w_down, w_ple_in, w_ple_gate):
    bsz, seq, _ = x.shape
    xn = _rms_norm(x, g_mix_pre)
    u = xn @ w_in
    q, k, v, o, gates, z = jnp.split(
        u, [QK_W, 2 * QK_W, 2 * QK_W + V_W, 2 * QK_W + 2 * V_W, 2 * QK_W + 2 * V_W + GATE_W], axis=-1)

    def heads(a, d):
        return a.reshape(bsz, seq, M_HEADS, d).transpose(0, 2, 1, 3).astype(jnp.float32)

    qh = heads(q, M_QK_DIM) * (M_QK_DIM ** -0.5)
    kh = heads(k, M_QK_DIM)
    vh = heads(v, M_V_DIM)
    h = _bidir_mlstm(qh, kh, vh, gates, b_igate, b_fgate)
    h = h * lax.rsqrt(jnp.mean(h * h, axis=-1, keepdims=True) + EPS)
    h = h * g_head.astype(jnp.float32).reshape(M_HEADS, 1, M_V_DIM)
    h = h.transpose(0, 2, 1, 3).reshape(bsz, seq, V_W).astype(x.dtype)
    h = h * jax.nn.sigmoid(o)
    y_m = h @ w_branch_m

    zf = z.astype(jnp.float32).reshape(bsz, seq, F_GROUPS, F_GROUP_DIM)
    zf = jnp.fft.fftn(zf, axes=(1, 3), norm="ortho").real
    y_f = zf.reshape(bsz, seq, F_W).astype(x.dtype) @ w_branch_f

    gate = jax.nn.sigmoid(xn @ w_merge + b_merge)
    g_m, g_f = jnp.split(gate, 2, axis=-1)
    mix = (g_m * y_m + g_f * y_f) @ w_out
    x = x + _rms_norm(mix, g_mix_post)
    hn = _rms_norm(x, g_mlp_pre)
    y = jnp.square(jax.nn.relu(hn @ w_up)) @ w_down
    x = x + _rms_norm(y, g_mlp_post)
    ple = jax.nn.sigmoid(x @ w_ple_gate) * (p_i @ w_ple_in)
    x = x + _rms_norm(ple, g_ple_post)
    return x


def setup_inputs(seed: int = 0) -> dict:
    key = jax.random.key(seed)
    ks = jax.random.split(key, 24)

    def normal(k, shape, scale):
        return jax.random.normal(k, shape, jnp.float32) * scale

    def gain(k):
        return 1.0 + normal(k, (DEPTH, D_MODEL), 0.02)

    b_fgate = jnp.linspace(3.0, 6.0, M_HEADS, dtype=jnp.float32)[None, None, :] + normal(ks[9], (DEPTH, 2, M_HEADS), 0.1)
    return {
        "x": normal(ks[0], (BATCH, SEQ, D_MODEL), 1.0),
        "p": normal(ks[1], (DEPTH, BATCH, SEQ, PLE_DIM), 1.0),
        "g_mix_pre": gain(ks[2]),
        "g_mix_post": gain(ks[3]),
        "g_mlp_pre": gain(ks[4]),
        "g_mlp_post": gain(ks[5]),
        "g_ple_post": gain(ks[6]),
        "w_in": normal(ks[7], (DEPTH, D_MODEL, N_IN), D_MODEL ** -0.5),
        "b_igate": normal(ks[8], (DEPTH, 2, M_HEADS), 0.1),
        "b_fgate": b_fgate,
        "g_head": 1.0 + normal(ks[10], (DEPTH, V_W), 0.02),
        "w_branch_m": normal(ks[11], (DEPTH, V_W, D_MODEL), V_W ** -0.5),
        "w_branch_f": normal(ks[12], (DEPTH, F_W, D_MODEL), F_W ** -0.5),
        "w_merge": normal(ks[13], (DEPTH, D_MODEL, 2 * D_MODEL), D_MODEL ** -0.5),
        "b_merge": normal(ks[14], (DEPTH, 2 * D_MODEL), 0.01),
        "w_out": normal(ks[15], (DEPTH, D_MODEL, D_MODEL), D_MODEL ** -0.5),
        "w_up": normal(ks[16], (DEPTH, D_MODEL, D_FF), D_MODEL ** -0.5),
        "w_down": normal(ks[17], (DEPTH, D_FF, D_MODEL), D_FF ** -0.5),
        "w_ple_in": normal(ks[18], (DEPTH, PLE_DIM, D_MODEL), PLE_DIM ** -0.5),
        "w_ple_gate": normal(ks[19], (DEPTH, D_MODEL, D_MODEL), D_MODEL ** -0.5),
    }


def reference(x, p, g_mix_pre, g_mix_post, g_mlp_pre, g_mlp_post, g_ple_post, w_in, b_igate, b_fgate,
              g_head, w_branch_m, w_branch_f, w_merge, b_merge, w_out, w_up, w_down, w_ple_in, w_ple_gate):
    for i in range(DEPTH):
        x = _layer(x, p[i], g_mix_pre[i], g_mix_post[i], g_mlp_pre[i], g_mlp_post[i], g_ple_post[i],
                   w_in[i], b_igate[i], b_fgate[i], g_head[i], w_branch_m[i], w_branch_f[i],
                   w_merge[i], b_merge[i], w_out[i], w_up[i], w_down[i], w_ple_in[i], w_ple_gate[i])
    return x
```

```python
import functools
import math

import jax
import jax.numpy as jnp
from jax import lax
from jax.experimental import pallas as pl
from jax.experimental.pallas import tpu as pltpu

F32 = jnp.float32
BF16 = jnp.bfloat16

M_HEADS = 8
F_GROUPS = 4
GATE_CAP = 15.0
EPS = 1e-6
MLSTM_CHUNK = 256
V7X_VMEM_BYTES = 64 * 1024 * 1024
VMEM_LIMIT = V7X_VMEM_BYTES - 8 * 1024 * 1024


def _params(*semantics):
    return pltpu.CompilerParams(dimension_semantics=semantics, vmem_limit_bytes=VMEM_LIMIT)


def _rms(x, g):
    return x * lax.rsqrt(jnp.mean(x * x, axis=-1, keepdims=True) + EPS) * g


def _rmsnorm_kernel(x_ref, g_ref, o_ref):
    o_ref[...] = _rms(x_ref[...], g_ref[...]).astype(o_ref.dtype)


def rmsnorm(x, g, tm=256):
    m, d = x.shape
    return pl.pallas_call(
        _rmsnorm_kernel,
        grid=(m // tm,),
        in_specs=[pl.BlockSpec((tm, d), lambda i: (i, 0)),
                  pl.BlockSpec((1, d), lambda i: (0, 0))],
        out_specs=pl.BlockSpec((tm, d), lambda i: (i, 0)),
        out_shape=jax.ShapeDtypeStruct((m, d), BF16),
        compiler_params=_params("parallel"),
        name="rmsnorm",
    )(x, g.reshape(1, d))


def _resnorm_kernel(x_ref, y_ref, gp_ref, gn_ref, xo_ref, xn_ref, *, norm_next):
    xnew = x_ref[...] + _rms(y_ref[...].astype(F32), gp_ref[...])
    xo_ref[...] = xnew
    if norm_next:
        xn_ref[...] = _rms(xnew, gn_ref[...]).astype(xn_ref.dtype)
    else:
        xn_ref[...] = xnew.astype(xn_ref.dtype)


def resnorm(x, y, g_post, g_next, tm=256):
    m, d = x.shape
    norm_next = g_next is not None
    gn = g_next if norm_next else g_post
    row = pl.BlockSpec((tm, d), lambda i: (i, 0))
    vec = pl.BlockSpec((1, d), lambda i: (0, 0))
    return pl.pallas_call(
        functools.partial(_resnorm_kernel, norm_next=norm_next),
        grid=(m // tm,),
        in_specs=[row, row, vec, vec],
        out_specs=[row, row],
        out_shape=[jax.ShapeDtypeStruct((m, d), F32), jax.ShapeDtypeStruct((m, d), BF16)],
        compiler_params=_params("parallel"),
        name="resnorm",
    )(x, y, g_post.reshape(1, d), gn.reshape(1, d))


def _dots_kernel(*refs, n_dots, n_extra, epilogue):
    accs = [jnp.dot(refs[2 * i][...], refs[2 * i + 1][...], preferred_element_type=F32)
            for i in range(n_dots)]
    extras = [r[...] for r in refs[2 * n_dots:2 * n_dots + n_extra]]
    out_ref = refs[-1]
    out_ref[...] = epilogue(accs, extras).astype(out_ref.dtype)


def fused_dots(pairs, extras, epilogue, out_shape, out_dtype, tm, tn, name):
    m, n = out_shape
    in_specs, args = [], []
    for a, a_cb, a_step, w, w_cb, w_step in pairs:
        k = w.shape[0]
        in_specs.append(pl.BlockSpec((tm, k), lambda i, j, c=a_cb, s=a_step: (i, c + j * s)))
        in_specs.append(pl.BlockSpec((k, tn), lambda i, j, c=w_cb, s=w_step: (0, c + j * s)))
        args += [a, w]
    for arr, kind, cb0 in extras:
        if kind == "row":
            in_specs.append(pl.BlockSpec((1, tn), lambda i, j, cb0=cb0: (0, cb0 + j)))
        else:
            in_specs.append(pl.BlockSpec((tm, tn), lambda i, j, cb0=cb0: (i, cb0 + j)))
        args.append(arr)
    return pl.pallas_call(
        functools.partial(_dots_kernel, n_dots=len(pairs), n_extra=len(extras), epilogue=epilogue),
        grid=(m // tm, n // tn),
        in_specs=in_specs,
        out_specs=pl.BlockSpec((tm, tn), lambda i, j: (i, j)),
        out_shape=jax.ShapeDtypeStruct((m, n), out_dtype),
        compiler_params=_params("parallel", "parallel"),
        name=name,
    )(*args)


def _ep_identity(accs, extras):
    return accs[0]


def _ep_bias_sigmoid(accs, extras):
    return jax.nn.sigmoid(accs[0] + extras[0])


def _ep_relu2(accs, extras):
    return jnp.square(jnp.maximum(accs[0], 0.0))


def _ep_gated_sum(accs, extras):
    return extras[0].astype(F32) * accs[0] + extras[1].astype(F32) * accs[1]


def _ep_sigmoid_mul(accs, extras):
    return jax.nn.sigmoid(accs[0]) * accs[1]


def _ep_sub(accs, extras):
    return accs[0] - accs[1]


def _matmul_ktiled_kernel(a_ref, w_ref, o_ref, acc_ref):
    kk = pl.program_id(2)

    @pl.when(kk == 0)
    def _():
        acc_ref[...] = jnp.zeros_like(acc_ref)

    acc_ref[...] += jnp.dot(a_ref[...], w_ref[...], preferred_element_type=F32)

    @pl.when(kk == pl.num_programs(2) - 1)
    def _():
        o_ref[...] = acc_ref[...].astype(o_ref.dtype)


def matmul_ktiled(a, w, out_dtype, tm, tn, tk, name):
    m, k = a.shape
    n = w.shape[1]
    return pl.pallas_call(
        _matmul_ktiled_kernel,
        grid=(m // tm, n // tn, k // tk),
        in_specs=[pl.BlockSpec((tm, tk), lambda i, j, kk: (i, kk)),
                  pl.BlockSpec((tk, tn), lambda i, j, kk: (kk, j))],
        out_specs=pl.BlockSpec((tm, tn), lambda i, j, kk: (i, j)),
        out_shape=jax.ShapeDtypeStruct((m, n), out_dtype),
        scratch_shapes=[pltpu.VMEM((tm, tn), F32)],
        compiler_params=_params("parallel", "parallel", "arbitrary"),
        name=name,
    )(a, w)


def _split3(x):
    hi = x.astype(BF16)
    r1 = x - hi.astype(F32)
    mid = r1.astype(BF16)
    lo = (r1 - mid.astype(F32)).astype(BF16)
    return hi, mid, lo


def _gate_prep_kernel(gi_ref, gf_ref, bi_ref, bf_ref, b_ref, e_ref, r_ref, *, chunk):
    s, w = gi_ref.shape
    nh = w // 2
    log_i = GATE_CAP * jnp.tanh((gi_ref[...] + bi_ref[...]) * (1.0 / GATE_CAP))
    af = GATE_CAP * jnp.tanh((gf_ref[...] + bf_ref[...]) * (1.0 / GATE_CAP))
    log_f = jnp.minimum(af, 0.0) - jnp.log1p(jnp.exp(-jnp.abs(af)))
    ri = lax.broadcasted_iota(jnp.int32, (chunk, chunk), 0)
    ci = lax.broadcasted_iota(jnp.int32, (chunk, chunk), 1)
    tril = (ci <= ri).astype(BF16)
    triu = (ci >= ri).astype(BF16)
    is_fwd = lax.broadcasted_iota(jnp.int32, (chunk, w), 1) < nh
    for c in range(s // chunk):
        rows = slice(c * chunk, (c + 1) * chunk)
        parts = _split3(log_f[rows, :])
        cum_f = sum(jnp.dot(tril, p, preferred_element_type=F32) for p in parts)
        cum_b = sum(jnp.dot(triu, p, preferred_element_type=F32) for p in parts)
        b = jnp.where(is_fwd, cum_f, cum_b)
        tot = jnp.where(is_fwd[:1], cum_f[chunk - 1:chunk, :], cum_b[0:1, :])
        li = log_i[rows, :]
        b_ref[rows, :] = b
        e_ref[rows, :] = tot - b + li
        r_ref[rows, :] = li - b


def gate_prep(gi, gf, bi, bf, chunk):
    bsz, s, w = gi.shape
    blk = pl.BlockSpec((None, s, w), lambda b: (b, 0, 0))
    vec = pl.BlockSpec((1, w), lambda b: (0, 0))
    out = jax.ShapeDtypeStruct((bsz, s, w), F32)
    return pl.pallas_call(
        functools.partial(_gate_prep_kernel, chunk=chunk),
        grid=(bsz,),
        in_specs=[blk, blk, vec, vec],
        out_specs=[blk, blk, blk],
        out_shape=[out, out, out],
        compiler_params=_params("parallel"),
        name="gate_prep",
    )(gi, gf, bi, bf)


def _mlstm_kernel(q_ref, k_ref, v_ref, o_ref, gcol_ref, grow_ref, gh_ref, out_ref,
                  hf_ref, c_ref, n_ref, *, chunk, q_scale):
    s = q_ref.shape[0]
    nc = s // chunk
    ri = lax.broadcasted_iota(jnp.int32, (chunk, chunk), 0)
    ci = lax.broadcasted_iota(jnp.int32, (chunk, chunk), 1)

    for d in (0, 1):
        mask = (ci <= ri) if d == 0 else (ci >= ri)
        c_ref[...] = jnp.zeros_like(c_ref)
        n_ref[...] = jnp.zeros_like(n_ref)

        def body(t, carry, d=d, mask=mask):
            c = t if d == 0 else nc - 1 - t
            rows = pl.ds(pl.multiple_of(c * chunk, chunk), chunk)
            q = q_ref[rows, :] * q_scale
            k = k_ref[rows, :]
            v = v_ref[rows, :]
            gc = gcol_ref[rows, :]
            bcol = gc[:, 2 * d:2 * d + 1]
            ecol = gc[:, 2 * d + 1:2 * d + 2]
            rrow = grow_ref[d, c]
            last = chunk - 1 if d == 0 else 0
            decay = jnp.exp(bcol[last:last + 1, :])

            w_intra = jnp.exp(jnp.where(mask, bcol + rrow, -jnp.inf))
            w_inter = jnp.exp(bcol)
            sc = lax.dot_general(q, k, (((1,), (1,)), ((), ())), preferred_element_type=F32) * w_intra
            c_prev = c_ref[...]
            n_prev = n_ref[...]
            inter = jnp.dot(q, c_prev.astype(BF16), preferred_element_type=F32)
            num = jnp.dot(sc.astype(BF16), v, preferred_element_type=F32) + w_inter * inter
            qn = jnp.sum(q.astype(F32) * n_prev, axis=1, keepdims=True)
            den = jnp.sum(sc, axis=1, keepdims=True) + w_inter * qn
            h = num * (1.0 / jnp.maximum(jnp.abs(den), 1.0))

            kw = k.astype(F32) * jnp.exp(ecol)
            c_ref[...] = decay * c_prev + lax.dot_general(
                kw.astype(BF16), v, (((0,), (0,)), ((), ())), preferred_element_type=F32)
            n_ref[...] = decay * n_prev + jnp.sum(kw, axis=0, keepdims=True)

            if d == 0:
                hf_ref[rows, :] = h
            else:
                tot = hf_ref[rows, :] + h
                y = _rms(tot, gh_ref[...]) * jax.nn.sigmoid(o_ref[rows, :].astype(F32))
                out_ref[rows, :] = y.astype(out_ref.dtype)
            return carry

        lax.fori_loop(0, nc, body, 0)


def mlstm(u3, gcol, grow, g_head, *, dk, dv, q_cb0, k_cb0, v_cb0, o_cb0, chunk):
    bsz, s, _ = u3.shape
    nh = gcol.shape[1]
    nc = s // chunk

    def col(width, cb0):
        return pl.BlockSpec((None, s, width), lambda b, h, cb0=cb0: (b, 0, cb0 + h))

    return pl.pallas_call(
        functools.partial(_mlstm_kernel, chunk=chunk, q_scale=dk ** -0.5),
        grid=(bsz, nh),
        in_specs=[col(dk, q_cb0), col(dk, k_cb0), col(dv, v_cb0), col(dv, o_cb0),
                  pl.BlockSpec((None, None, s, 4), lambda b, h: (b, h, 0, 0)),
                  pl.BlockSpec((None, None, 2, nc, 1, chunk), lambda b, h: (b, h, 0, 0, 0, 0)),
                  pl.BlockSpec((1, dv), lambda b, h: (0, h))],
        out_specs=pl.BlockSpec((None, s, dv), lambda b, h: (b, 0, h)),
        out_shape=jax.ShapeDtypeStruct((bsz, s, nh * dv), BF16),
        scratch_shapes=[pltpu.VMEM((s, dv), F32), pltpu.VMEM((dk, dv), F32), pltpu.VMEM((1, dk), F32)],
        compiler_params=_params("parallel", "parallel"),
        name="mlstm",
    )(u3, u3, u3, u3, gcol, grow, g_head.reshape(1, nh * dv))


def _dft_mats(n, scale):
    j = jnp.arange(n, dtype=jnp.int32)
    ang = ((j[:, None] * j[None, :]) % n).astype(F32) * (2.0 * math.pi / n)
    return (jnp.cos(ang) * scale).astype(BF16), (jnp.sin(ang) * scale).astype(BF16)


def _dft_seq_kernel(cs_ref, ss_ref, tc_ref, ts_ref, o_ref):
    acc = jnp.dot(cs_ref[...], tc_ref[...], preferred_element_type=F32)
    acc -= jnp.dot(ss_ref[...], ts_ref[...], preferred_element_type=F32)
    o_ref[...] = acc.astype(o_ref.dtype)


def dft_seq(cs, ss, t3, gd, tm):
    bsz, s, w = t3.shape
    ng = w // (2 * gd)
    return pl.pallas_call(
        _dft_seq_kernel,
        grid=(bsz, ng, s // tm),
        in_specs=[pl.BlockSpec((tm, s), lambda b, g, i: (i, 0)),
                  pl.BlockSpec((tm, s), lambda b, g, i: (i, 0)),
                  pl.BlockSpec((None, s, gd), lambda b, g, i: (b, 0, 2 * g)),
                  pl.BlockSpec((None, s, gd), lambda b, g, i: (b, 0, 2 * g + 1))],
        out_specs=pl.BlockSpec((None, tm, gd), lambda b, g, i: (b, i, g)),
        out_shape=jax.ShapeDtypeStruct((bsz, s, ng * gd), BF16),
        compiler_params=_params("parallel", "parallel", "arbitrary"),
        name="dft_seq",
    )(cs, ss, t3, t3)


def _tile(n, pref):
    return pref if n % pref == 0 else n


def _layer(x, xn, p_i, lw, g_next, *, bsz, seq):
    m, d = x.shape
    nh = M_HEADS
    dv = d // nh
    dk = dv // 2
    qk_w, v_w = nh * dk, nh * dv
    gd = d // F_GROUPS
    n_main = 2 * qk_w + 2 * v_w
    g0 = n_main
    z0 = n_main + 4 * nh
    chunk = min(MLSTM_CHUNK, seq)
    tm = _tile(m, 1024)

    w_in = lw["w_in"]
    w_main = jnp.concatenate([w_in[:, :n_main], w_in[:, z0:]], axis=1).astype(BF16)
    w_gate = jnp.pad(w_in[:, g0:z0], ((0, 0), (0, 128 - 4 * nh))).astype(BF16)

    u = fused_dots([(xn, 0, 0, w_main, 0, 1)], [], _ep_identity, (m, n_main + d), BF16,
                   tm, _tile(n_main + d, 1024), "in_proj")
    graw = fused_dots([(xn, 0, 0, w_gate, 0, 1)], [], _ep_identity, (m, 128), F32, tm, 128, "gate_proj")

    graw = graw[:, :4 * nh].reshape(bsz, seq, 2, 2, nh)
    gi = graw[:, :, :, 0].reshape(bsz, seq, 2 * nh)
    gf = graw[:, :, :, 1].reshape(bsz, seq, 2 * nh)
    b_arr, e_arr, r_arr = gate_prep(gi, gf, lw["b_igate"].reshape(1, 2 * nh),
                                    lw["b_fgate"].reshape(1, 2 * nh), chunk)
    b4 = b_arr.reshape(bsz, seq, 2, nh)
    e4 = e_arr.reshape(bsz, seq, 2, nh)
    gcol = jnp.stack([b4[:, :, 0], e4[:, :, 0], b4[:, :, 1], e4[:, :, 1]], axis=-1)
    gcol = gcol.transpose(0, 2, 1, 3)
    grow = r_arr.reshape(bsz, seq // chunk, chunk, 2, nh).transpose(0, 4, 3, 1, 2)
    grow = grow[:, :, :, :, None, :]
    u3 = u.reshape(bsz, seq, n_main + d)
    hg = mlstm(u3, gcol, grow, lw["g_head"], dk=dk, dv=dv, q_cb0=0, k_cb0=qk_w // dk,
               v_cb0=2 * qk_w // dv, o_cb0=(2 * qk_w + v_w) // dv, chunk=chunk)
    hg = hg.reshape(m, v_w)

    cc, sc_ = _dft_mats(gd, gd ** -0.5)
    cs, ss = _dft_mats(seq, seq ** -0.5)
    w_chan = jnp.concatenate([cc, sc_], axis=1)
    t = fused_dots([(u, n_main // gd, 1, w_chan, 0, 0)], [], _ep_identity, (m, 2 * d), BF16,
                   tm, 2 * gd, "dft_chan")
    zf = dft_seq(cs, ss, t.reshape(bsz, seq, 2 * d), gd, _tile(seq, 256)).reshape(m, d)

    tn = _tile(d, 1024)
    gates = fused_dots([(xn, 0, 0, lw["w_merge"].astype(BF16), 0, 1)],
                       [(lw["b_merge"].reshape(1, 2 * d), "row", 0)],
                       _ep_bias_sigmoid, (m, 2 * d), BF16, tm, tn, "merge_gate")
    mix = fused_dots([(hg, 0, 0, lw["w_branch_m"].astype(BF16), 0, 1),
                      (zf, 0, 0, lw["w_branch_f"].astype(BF16), 0, 1)],
                     [(gates, "tile", 0), (gates, "tile", d // _tile(d, 512))],
                     _ep_gated_sum, (m, d), BF16, _tile(m, 512), _tile(d, 512), "branch_mix")
    y = fused_dots([(mix, 0, 0, lw["w_out"].astype(BF16), 0, 1)], [], _ep_identity, (m, d), F32,
                   tm, tn, "out_proj")
    x, hn = resnorm(x, y, lw["g_mix_post"], lw["g_mlp_pre"])

    d_ff = lw["w_up"].shape[1]
    up = fused_dots([(hn, 0, 0, lw["w_up"].astype(BF16), 0, 1)], [], _ep_relu2, (m, d_ff), BF16,
                    tm, _tile(d_ff, 1024), "mlp_up")
    y = matmul_ktiled(up, lw["w_down"].astype(BF16), F32, tm, tn, _tile(d_ff, 4096), "mlp_down")
    x, xb = resnorm(x, y, lw["g_mlp_post"], None)

    ple = fused_dots([(xb, 0, 0, lw["w_ple_gate"].astype(BF16), 0, 1),
                      (p_i.astype(BF16), 0, 0, lw["w_ple_in"].astype(BF16), 0, 1)],
                     [], _ep_sigmoid_mul, (m, d), F32, tm, _tile(d, 512), "ple")
    if g_next is None:
        x, _ = resnorm(x, ple, lw["g_ple_post"], None)
        return x, None
    return resnorm(x, ple, lw["g_ple_post"], g_next)


def kernel(x, p, g_mix_pre, g_mix_post, g_mlp_pre, g_mlp_post, g_ple_post, w_in, b_igate, b_fgate,
           g_head, w_branch_m, w_branch_f, w_merge, b_merge, w_out, w_up, w_down, w_ple_in, w_ple_gate):
    bsz, seq, d = x.shape
    depth = w_in.shape[0]
    m = bsz * seq
    xf = x.reshape(m, d)
    xn = rmsnorm(xf, g_mix_pre[0])
    for i in range(depth):
        lw = dict(w_in=w_in[i], b_igate=b_igate[i], b_fgate=b_fgate[i], g_head=g_head[i],
                  w_branch_m=w_branch_m[i], w_branch_f=w_branch_f[i], w_merge=w_merge[i],
                  b_merge=b_merge[i], w_out=w_out[i], w_up=w_up[i], w_down=w_down[i],
                  w_ple_in=w_ple_in[i], w_ple_gate=w_ple_gate[i], g_mix_post=g_mix_post[i],
                  g_mlp_pre=g_mlp_pre[i], g_mlp_post=g_mlp_post[i], g_ple_post=g_ple_post[i])
        g_next = g_mix_pre[i + 1] if i + 1 < depth else None
        xf, xn = _layer(xf, xn, p[i].reshape(m, -1), lw, g_next, bsz=bsz, seq=seq)
    return xf.reshape(bsz, seq, d)
```

```python
import functools
import math

import jax
import jax.numpy as jnp
from jax import lax
from jax.experimental import pallas as pl
from jax.experimental.pallas import tpu as pltpu

F32 = jnp.float32
BF16 = jnp.bfloat16

M_HEADS = 8
F_GROUPS = 4
GATE_CAP = 15.0
EPS = 1e-6
MLSTM_CHUNK = 256
V7X_VMEM_BYTES = 64 * 1024 * 1024
VMEM_LIMIT = V7X_VMEM_BYTES - 8 * 1024 * 1024


def _params(*semantics):
    return pltpu.CompilerParams(dimension_semantics=semantics, vmem_limit_bytes=VMEM_LIMIT)


def _rms(x, g):
    return x * lax.rsqrt(jnp.mean(x * x, axis=-1, keepdims=True) + EPS) * g


def _rmsnorm_kernel(x_ref, g_ref, o_ref):
    o_ref[...] = _rms(x_ref[...], g_ref[...]).astype(o_ref.dtype)


def rmsnorm(x, g, tm=256):
    m, d = x.shape
    return pl.pallas_call(
        _rmsnorm_kernel,
        grid=(m // tm,),
        in_specs=[pl.BlockSpec((tm, d), lambda i: (i, 0)),
                  pl.BlockSpec((1, d), lambda i: (0, 0))],
        out_specs=pl.BlockSpec((tm, d), lambda i: (i, 0)),
        out_shape=jax.ShapeDtypeStruct((m, d), BF16),
        compiler_params=_params("parallel"),
        name="rmsnorm",
    )(x, g.reshape(1, d))


def _resnorm_kernel(x_ref, y_ref, gp_ref, gn_ref, xo_ref, xn_ref, *, norm_next):
    xnew = x_ref[...] + _rms(y_ref[...].astype(F32), gp_ref[...])
    xo_ref[...] = xnew
    if norm_next:
        xn_ref[...] = _rms(xnew, gn_ref[...]).astype(xn_ref.dtype)
    else:
        xn_ref[...] = xnew.astype(xn_ref.dtype)


def resnorm(x, y, g_post, g_next, tm=256):
    m, d = x.shape
    norm_next = g_next is not None
    gn = g_next if norm_next else g_post
    row = pl.BlockSpec((tm, d), lambda i: (i, 0))
    vec = pl.BlockSpec((1, d), lambda i: (0, 0))
    return pl.pallas_call(
        functools.partial(_resnorm_kernel, norm_next=norm_next),
        grid=(m // tm,),
        in_specs=[row, row, vec, vec],
        out_specs=[row, row],
        out_shape=[jax.ShapeDtypeStruct((m, d), F32), jax.ShapeDtypeStruct((m, d), BF16)],
        compiler_params=_params("parallel"),
        name="resnorm",
    )(x, y, g_post.reshape(1, d), gn.reshape(1, d))


def _dots_kernel(*refs, n_dots, n_extra, cast, epilogue):
    out_ref = refs[2 * n_dots + n_extra]
    scratch = list(refs[2 * n_dots + n_extra + 1:])
    w_refs = []
    for p in range(n_dots):
        w_refs.append(scratch.pop(0) if cast[p] else refs[2 * p + 1])

    @pl.when(pl.program_id(1) == 0)
    def _():
        for p in range(n_dots):
            if cast[p]:
                w_refs[p][...] = refs[2 * p + 1][...].astype(BF16)

    accs = [jnp.dot(refs[2 * p][...].astype(BF16), w_refs[p][...], preferred_element_type=F32)
            for p in range(n_dots)]
    extras = [r[...] for r in refs[2 * n_dots:2 * n_dots + n_extra]]
    out_ref[...] = epilogue(accs, extras).astype(out_ref.dtype)


def fused_dots(pairs, extras, epilogue, out_shape, out_dtype, tm, tn, name):
    m, n = out_shape
    in_specs, args, scratch, cast = [], [], [], []
    for a, a_cb, a_step, w, layer, w_cb, w_step in pairs:
        k = w.shape[-2]
        in_specs.append(pl.BlockSpec((tm, k), lambda j, i, c=a_cb, s=a_step: (i, c + j * s)))
        if layer is None:
            in_specs.append(pl.BlockSpec((k, tn), lambda j, i, c=w_cb, s=w_step: (0, c + j * s)))
        else:
            in_specs.append(pl.BlockSpec((None, k, tn),
                                         lambda j, i, l=layer, c=w_cb, s=w_step: (l, 0, c + j * s)))
        args += [a, w]
        cast.append(w.dtype != BF16)
        if cast[-1]:
            scratch.append(pltpu.VMEM((k, tn), BF16))
    for arr, kind, cb0 in extras:
        if kind == "row":
            in_specs.append(pl.BlockSpec((1, tn), lambda j, i, cb0=cb0: (0, cb0 + j)))
        else:
            in_specs.append(pl.BlockSpec((tm, tn), lambda j, i, cb0=cb0: (i, cb0 + j)))
        args.append(arr)
    return pl.pallas_call(
        functools.partial(_dots_kernel, n_dots=len(pairs), n_extra=len(extras), cast=tuple(cast),
                          epilogue=epilogue),
        grid=(n // tn, m // tm),
        in_specs=in_specs,
        out_specs=pl.BlockSpec((tm, tn), lambda j, i: (i, j)),
        out_shape=jax.ShapeDtypeStruct((m, n), out_dtype),
        scratch_shapes=scratch,
        compiler_params=_params("parallel", "arbitrary"),
        name=name,
    )(*args)


def _cast_kernel(x_ref, o_ref):
    o_ref[...] = x_ref[...].astype(o_ref.dtype)


def cast_rows(w, layer, dtype, tr):
    _, k, n = w.shape
    return pl.pallas_call(
        _cast_kernel,
        grid=(k // tr,),
        in_specs=[pl.BlockSpec((None, tr, n), lambda r, l=layer: (l, r, 0))],
        out_specs=pl.BlockSpec((tr, n), lambda r: (r, 0)),
        out_shape=jax.ShapeDtypeStruct((k, n), dtype),
        compiler_params=_params("parallel"),
        name="cast_rows",
    )(w)


def _ep_identity(accs, extras):
    return accs[0]


def _ep_bias_sigmoid(accs, extras):
    return jax.nn.sigmoid(accs[0] + extras[0])


def _ep_relu2(accs, extras):
    return jnp.square(jnp.maximum(accs[0], 0.0))


def _ep_gate_mul(accs, extras):
    return extras[0].astype(F32) * accs[0]


def _ep_gate_mul_add(accs, extras):
    return extras[0].astype(F32) * accs[0] + extras[1].astype(F32)


def _ep_sigmoid_mul(accs, extras):
    return jax.nn.sigmoid(accs[0]) * accs[1]


def _ep_sub(accs, extras):
    return accs[0] - accs[1]


def _matmul_ktiled_kernel(a_ref, w_ref, o_ref, acc_ref):
    kk = pl.program_id(2)

    @pl.when(kk == 0)
    def _():
        acc_ref[...] = jnp.zeros_like(acc_ref)

    acc_ref[...] += jnp.dot(a_ref[...], w_ref[...], preferred_element_type=F32)

    @pl.when(kk == pl.num_programs(2) - 1)
    def _():
        o_ref[...] = acc_ref[...].astype(o_ref.dtype)


def matmul_ktiled(a, w, out_dtype, tm, tn, tk, name):
    m, k = a.shape
    n = w.shape[1]
    return pl.pallas_call(
        _matmul_ktiled_kernel,
        grid=(m // tm, n // tn, k // tk),
        in_specs=[pl.BlockSpec((tm, tk), lambda i, j, kk: (i, kk)),
                  pl.BlockSpec((tk, tn), lambda i, j, kk: (kk, j))],
        out_specs=pl.BlockSpec((tm, tn), lambda i, j, kk: (i, j)),
        out_shape=jax.ShapeDtypeStruct((m, n), out_dtype),
        scratch_shapes=[pltpu.VMEM((tm, tn), F32)],
        compiler_params=_params("parallel", "parallel", "arbitrary"),
        name=name,
    )(a, w)


def _split3(x):
    hi = x.astype(BF16)
    r1 = x - hi.astype(F32)
    mid = r1.astype(BF16)
    lo = (r1 - mid.astype(F32)).astype(BF16)
    return hi, mid, lo


def _gate_prep_kernel(gi_ref, gf_ref, bi_ref, bf_ref, b_ref, e_ref, r_ref, *, chunk):
    s, w = gi_ref.shape
    nh = w // 2
    log_i = GATE_CAP * jnp.tanh((gi_ref[...] + bi_ref[...]) * (1.0 / GATE_CAP))
    af = GATE_CAP * jnp.tanh((gf_ref[...] + bf_ref[...]) * (1.0 / GATE_CAP))
    log_f = jnp.minimum(af, 0.0) - jnp.log1p(jnp.exp(-jnp.abs(af)))
    ri = lax.broadcasted_iota(jnp.int32, (chunk, chunk), 0)
    ci = lax.broadcasted_iota(jnp.int32, (chunk, chunk), 1)
    tril = (ci <= ri).astype(BF16)
    triu = (ci >= ri).astype(BF16)
    is_fwd = lax.broadcasted_iota(jnp.int32, (chunk, w), 1) < nh
    for c in range(s // chunk):
        rows = slice(c * chunk, (c + 1) * chunk)
        parts = _split3(log_f[rows, :])
        cum_f = sum(jnp.dot(tril, p, preferred_element_type=F32) for p in parts)
        cum_b = sum(jnp.dot(triu, p, preferred_element_type=F32) for p in parts)
        b = jnp.where(is_fwd, cum_f, cum_b)
        tot = jnp.where(is_fwd[:1], cum_f[chunk - 1:chunk, :], cum_b[0:1, :])
        li = log_i[rows, :]
        b_ref[rows, :] = b
        e_ref[rows, :] = tot - b + li
        r_ref[rows, :] = li - b


def gate_prep(gi, gf, bi, bf, chunk):
    bsz, s, w = gi.shape
    blk = pl.BlockSpec((None, s, w), lambda b: (b, 0, 0))
    vec = pl.BlockSpec((1, w), lambda b: (0, 0))
    out = jax.ShapeDtypeStruct((bsz, s, w), F32)
    return pl.pallas_call(
        functools.partial(_gate_prep_kernel, chunk=chunk),
        grid=(bsz,),
        in_specs=[blk, blk, vec, vec],
        out_specs=[blk, blk, blk],
        out_shape=[out, out, out],
        compiler_params=_params("parallel"),
        name="gate_prep",
    )(gi, gf, bi, bf)


def _mlstm_kernel(q_ref, k_ref, v_ref, o_ref, gcol_ref, grow_ref, gh_ref, out_ref,
                  hf_ref, c_ref, n_ref, *, chunk, q_scale):
    s = q_ref.shape[0]
    nc = s // chunk
    ri = lax.broadcasted_iota(jnp.int32, (chunk, chunk), 0)
    ci = lax.broadcasted_iota(jnp.int32, (chunk, chunk), 1)

    for d in (0, 1):
        mask = (ci <= ri) if d == 0 else (ci >= ri)
        c_ref[...] = jnp.zeros_like(c_ref)
        n_ref[...] = jnp.zeros_like(n_ref)

        def body(t, carry, d=d, mask=mask):
            c = t if d == 0 else nc - 1 - t
            rows = pl.ds(pl.multiple_of(c * chunk, chunk), chunk)
            q = q_ref[rows, :] * q_scale
            k = k_ref[rows, :]
            v = v_ref[rows, :]
            gc = gcol_ref[rows, :]
            bcol = gc[:, 2 * d:2 * d + 1]
            ecol = gc[:, 2 * d + 1:2 * d + 2]
            rrow = grow_ref[d, c]
            last = chunk - 1 if d == 0 else 0
            decay = jnp.exp(bcol[last:last + 1, :])

            w_intra = jnp.exp(jnp.where(mask, bcol + rrow, -jnp.inf))
            w_inter = jnp.exp(bcol)
            sc = lax.dot_general(q, k, (((1,), (1,)), ((), ())), preferred_element_type=F32) * w_intra
            c_prev = c_ref[...]
            n_prev = n_ref[...]
            inter = jnp.dot(q, c_prev.astype(BF16), preferred_element_type=F32)
            num = jnp.dot(sc.astype(BF16), v, preferred_element_type=F32) + w_inter * inter
            qn = jnp.sum(q.astype(F32) * n_prev, axis=1, keepdims=True)
            den = jnp.sum(sc, axis=1, keepdims=True) + w_inter * qn
            h = num * (1.0 / jnp.maximum(jnp.abs(den), 1.0))

            kw = k.astype(F32) * jnp.exp(ecol)
            c_ref[...] = decay * c_prev + lax.dot_general(
                kw.astype(BF16), v, (((0,), (0,)), ((), ())), preferred_element_type=F32)
            n_ref[...] = decay * n_prev + jnp.sum(kw, axis=0, keepdims=True)

            if d == 0:
                hf_ref[rows, :] = h
            else:
                tot = hf_ref[rows, :] + h
                y = _rms(tot, gh_ref[...]) * jax.nn.sigmoid(o_ref[rows, :].astype(F32))
                out_ref[rows, :] = y.astype(out_ref.dtype)
            return carry

        lax.fori_loop(0, nc, body, 0)


def mlstm(u3, gcol, grow, g_head, *, dk, dv, q_cb0, k_cb0, v_cb0, o_cb0, chunk):
    bsz, s, _ = u3.shape
    nh = gcol.shape[1]
    nc = s // chunk

    def col(width, cb0):
        return pl.BlockSpec((None, s, width), lambda b, h, cb0=cb0: (b, 0, cb0 + h))

    return pl.pallas_call(
        functools.partial(_mlstm_kernel, chunk=chunk, q_scale=dk ** -0.5),
        grid=(bsz, nh),
        in_specs=[col(dk, q_cb0), col(dk, k_cb0), col(dv, v_cb0), col(dv, o_cb0),
                  pl.BlockSpec((None, None, s, 4), lambda b, h: (b, h, 0, 0)),
                  pl.BlockSpec((None, None, 2, nc, 1, chunk), lambda b, h: (b, h, 0, 0, 0, 0)),
                  pl.BlockSpec((1, dv), lambda b, h: (0, h))],
        out_specs=pl.BlockSpec((None, s, dv), lambda b, h: (b, 0, h)),
        out_shape=jax.ShapeDtypeStruct((bsz, s, nh * dv), BF16),
        scratch_shapes=[pltpu.VMEM((s, dv), F32), pltpu.VMEM((dk, dv), F32), pltpu.VMEM((1, dk), F32)],
        compiler_params=_params("parallel", "parallel"),
        name="mlstm",
    )(u3, u3, u3, u3, gcol, grow, g_head.reshape(1, nh * dv))


def _cos_sin(rows, n_cols, period):
    k = jnp.arange(n_cols, dtype=jnp.int32)
    ang = ((rows[:, None] * k[None, :]) % period).astype(F32) * (2.0 * math.pi / period)
    return jnp.cos(ang), jnp.sin(ang)


def _dft_mats(n, scale):
    n1 = 1 << ((n.bit_length() - 1) // 2)
    n2 = n // n1
    ca, sa = _cos_sin(jnp.arange(n1, dtype=jnp.int32), n, n1)
    cb, sb = _cos_sin(jnp.arange(n2, dtype=jnp.int32), n, n)
    ca, sa, cb, sb = ca[:, None, :], sa[:, None, :], cb[None, :, :], sb[None, :, :]
    c = (ca * cb - sa * sb) * scale
    s = (sa * cb + ca * sb) * scale
    return c.reshape(n, n).astype(BF16), s.reshape(n, n).astype(BF16)


def _dft_seq_kernel(cs_ref, ss_ref, tc_ref, ts_ref, o_ref):
    acc = jnp.dot(cs_ref[...], tc_ref[...], preferred_element_type=F32)
    acc -= jnp.dot(ss_ref[...], ts_ref[...], preferred_element_type=F32)
    o_ref[...] = acc.astype(o_ref.dtype)


def dft_seq(cs, ss, t3, gd, tm):
    bsz, s, w = t3.shape
    ng = w // (2 * gd)
    return pl.pallas_call(
        _dft_seq_kernel,
        grid=(bsz, ng, s // tm),
        in_specs=[pl.BlockSpec((tm, s), lambda b, g, i: (i, 0)),
                  pl.BlockSpec((tm, s), lambda b, g, i: (i, 0)),
                  pl.BlockSpec((None, s, gd), lambda b, g, i: (b, 0, 2 * g)),
                  pl.BlockSpec((None, s, gd), lambda b, g, i: (b, 0, 2 * g + 1))],
        out_specs=pl.BlockSpec((None, tm, gd), lambda b, g, i: (b, i, g)),
        out_shape=jax.ShapeDtypeStruct((bsz, s, ng * gd), BF16),
        compiler_params=_params("parallel", "parallel", "arbitrary"),
        name="dft_seq",
    )(cs, ss, t3, t3)


def _tile(n, pref):
    return pref if n % pref == 0 else n


def _layer(x, xn, p_i, lw, g_next, *, bsz, seq):
    m, d = x.shape
    nh = M_HEADS
    dv = d // nh
    dk = dv // 2
    qk_w, v_w = nh * dk, nh * dv
    gd = d // F_GROUPS
    n_main = 2 * qk_w + 2 * v_w
    g0 = n_main
    z0 = n_main + 4 * nh
    chunk = min(MLSTM_CHUNK, seq)
    tm = _tile(m, 1024)

    li = lw["layer"]
    tn = _tile(d, 512)
    w_in = lw["w_in"]
    w_gate = jnp.pad(w_in[li, :, g0:z0], ((0, 0), (0, 128 - 4 * nh)))
    w_z = w_in[li, :, z0:]

    u = fused_dots([(xn, 0, 0, w_in, li, 0, 1)], [], _ep_identity, (m, n_main), BF16, tm, tn, "in_proj")
    z = fused_dots([(xn, 0, 0, w_z, None, 0, 1)], [], _ep_identity, (m, d), BF16, tm, tn, "in_proj_z")
    graw = fused_dots([(xn, 0, 0, w_gate, None, 0, 1)], [], _ep_identity, (m, 128), F32, tm, 128,
                      "gate_proj")

    graw = graw[:, :4 * nh].reshape(bsz, seq, 2, 2, nh)
    gi = graw[:, :, :, 0].reshape(bsz, seq, 2 * nh)
    gf = graw[:, :, :, 1].reshape(bsz, seq, 2 * nh)
    b_arr, e_arr, r_arr = gate_prep(gi, gf, lw["b_igate"][li].reshape(1, 2 * nh),
                                    lw["b_fgate"][li].reshape(1, 2 * nh), chunk)
    b4 = b_arr.reshape(bsz, seq, 2, nh)
    e4 = e_arr.reshape(bsz, seq, 2, nh)
    gcol = jnp.stack([b4[:, :, 0], e4[:, :, 0], b4[:, :, 1], e4[:, :, 1]], axis=-1)
    gcol = gcol.transpose(0, 2, 1, 3)
    grow = r_arr.reshape(bsz, seq // chunk, chunk, 2, nh).transpose(0, 4, 3, 1, 2)
    grow = grow[:, :, :, :, None, :]
    u3 = u.reshape(bsz, seq, n_main)
    hg = mlstm(u3, gcol, grow, lw["g_head"][li], dk=dk, dv=dv, q_cb0=0, k_cb0=qk_w // dk,
               v_cb0=2 * qk_w // dv, o_cb0=(2 * qk_w + v_w) // dv, chunk=chunk)
    hg = hg.reshape(m, v_w)

    cc, sc_ = _dft_mats(gd, gd ** -0.5)
    cs, ss = _dft_mats(seq, seq ** -0.5)
    w_chan = jnp.concatenate([cc, sc_], axis=1)
    t = fused_dots([(z, 0, 1, w_chan, None, 0, 0)], [], _ep_identity, (m, 2 * d), BF16,
                   tm, 2 * gd, "dft_chan")
    zf = dft_seq(cs, ss, t.reshape(bsz, seq, 2 * d), gd, _tile(seq, 256)).reshape(m, d)

    gates = fused_dots([(xn, 0, 0, lw["w_merge"], li, 0, 1)],
                       [(lw["b_merge"][li].reshape(1, 2 * d), "row", 0)],
                       _ep_bias_sigmoid, (m, 2 * d), BF16, tm, tn, "merge_gate")
    ym = fused_dots([(hg, 0, 0, lw["w_branch_m"], li, 0, 1)], [(gates, "tile", 0)],
                    _ep_gate_mul, (m, d), BF16, tm, tn, "branch_m")
    mix = fused_dots([(zf, 0, 0, lw["w_branch_f"], li, 0, 1)], [(gates, "tile", d // tn), (ym, "tile", 0)],
                     _ep_gate_mul_add, (m, d), BF16, tm, tn, "branch_f")
    y = fused_dots([(mix, 0, 0, lw["w_out"], li, 0, 1)], [], _ep_identity, (m, d), F32, tm, tn, "out_proj")
    x, hn = resnorm(x, y, lw["g_mix_post"][li], lw["g_mlp_pre"][li])

    d_ff = lw["w_up"].shape[-1]
    up = fused_dots([(hn, 0, 0, lw["w_up"], li, 0, 1)], [], _ep_relu2, (m, d_ff), BF16, tm, tn, "mlp_up")
    w_down = cast_rows(lw["w_down"], li, BF16, _tile(d_ff, 512))
    y = matmul_ktiled(up, w_down, F32, tm, _tile(d, 1024), _tile(d_ff, 4096), "mlp_down")
    x, xb = resnorm(x, y, lw["g_mlp_post"][li], None)

    ple = fused_dots([(xb, 0, 0, lw["w_ple_gate"], li, 0, 1), (p_i, 0, 0, lw["w_ple_in"], li, 0, 1)],
                     [], _ep_sigmoid_mul, (m, d), F32, tm, tn, "ple")
    if g_next is None:
        x, _ = resnorm(x, ple, lw["g_ple_post"][li], None)
        return x, None
    return resnorm(x, ple, lw["g_ple_post"][li], g_next)


def kernel(x, p, g_mix_pre, g_mix_post, g_mlp_pre, g_mlp_post, g_ple_post, w_in, b_igate, b_fgate,
           g_head, w_branch_m, w_branch_f, w_merge, b_merge, w_out, w_up, w_down, w_ple_in, w_ple_gate):
    bsz, seq, d = x.shape
    depth = w_in.shape[0]
    m = bsz * seq
    xf = x.reshape(m, d)
    xn = rmsnorm(xf, g_mix_pre[0])
    lw = dict(w_in=w_in, b_igate=b_igate, b_fgate=b_fgate, g_head=g_head, w_branch_m=w_branch_m,
              w_branch_f=w_branch_f, w_merge=w_merge, b_merge=b_merge, w_out=w_out, w_up=w_up,
              w_down=w_down, w_ple_in=w_ple_in, w_ple_gate=w_ple_gate, g_mix_post=g_mix_post,
              g_mlp_pre=g_mlp_pre, g_mlp_post=g_mlp_post, g_ple_post=g_ple_post)
    for i in range(depth):
        g_next = g_mix_pre[i + 1] if i + 1 < depth else None
        xf, xn = _layer(xf, xn, p[i].reshape(m, -1), dict(lw, layer=i), g_next, bsz=bsz, seq=seq)
    return xf.reshape(bsz, seq, d)
```

```python
import functools
import math

import jax
import jax.numpy as jnp
from jax import lax
from jax.experimental import pallas as pl
from jax.experimental.pallas import tpu as pltpu

F32 = jnp.float32
BF16 = jnp.bfloat16

M_HEADS = 8
F_GROUPS = 4
GATE_CAP = 15.0
EPS = 1e-6
MLSTM_CHUNK = 256
V7X_VMEM_BYTES = 64 * 1024 * 1024
VMEM_LIMIT = V7X_VMEM_BYTES - 8 * 1024 * 1024


def _params(*semantics):
    return pltpu.CompilerParams(dimension_semantics=semantics, vmem_limit_bytes=VMEM_LIMIT)


def _rms(x, g):
    return x * lax.rsqrt(jnp.mean(x * x, axis=-1, keepdims=True) + EPS) * g


def _rmsnorm_kernel(x_ref, g_ref, o_ref):
    o_ref[...] = _rms(x_ref[...], g_ref[...]).astype(o_ref.dtype)


def rmsnorm(x, g, tm=256):
    m, d = x.shape
    return pl.pallas_call(
        _rmsnorm_kernel,
        grid=(m // tm,),
        in_specs=[pl.BlockSpec((tm, d), lambda i: (i, 0)),
                  pl.BlockSpec((1, d), lambda i: (0, 0))],
        out_specs=pl.BlockSpec((tm, d), lambda i: (i, 0)),
        out_shape=jax.ShapeDtypeStruct((m, d), BF16),
        compiler_params=_params("parallel"),
        name="rmsnorm",
    )(x, g.reshape(1, d))


def _resnorm_kernel(x_ref, y_ref, gp_ref, gn_ref, xo_ref, xn_ref, *, norm_next):
    xnew = x_ref[...] + _rms(y_ref[...].astype(F32), gp_ref[...])
    xo_ref[...] = xnew
    if norm_next:
        xn_ref[...] = _rms(xnew, gn_ref[...]).astype(xn_ref.dtype)
    else:
        xn_ref[...] = xnew.astype(xn_ref.dtype)


def resnorm(x, y, g_post, g_next, tm=256):
    m, d = x.shape
    norm_next = g_next is not None
    gn = g_next if norm_next else g_post
    row = pl.BlockSpec((tm, d), lambda i: (i, 0))
    vec = pl.BlockSpec((1, d), lambda i: (0, 0))
    return pl.pallas_call(
        functools.partial(_resnorm_kernel, norm_next=norm_next),
        grid=(m // tm,),
        in_specs=[row, row, vec, vec],
        out_specs=[row, row],
        out_shape=[jax.ShapeDtypeStruct((m, d), F32), jax.ShapeDtypeStruct((m, d), BF16)],
        compiler_params=_params("parallel"),
        name="resnorm",
    )(x, y, g_post.reshape(1, d), gn.reshape(1, d))


def _dots_kernel(*refs, n_dots, n_extra, cast, epilogue):
    out_ref = refs[2 * n_dots + n_extra]
    scratch = list(refs[2 * n_dots + n_extra + 1:])
    w_refs = []
    for p in range(n_dots):
        w_refs.append(scratch.pop(0) if cast[p] else refs[2 * p + 1])

    @pl.when(pl.program_id(1) == 0)
    def _():
        for p in range(n_dots):
            if cast[p]:
                w_refs[p][...] = refs[2 * p + 1][...].astype(BF16)

    accs = [jnp.dot(refs[2 * p][...].astype(BF16), w_refs[p][...], preferred_element_type=F32)
            for p in range(n_dots)]
    extras = [r[...] for r in refs[2 * n_dots:2 * n_dots + n_extra]]
    out_ref[...] = epilogue(accs, extras).astype(out_ref.dtype)


def fused_dots(pairs, extras, epilogue, out_shape, out_dtype, tm, tn, name):
    m, n = out_shape
    in_specs, args, scratch, cast = [], [], [], []
    for a, a_cb, a_step, w, layer, w_cb, w_step in pairs:
        k = w.shape[-2]
        in_specs.append(pl.BlockSpec((tm, k), lambda j, i, c=a_cb, s=a_step: (i, c + j * s)))
        if layer is None:
            in_specs.append(pl.BlockSpec((k, tn), lambda j, i, c=w_cb, s=w_step: (0, c + j * s)))
        else:
            in_specs.append(pl.BlockSpec((None, k, tn),
                                         lambda j, i, l=layer, c=w_cb, s=w_step: (l, 0, c + j * s)))
        args += [a, w]
        cast.append(w.dtype != BF16)
        if cast[-1]:
            scratch.append(pltpu.VMEM((k, tn), BF16))
    for arr, kind, cb0 in extras:
        if kind == "row":
            in_specs.append(pl.BlockSpec((1, tn), lambda j, i, cb0=cb0: (0, cb0 + j)))
        else:
            in_specs.append(pl.BlockSpec((tm, tn), lambda j, i, cb0=cb0: (i, cb0 + j)))
        args.append(arr)
    return pl.pallas_call(
        functools.partial(_dots_kernel, n_dots=len(pairs), n_extra=len(extras), cast=tuple(cast),
                          epilogue=epilogue),
        grid=(n // tn, m // tm),
        in_specs=in_specs,
        out_specs=pl.BlockSpec((tm, tn), lambda j, i: (i, j)),
        out_shape=jax.ShapeDtypeStruct((m, n), out_dtype),
        scratch_shapes=scratch,
        compiler_params=_params("parallel", "arbitrary"),
        name=name,
    )(*args)


def _dot_wt_kernel(a_ref, wt_ref, o_ref, wbf_ref):
    @pl.when(pl.program_id(1) == 0)
    def _():
        wbf_ref[...] = wt_ref[0].T.astype(BF16)

    o_ref[...] = jnp.dot(a_ref[...], wbf_ref[...], preferred_element_type=F32).astype(o_ref.dtype)


def dot_wt(a, wt, layer, row_offset, n_out, out_dtype, tm, tn, name):
    m, k = a.shape
    return pl.pallas_call(
        _dot_wt_kernel,
        grid=(n_out // tn, m // tm),
        in_specs=[pl.BlockSpec((tm, k), lambda j, i: (i, 0)),
                  pl.BlockSpec((pl.Element(1), pl.Element(tn), pl.Element(k)),
                               lambda j, i: (layer, pl.multiple_of(row_offset(j), 8), 0))],
        out_specs=pl.BlockSpec((tm, tn), lambda j, i: (i, j)),
        out_shape=jax.ShapeDtypeStruct((m, n_out), out_dtype),
        scratch_shapes=[pltpu.VMEM((k, tn), BF16)],
        compiler_params=_params("parallel", "arbitrary"),
        name=name,
    )(a, wt)


def _cast_kernel(x_ref, o_ref):
    o_ref[...] = x_ref[...].astype(o_ref.dtype)


def cast_rows(w, layer, dtype, tr):
    _, k, n = w.shape
    return pl.pallas_call(
        _cast_kernel,
        grid=(k // tr,),
        in_specs=[pl.BlockSpec((None, tr, n), lambda r, l=layer: (l, r, 0))],
        out_specs=pl.BlockSpec((tr, n), lambda r: (r, 0)),
        out_shape=jax.ShapeDtypeStruct((k, n), dtype),
        compiler_params=_params("parallel"),
        name="cast_rows",
    )(w)


def _ep_identity(accs, extras):
    return accs[0]


def _ep_bias_sigmoid(accs, extras):
    return jax.nn.sigmoid(accs[0] + extras[0])


def _ep_relu2(accs, extras):
    return jnp.square(jnp.maximum(accs[0], 0.0))


def _ep_gate_mul(accs, extras):
    return extras[0].astype(F32) * accs[0]


def _ep_gate_mul_add(accs, extras):
    return extras[0].astype(F32) * accs[0] + extras[1].astype(F32)


def _ep_sigmoid_mul(accs, extras):
    return jax.nn.sigmoid(accs[0]) * accs[1]


def _ep_sub(accs, extras):
    return accs[0] - accs[1]


def _matmul_ktiled_kernel(a_ref, w_ref, o_ref, acc_ref):
    kk = pl.program_id(2)

    @pl.when(kk == 0)
    def _():
        acc_ref[...] = jnp.zeros_like(acc_ref)

    acc_ref[...] += jnp.dot(a_ref[...], w_ref[...], preferred_element_type=F32)

    @pl.when(kk == pl.num_programs(2) - 1)
    def _():
        o_ref[...] = acc_ref[...].astype(o_ref.dtype)


def matmul_ktiled(a, w, out_dtype, tm, tn, tk, name):
    m, k = a.shape
    n = w.shape[1]
    return pl.pallas_call(
        _matmul_ktiled_kernel,
        grid=(m // tm, n // tn, k // tk),
        in_specs=[pl.BlockSpec((tm, tk), lambda i, j, kk: (i, kk)),
                  pl.BlockSpec((tk, tn), lambda i, j, kk: (kk, j))],
        out_specs=pl.BlockSpec((tm, tn), lambda i, j, kk: (i, j)),
        out_shape=jax.ShapeDtypeStruct((m, n), out_dtype),
        scratch_shapes=[pltpu.VMEM((tm, tn), F32)],
        compiler_params=_params("parallel", "parallel", "arbitrary"),
        name=name,
    )(a, w)


def _split3(x):
    hi = x.astype(BF16)
    r1 = x - hi.astype(F32)
    mid = r1.astype(BF16)
    lo = (r1 - mid.astype(F32)).astype(BF16)
    return hi, mid, lo


def _gate_prep_kernel(gi_ref, gf_ref, bi_ref, bf_ref, b_ref, e_ref, r_ref, *, chunk):
    s, w = gi_ref.shape
    nh = w // 2
    log_i = GATE_CAP * jnp.tanh((gi_ref[...] + bi_ref[...]) * (1.0 / GATE_CAP))
    af = GATE_CAP * jnp.tanh((gf_ref[...] + bf_ref[...]) * (1.0 / GATE_CAP))
    log_f = jnp.minimum(af, 0.0) - jnp.log1p(jnp.exp(-jnp.abs(af)))
    ri = lax.broadcasted_iota(jnp.int32, (chunk, chunk), 0)
    ci = lax.broadcasted_iota(jnp.int32, (chunk, chunk), 1)
    tril = (ci <= ri).astype(BF16)
    triu = (ci >= ri).astype(BF16)
    is_fwd = lax.broadcasted_iota(jnp.int32, (chunk, w), 1) < nh
    for c in range(s // chunk):
        rows = slice(c * chunk, (c + 1) * chunk)
        parts = _split3(log_f[rows, :])
        cum_f = sum(jnp.dot(tril, p, preferred_element_type=F32) for p in parts)
        cum_b = sum(jnp.dot(triu, p, preferred_element_type=F32) for p in parts)
        b = jnp.where(is_fwd, cum_f, cum_b)
        tot = jnp.where(is_fwd[:1], cum_f[chunk - 1:chunk, :], cum_b[0:1, :])
        li = log_i[rows, :]
        b_ref[rows, :] = b
        e_ref[rows, :] = tot - b + li
        r_ref[rows, :] = li - b


def gate_prep(gi, gf, bi, bf, chunk):
    bsz, s, w = gi.shape
    blk = pl.BlockSpec((None, s, w), lambda b: (b, 0, 0))
    vec = pl.BlockSpec((1, w), lambda b: (0, 0))
    out = jax.ShapeDtypeStruct((bsz, s, w), F32)
    return pl.pallas_call(
        functools.partial(_gate_prep_kernel, chunk=chunk),
        grid=(bsz,),
        in_specs=[blk, blk, vec, vec],
        out_specs=[blk, blk, blk],
        out_shape=[out, out, out],
        compiler_params=_params("parallel"),
        name="gate_prep",
    )(gi, gf, bi, bf)


def _mlstm_kernel(q_ref, k_ref, v_ref, o_ref, gcol_ref, grow_ref, gh_ref, out_ref,
                  hf_ref, c_ref, n_ref, *, chunk, q_scale):
    s = q_ref.shape[0]
    nc = s // chunk
    ri = lax.broadcasted_iota(jnp.int32, (chunk, chunk), 0)
    ci = lax.broadcasted_iota(jnp.int32, (chunk, chunk), 1)

    for d in (0, 1):
        mask = (ci <= ri) if d == 0 else (ci >= ri)
        c_ref[...] = jnp.zeros_like(c_ref)
        n_ref[...] = jnp.zeros_like(n_ref)

        def body(t, carry, d=d, mask=mask):
            c = t if d == 0 else nc - 1 - t
            rows = pl.ds(pl.multiple_of(c * chunk, chunk), chunk)
            q = q_ref[rows, :] * q_scale
            k = k_ref[rows, :]
            v = v_ref[rows, :]
            gc = gcol_ref[rows, :]
            bcol = gc[:, 2 * d:2 * d + 1]
            ecol = gc[:, 2 * d + 1:2 * d + 2]
            rrow = grow_ref[d, c]
            last = chunk - 1 if d == 0 else 0
            decay = jnp.exp(bcol[last:last + 1, :])

            w_intra = jnp.exp(jnp.where(mask, bcol + rrow, -jnp.inf))
            w_inter = jnp.exp(bcol)
            sc = lax.dot_general(q, k, (((1,), (1,)), ((), ())), preferred_element_type=F32) * w_intra
            c_prev = c_ref[...]
            n_prev = n_ref[...]
            inter = jnp.dot(q, c_prev.astype(BF16), preferred_element_type=F32)
            num = jnp.dot(sc.astype(BF16), v, preferred_element_type=F32) + w_inter * inter
            qn = jnp.sum(q.astype(F32) * n_prev, axis=1, keepdims=True)
            den = jnp.sum(sc, axis=1, keepdims=True) + w_inter * qn
            h = num * (1.0 / jnp.maximum(jnp.abs(den), 1.0))

            kw = k.astype(F32) * jnp.exp(ecol)
            c_ref[...] = decay * c_prev + lax.dot_general(
                kw.astype(BF16), v, (((0,), (0,)), ((), ())), preferred_element_type=F32)
            n_ref[...] = decay * n_prev + jnp.sum(kw, axis=0, keepdims=True)

            if d == 0:
                hf_ref[rows, :] = h
            else:
                tot = hf_ref[rows, :] + h
                y = _rms(tot, gh_ref[...]) * jax.nn.sigmoid(o_ref[rows, :].astype(F32))
                out_ref[rows, :] = y.astype(out_ref.dtype)
            return carry

        lax.fori_loop(0, nc, body, 0)


def mlstm(u3, gcol, grow, g_head, *, dk, dv, q_cb0, k_cb0, v_cb0, o_cb0, chunk):
    bsz, s, _ = u3.shape
    nh = gcol.shape[1]
    nc = s // chunk

    def col(width, cb0):
        return pl.BlockSpec((None, s, width), lambda b, h, cb0=cb0: (b, 0, cb0 + h))

    return pl.pallas_call(
        functools.partial(_mlstm_kernel, chunk=chunk, q_scale=dk ** -0.5),
        grid=(bsz, nh),
        in_specs=[col(dk, q_cb0), col(dk, k_cb0), col(dv, v_cb0), col(dv, o_cb0),
                  pl.BlockSpec((None, None, s, 4), lambda b, h: (b, h, 0, 0)),
                  pl.BlockSpec((None, None, 2, nc, 1, chunk), lambda b, h: (b, h, 0, 0, 0, 0)),
                  pl.BlockSpec((1, dv), lambda b, h: (0, h))],
        out_specs=pl.BlockSpec((None, s, dv), lambda b, h: (b, 0, h)),
        out_shape=jax.ShapeDtypeStruct((bsz, s, nh * dv), BF16),
        scratch_shapes=[pltpu.VMEM((s, dv), F32), pltpu.VMEM((dk, dv), F32), pltpu.VMEM((1, dk), F32)],
        compiler_params=_params("parallel", "parallel"),
        name="mlstm",
    )(u3, u3, u3, u3, gcol, grow, g_head.reshape(1, nh * dv))


def _cos_sin(rows, n_cols, period):
    k = jnp.arange(n_cols, dtype=jnp.int32)
    ang = ((rows[:, None] * k[None, :]) % period).astype(F32) * (2.0 * math.pi / period)
    return jnp.cos(ang), jnp.sin(ang)


def _dft_mats(n, scale):
    n1 = 1 << ((n.bit_length() - 1) // 2)
    n2 = n // n1
    ca, sa = _cos_sin(jnp.arange(n1, dtype=jnp.int32), n, n1)
    cb, sb = _cos_sin(jnp.arange(n2, dtype=jnp.int32), n, n)
    ca, sa, cb, sb = ca[:, None, :], sa[:, None, :], cb[None, :, :], sb[None, :, :]
    c = (ca * cb - sa * sb) * scale
    s = (sa * cb + ca * sb) * scale
    return c.reshape(n, n).astype(BF16), s.reshape(n, n).astype(BF16)


def _dft_seq_kernel(cs_ref, ss_ref, tc_ref, ts_ref, o_ref):
    acc = jnp.dot(cs_ref[...], tc_ref[...], preferred_element_type=F32)
    acc -= jnp.dot(ss_ref[...], ts_ref[...], preferred_element_type=F32)
    o_ref[...] = acc.astype(o_ref.dtype)


def dft_seq(cs, ss, t3, gd, tm):
    bsz, s, w = t3.shape
    ng = w // (2 * gd)
    return pl.pallas_call(
        _dft_seq_kernel,
        grid=(bsz, ng, s // tm),
        in_specs=[pl.BlockSpec((tm, s), lambda b, g, i: (i, 0)),
                  pl.BlockSpec((tm, s), lambda b, g, i: (i, 0)),
                  pl.BlockSpec((None, s, gd), lambda b, g, i: (b, 0, 2 * g)),
                  pl.BlockSpec((None, s, gd), lambda b, g, i: (b, 0, 2 * g + 1))],
        out_specs=pl.BlockSpec((None, tm, gd), lambda b, g, i: (b, i, g)),
        out_shape=jax.ShapeDtypeStruct((bsz, s, ng * gd), BF16),
        compiler_params=_params("parallel", "parallel", "arbitrary"),
        name="dft_seq",
    )(cs, ss, t3, t3)


def _tile(n, pref):
    return pref if n % pref == 0 else n


def _layer(x, xn, p_i, lw, g_next, *, bsz, seq):
    m, d = x.shape
    nh = M_HEADS
    dv = d // nh
    dk = dv // 2
    qk_w, v_w = nh * dk, nh * dv
    gd = d // F_GROUPS
    n_main = 2 * qk_w + 2 * v_w
    g0 = n_main
    z0 = n_main + 4 * nh
    chunk = min(MLSTM_CHUNK, seq)
    tm = _tile(m, 1024)

    li = lw["layer"]
    tn = _tile(d, 512)
    w_in_t = lw["w_in_t"]
    u = dot_wt(xn, w_in_t, li, lambda j: j * tn + jnp.where(j * tn >= g0, z0 - g0, 0),
               n_main + d, BF16, tm, tn, "in_proj")
    graw = dot_wt(xn, w_in_t, li, lambda j: g0, 128, F32, tm, 128, "gate_proj")

    graw = graw[:, :4 * nh].reshape(bsz, seq, 2, 2, nh)
    gi = graw[:, :, :, 0].reshape(bsz, seq, 2 * nh)
    gf = graw[:, :, :, 1].reshape(bsz, seq, 2 * nh)
    b_arr, e_arr, r_arr = gate_prep(gi, gf, lw["b_igate"][li].reshape(1, 2 * nh),
                                    lw["b_fgate"][li].reshape(1, 2 * nh), chunk)
    b4 = b_arr.reshape(bsz, seq, 2, nh)
    e4 = e_arr.reshape(bsz, seq, 2, nh)
    gcol = jnp.stack([b4[:, :, 0], e4[:, :, 0], b4[:, :, 1], e4[:, :, 1]], axis=-1)
    gcol = gcol.transpose(0, 2, 1, 3)
    grow = r_arr.reshape(bsz, seq // chunk, chunk, 2, nh).transpose(0, 4, 3, 1, 2)
    grow = grow[:, :, :, :, None, :]
    u3 = u.reshape(bsz, seq, n_main + d)
    hg = mlstm(u3, gcol, grow, lw["g_head"][li], dk=dk, dv=dv, q_cb0=0, k_cb0=qk_w // dk,
               v_cb0=2 * qk_w // dv, o_cb0=(2 * qk_w + v_w) // dv, chunk=chunk)
    hg = hg.reshape(m, v_w)

    cc, sc_ = _dft_mats(gd, gd ** -0.5)
    cs, ss = _dft_mats(seq, seq ** -0.5)
    w_chan = jnp.concatenate([cc, sc_], axis=1)
    t = fused_dots([(u, n_main // gd, 1, w_chan, None, 0, 0)], [], _ep_identity, (m, 2 * d), BF16,
                   tm, 2 * gd, "dft_chan")
    zf = dft_seq(cs, ss, t.reshape(bsz, seq, 2 * d), gd, _tile(seq, 256)).reshape(m, d)

    gates = fused_dots([(xn, 0, 0, lw["w_merge"], li, 0, 1)],
                       [(lw["b_merge"][li].reshape(1, 2 * d), "row", 0)],
                       _ep_bias_sigmoid, (m, 2 * d), BF16, tm, tn, "merge_gate")
    ym = fused_dots([(hg, 0, 0, lw["w_branch_m"], li, 0, 1)], [(gates, "tile", 0)],
                    _ep_gate_mul, (m, d), BF16, tm, tn, "branch_m")
    mix = fused_dots([(zf, 0, 0, lw["w_branch_f"], li, 0, 1)], [(gates, "tile", d // tn), (ym, "tile", 0)],
                     _ep_gate_mul_add, (m, d), BF16, tm, tn, "branch_f")
    y = fused_dots([(mix, 0, 0, lw["w_out"], li, 0, 1)], [], _ep_identity, (m, d), BF16, tm, tn, "out_proj")
    x, hn = resnorm(x, y, lw["g_mix_post"][li], lw["g_mlp_pre"][li])

    d_ff = lw["w_up"].shape[-1]
    up = fused_dots([(hn, 0, 0, lw["w_up"], li, 0, 1)], [], _ep_relu2, (m, d_ff), BF16, tm, tn, "mlp_up")
    w_down = cast_rows(lw["w_down"], li, BF16, _tile(d_ff, 512))
    y = matmul_ktiled(up, w_down, BF16, tm, _tile(d, 1024), _tile(d_ff, 4096), "mlp_down")
    x, xb = resnorm(x, y, lw["g_mlp_post"][li], None)

    ple = fused_dots([(xb, 0, 0, lw["w_ple_gate"], li, 0, 1), (p_i, 0, 0, lw["w_ple_in"], li, 0, 1)],
                     [], _ep_sigmoid_mul, (m, d), BF16, tm, tn, "ple")
    if g_next is None:
        x, _ = resnorm(x, ple, lw["g_ple_post"][li], None)
        return x, None
    return resnorm(x, ple, lw["g_ple_post"][li], g_next)


def kernel(x, p, g_mix_pre, g_mix_post, g_mlp_pre, g_mlp_post, g_ple_post, w_in, b_igate, b_fgate,
           g_head, w_branch_m, w_branch_f, w_merge, b_merge, w_out, w_up, w_down, w_ple_in, w_ple_gate):
    bsz, seq, d = x.shape
    depth = w_in.shape[0]
    m = bsz * seq
    xf = x.reshape(m, d)
    xn = rmsnorm(xf, g_mix_pre[0])
    lw = dict(w_in_t=jnp.swapaxes(w_in, 1, 2), b_igate=b_igate, b_fgate=b_fgate, g_head=g_head, w_branch_m=w_branch_m,
              w_branch_f=w_branch_f, w_merge=w_merge, b_merge=b_merge, w_out=w_out, w_up=w_up,
              w_down=w_down, w_ple_in=w_ple_in, w_ple_gate=w_ple_gate, g_mix_post=g_mix_post,
              g_mlp_pre=g_mlp_pre, g_mlp_post=g_mlp_post, g_ple_post=g_ple_post)
    for i in range(depth):
        g_next = g_mix_pre[i + 1] if i + 1 < depth else None
        xf, xn = _layer(xf, xn, p[i].reshape(m, -1), dict(lw, layer=i), g_next, bsz=bsz, seq=seq)
    return xf.reshape(bsz, seq, d)
```

```python
import functools
import math

import jax
import jax.numpy as jnp
from jax import lax
from jax.experimental import pallas as pl
from jax.experimental.pallas import tpu as pltpu

F32 = jnp.float32
BF16 = jnp.bfloat16

M_HEADS = 8
F_GROUPS = 4
GATE_CAP = 15.0
EPS = 1e-6
MLSTM_CHUNK = 256
V7X_VMEM_BYTES = 64 * 1024 * 1024
VMEM_LIMIT = V7X_VMEM_BYTES - 8 * 1024 * 1024


def _params(*semantics):
    return pltpu.CompilerParams(dimension_semantics=semantics, vmem_limit_bytes=VMEM_LIMIT)


def _rms(x, g):
    return x * lax.rsqrt(jnp.mean(x * x, axis=-1, keepdims=True) + EPS) * g


def _rmsnorm_kernel(x_ref, g_ref, o_ref):
    o_ref[...] = _rms(x_ref[...], g_ref[...]).astype(o_ref.dtype)


def rmsnorm(x, g, tm=256):
    m, d = x.shape
    return pl.pallas_call(
        _rmsnorm_kernel,
        grid=(m // tm,),
        in_specs=[pl.BlockSpec((tm, d), lambda i: (i, 0)),
                  pl.BlockSpec((1, d), lambda i: (0, 0))],
        out_specs=pl.BlockSpec((tm, d), lambda i: (i, 0)),
        out_shape=jax.ShapeDtypeStruct((m, d), BF16),
        compiler_params=_params("parallel"),
        name="rmsnorm",
    )(x, g.reshape(1, d))


def _resnorm_kernel(x_ref, y_ref, gp_ref, gn_ref, xo_ref, xn_ref, *, norm_next):
    xnew = x_ref[...] + _rms(y_ref[...].astype(F32), gp_ref[...])
    xo_ref[...] = xnew
    if norm_next:
        xn_ref[...] = _rms(xnew, gn_ref[...]).astype(xn_ref.dtype)
    else:
        xn_ref[...] = xnew.astype(xn_ref.dtype)


def resnorm(x, y, g_post, g_next, tm=256):
    m, d = x.shape
    norm_next = g_next is not None
    gn = g_next if norm_next else g_post
    row = pl.BlockSpec((tm, d), lambda i: (i, 0))
    vec = pl.BlockSpec((1, d), lambda i: (0, 0))
    return pl.pallas_call(
        functools.partial(_resnorm_kernel, norm_next=norm_next),
        grid=(m // tm,),
        in_specs=[row, row, vec, vec],
        out_specs=[row, row],
        out_shape=[jax.ShapeDtypeStruct((m, d), F32), jax.ShapeDtypeStruct((m, d), BF16)],
        compiler_params=_params("parallel"),
        name="resnorm",
    )(x, y, g_post.reshape(1, d), gn.reshape(1, d))


def _dots_kernel(*refs, n_dots, n_extra, cast, epilogue):
    out_ref = refs[2 * n_dots + n_extra]
    scratch = list(refs[2 * n_dots + n_extra + 1:])
    w_refs = []
    for p in range(n_dots):
        w_refs.append(scratch.pop(0) if cast[p] else refs[2 * p + 1])

    @pl.when(pl.program_id(1) == 0)
    def _():
        for p in range(n_dots):
            if cast[p]:
                w_refs[p][...] = refs[2 * p + 1][...].astype(BF16)

    accs = [jnp.dot(refs[2 * p][...].astype(BF16), w_refs[p][...], preferred_element_type=F32)
            for p in range(n_dots)]
    extras = [r[...] for r in refs[2 * n_dots:2 * n_dots + n_extra]]
    out_ref[...] = epilogue(accs, extras).astype(out_ref.dtype)


def fused_dots(pairs, extras, epilogue, out_shape, out_dtype, tm, tn, name):
    m, n = out_shape
    in_specs, args, scratch, cast = [], [], [], []
    for a, a_cb, a_step, w, layer, w_cb, w_step in pairs:
        k = w.shape[-2]
        in_specs.append(pl.BlockSpec((tm, k), lambda j, i, c=a_cb, s=a_step: (i, c + j * s)))
        if layer is None:
            in_specs.append(pl.BlockSpec((k, tn), lambda j, i, c=w_cb, s=w_step: (0, c + j * s)))
        else:
            in_specs.append(pl.BlockSpec((None, k, tn),
                                         lambda j, i, l=layer, c=w_cb, s=w_step: (l, 0, c + j * s)))
        args += [a, w]
        cast.append(w.dtype != BF16)
        if cast[-1]:
            scratch.append(pltpu.VMEM((k, tn), BF16))
    for arr, kind, cb0 in extras:
        if kind == "row":
            in_specs.append(pl.BlockSpec((1, tn), lambda j, i, cb0=cb0: (0, cb0 + j)))
        else:
            in_specs.append(pl.BlockSpec((tm, tn), lambda j, i, cb0=cb0: (i, cb0 + j)))
        args.append(arr)
    return pl.pallas_call(
        functools.partial(_dots_kernel, n_dots=len(pairs), n_extra=len(extras), cast=tuple(cast),
                          epilogue=epilogue),
        grid=(n // tn, m // tm),
        in_specs=in_specs,
        out_specs=pl.BlockSpec((tm, tn), lambda j, i: (i, j)),
        out_shape=jax.ShapeDtypeStruct((m, n), out_dtype),
        scratch_shapes=scratch,
        compiler_params=_params("parallel", "arbitrary"),
        name=name,
    )(*args)


def _dot_wt_kernel(a_ref, wt_ref, o_ref, wbf_ref):
    @pl.when(pl.program_id(1) == 0)
    def _():
        wbf_ref[...] = wt_ref[0].T.astype(BF16)

    o_ref[...] = jnp.dot(a_ref[...], wbf_ref[...], preferred_element_type=F32).astype(o_ref.dtype)


def dot_wt(a, wt, layer, row_offset, n_out, out_dtype, tm, tn, name):
    m, k = a.shape
    return pl.pallas_call(
        _dot_wt_kernel,
        grid=(n_out // tn, m // tm),
        in_specs=[pl.BlockSpec((tm, k), lambda j, i: (i, 0)),
                  pl.BlockSpec((pl.Element(1), pl.Element(tn), pl.Element(k)),
                               lambda j, i: (layer, pl.multiple_of(row_offset(j), 8), 0))],
        out_specs=pl.BlockSpec((tm, tn), lambda j, i: (i, j)),
        out_shape=jax.ShapeDtypeStruct((m, n_out), out_dtype),
        scratch_shapes=[pltpu.VMEM((k, tn), BF16)],
        compiler_params=_params("parallel", "arbitrary"),
        name=name,
    )(a, wt)


def _cast_kernel(x_ref, o_ref):
    o_ref[...] = x_ref[...].astype(o_ref.dtype)


def cast_rows(w, layer, dtype, tr):
    _, k, n = w.shape
    return pl.pallas_call(
        _cast_kernel,
        grid=(k // tr,),
        in_specs=[pl.BlockSpec((None, tr, n), lambda r, l=layer: (l, r, 0))],
        out_specs=pl.BlockSpec((tr, n), lambda r: (r, 0)),
        out_shape=jax.ShapeDtypeStruct((k, n), dtype),
        compiler_params=_params("parallel"),
        name="cast_rows",
    )(w)


def _ep_identity(accs, extras):
    return accs[0]


def _ep_bias_sigmoid(accs, extras):
    return jax.nn.sigmoid(accs[0] + extras[0])


def _ep_relu2(accs, extras):
    return jnp.square(jnp.maximum(accs[0], 0.0))


def _ep_gate_mul(accs, extras):
    return extras[0].astype(F32) * accs[0]


def _ep_gate_mul_add(accs, extras):
    return extras[0].astype(F32) * accs[0] + extras[1].astype(F32)


def _ep_sigmoid_mul(accs, extras):
    return jax.nn.sigmoid(accs[0]) * accs[1]


def _ep_sub(accs, extras):
    return accs[0] - accs[1]


def _matmul_ktiled_kernel(a_ref, w_ref, o_ref, acc_ref):
    kk = pl.program_id(2)

    @pl.when(kk == 0)
    def _():
        acc_ref[...] = jnp.zeros_like(acc_ref)

    acc_ref[...] += jnp.dot(a_ref[...], w_ref[...], preferred_element_type=F32)

    @pl.when(kk == pl.num_programs(2) - 1)
    def _():
        o_ref[...] = acc_ref[...].astype(o_ref.dtype)


def matmul_ktiled(a, w, out_dtype, tm, tn, tk, name):
    m, k = a.shape
    n = w.shape[1]
    return pl.pallas_call(
        _matmul_ktiled_kernel,
        grid=(m // tm, n // tn, k // tk),
        in_specs=[pl.BlockSpec((tm, tk), lambda i, j, kk: (i, kk)),
                  pl.BlockSpec((tk, tn), lambda i, j, kk: (kk, j))],
        out_specs=pl.BlockSpec((tm, tn), lambda i, j, kk: (i, j)),
        out_shape=jax.ShapeDtypeStruct((m, n), out_dtype),
        scratch_shapes=[pltpu.VMEM((tm, tn), F32)],
        compiler_params=_params("parallel", "parallel", "arbitrary"),
        name=name,
    )(a, w)


def _split3(x):
    hi = x.astype(BF16)
    r1 = x - hi.astype(F32)
    mid = r1.astype(BF16)
    lo = (r1 - mid.astype(F32)).astype(BF16)
    return hi, mid, lo


def _gate_prep_kernel(gi_ref, gf_ref, bi_ref, bf_ref, b_ref, e_ref, r_ref, *, chunk):
    s, w = gi_ref.shape
    nh = w // 2
    log_i = GATE_CAP * jnp.tanh((gi_ref[...] + bi_ref[...]) * (1.0 / GATE_CAP))
    af = GATE_CAP * jnp.tanh((gf_ref[...] + bf_ref[...]) * (1.0 / GATE_CAP))
    log_f = jnp.minimum(af, 0.0) - jnp.log1p(jnp.exp(-jnp.abs(af)))
    ri = lax.broadcasted_iota(jnp.int32, (chunk, chunk), 0)
    ci = lax.broadcasted_iota(jnp.int32, (chunk, chunk), 1)
    tril = (ci <= ri).astype(BF16)
    triu = (ci >= ri).astype(BF16)
    is_fwd = lax.broadcasted_iota(jnp.int32, (chunk, w), 1) < nh
    for c in range(s // chunk):
        rows = slice(c * chunk, (c + 1) * chunk)
        parts = _split3(log_f[rows, :])
        cum_f = sum(jnp.dot(tril, p, preferred_element_type=F32) for p in parts)
        cum_b = sum(jnp.dot(triu, p, preferred_element_type=F32) for p in parts)
        b = jnp.where(is_fwd, cum_f, cum_b)
        tot = jnp.where(is_fwd[:1], cum_f[chunk - 1:chunk, :], cum_b[0:1, :])
        li = log_i[rows, :]
        b_ref[rows, :] = b
        e_ref[rows, :] = tot - b + li
        r_ref[rows, :] = li - b


def gate_prep(gi, gf, bi, bf, chunk):
    bsz, s, w = gi.shape
    blk = pl.BlockSpec((None, s, w), lambda b: (b, 0, 0))
    vec = pl.BlockSpec((1, w), lambda b: (0, 0))
    out = jax.ShapeDtypeStruct((bsz, s, w), F32)
    return pl.pallas_call(
        functools.partial(_gate_prep_kernel, chunk=chunk),
        grid=(bsz,),
        in_specs=[blk, blk, vec, vec],
        out_specs=[blk, blk, blk],
        out_shape=[out, out, out],
        compiler_params=_params("parallel"),
        name="gate_prep",
    )(gi, gf, bi, bf)


def _mlstm_kernel(q_ref, k_ref, v_ref, o_ref, gcol_ref, grow_ref, gh_ref, out_ref,
                  hf_ref, c_ref, n_ref, *, chunk, q_scale):
    s = q_ref.shape[0]
    nc = s // chunk
    ri = lax.broadcasted_iota(jnp.int32, (chunk, chunk), 0)
    ci = lax.broadcasted_iota(jnp.int32, (chunk, chunk), 1)

    for d in (0, 1):
        mask = (ci <= ri) if d == 0 else (ci >= ri)
        c_ref[...] = jnp.zeros_like(c_ref)
        n_ref[...] = jnp.zeros_like(n_ref)

        def body(t, carry, d=d, mask=mask):
            c = t if d == 0 else nc - 1 - t
            rows = pl.ds(pl.multiple_of(c * chunk, chunk), chunk)
            q = q_ref[rows, :] * q_scale
            k = k_ref[rows, :]
            v = v_ref[rows, :]
            gc = gcol_ref[rows, :]
            bcol = gc[:, 2 * d:2 * d + 1]
            ecol = gc[:, 2 * d + 1:2 * d + 2]
            rrow = grow_ref[d, c]
            last = chunk - 1 if d == 0 else 0
            decay = jnp.exp(bcol[last:last + 1, :])

            w_intra = jnp.exp(jnp.where(mask, bcol + rrow, -jnp.inf))
            w_inter = jnp.exp(bcol)
            sc = lax.dot_general(q, k, (((1,), (1,)), ((), ())), preferred_element_type=F32) * w_intra
            c_prev = c_ref[...]
            n_prev = n_ref[...]
            inter = jnp.dot(q, c_prev.astype(BF16), preferred_element_type=F32)
            num = jnp.dot(sc.astype(BF16), v, preferred_element_type=F32) + w_inter * inter
            qn = jnp.sum(q.astype(F32) * n_prev, axis=1, keepdims=True)
            den = jnp.sum(sc, axis=1, keepdims=True) + w_inter * qn
            h = num * (1.0 / jnp.maximum(jnp.abs(den), 1.0))

            kw = k.astype(F32) * jnp.exp(ecol)
            c_ref[...] = decay * c_prev + lax.dot_general(
                kw.astype(BF16), v, (((0,), (0,)), ((), ())), preferred_element_type=F32)
            n_ref[...] = decay * n_prev + jnp.sum(kw, axis=0, keepdims=True)

            if d == 0:
                hf_ref[rows, :] = h
            else:
                tot = hf_ref[rows, :] + h
                y = _rms(tot, gh_ref[...]) * jax.nn.sigmoid(o_ref[rows, :].astype(F32))
                out_ref[rows, :] = y.astype(out_ref.dtype)
            return carry

        lax.fori_loop(0, nc, body, 0)


def mlstm(u3, gcol, grow, g_head, *, dk, dv, q_cb0, k_cb0, v_cb0, o_cb0, chunk):
    bsz, s, _ = u3.shape
    nh = gcol.shape[1]
    nc = s // chunk

    def col(width, cb0):
        return pl.BlockSpec((None, s, width), lambda b, h, cb0=cb0: (b, 0, cb0 + h))

    return pl.pallas_call(
        functools.partial(_mlstm_kernel, chunk=chunk, q_scale=dk ** -0.5),
        grid=(bsz, nh),
        in_specs=[col(dk, q_cb0), col(dk, k_cb0), col(dv, v_cb0), col(dv, o_cb0),
                  pl.BlockSpec((None, None, s, 4), lambda b, h: (b, h, 0, 0)),
                  pl.BlockSpec((None, None, 2, nc, 1, chunk), lambda b, h: (b, h, 0, 0, 0, 0)),
                  pl.BlockSpec((1, dv), lambda b, h: (0, h))],
        out_specs=pl.BlockSpec((None, s, dv), lambda b, h: (b, 0, h)),
        out_shape=jax.ShapeDtypeStruct((bsz, s, nh * dv), BF16),
        scratch_shapes=[pltpu.VMEM((s, dv), F32), pltpu.VMEM((dk, dv), F32), pltpu.VMEM((1, dk), F32)],
        compiler_params=_params("parallel", "parallel"),
        name="mlstm",
    )(u3, u3, u3, u3, gcol, grow, g_head.reshape(1, nh * dv))


def _cos_sin(rows, n_cols, period):
    k = jnp.arange(n_cols, dtype=jnp.int32)
    ang = ((rows[:, None] * k[None, :]) % period).astype(F32) * (2.0 * math.pi / period)
    return jnp.cos(ang), jnp.sin(ang)


def _dft_mats(n, scale):
    n1 = 1 << ((n.bit_length() - 1) // 2)
    n2 = n // n1
    ca, sa = _cos_sin(jnp.arange(n1, dtype=jnp.int32), n, n1)
    cb, sb = _cos_sin(jnp.arange(n2, dtype=jnp.int32), n, n)
    ca, sa, cb, sb = ca[:, None, :], sa[:, None, :], cb[None, :, :], sb[None, :, :]
    c = (ca * cb - sa * sb) * scale
    s = (sa * cb + ca * sb) * scale
    return c.reshape(n, n).astype(BF16), s.reshape(n, n).astype(BF16)


def _dft_chan_kernel(zlo_ref, zrev_ref, cc_ref, sc_ref, p_ref, q_ref):
    a = zlo_ref[...]
    r = zrev_ref[...]
    p_ref[...] = jnp.dot(a + r, cc_ref[...], preferred_element_type=F32).astype(p_ref.dtype)
    q_ref[...] = jnp.dot(a - r, sc_ref[...], preferred_element_type=F32).astype(q_ref.dtype)


def dft_chan(u3, z_cb0, zrev, cc, sc, hp):
    bsz = u3.shape[0]
    gd = cc.shape[0]
    ng = zrev.shape[2] // gd
    out = jax.ShapeDtypeStruct((bsz, hp, ng * gd), BF16)
    blk = pl.BlockSpec((None, hp, gd), lambda b, g: (b, 0, g))
    mat = pl.BlockSpec((gd, gd), lambda b, g: (0, 0))
    return pl.pallas_call(
        _dft_chan_kernel,
        grid=(bsz, ng),
        in_specs=[pl.BlockSpec((None, hp, gd), lambda b, g: (b, 0, z_cb0 + g)), blk, mat, mat],
        out_specs=[blk, blk],
        out_shape=[out, out],
        compiler_params=_params("parallel", "parallel"),
        name="dft_chan",
    )(u3, zrev, cc, sc)


def _dft_seq_kernel(cs_ref, ss_ref, p_ref, q_ref, lo_ref, hi_ref):
    a = jnp.dot(cs_ref[...], p_ref[...], preferred_element_type=F32)
    b = jnp.dot(ss_ref[...], q_ref[...], preferred_element_type=F32)
    lo_ref[...] = (a - b).astype(lo_ref.dtype)
    hi_ref[...] = (a + b).astype(hi_ref.dtype)


def dft_seq(cs, ss, p, q, gd, tm):
    bsz, hp, w = p.shape
    out = jax.ShapeDtypeStruct((bsz, hp, w), BF16)
    mat = pl.BlockSpec((tm, hp), lambda b, g, i: (i, 0))
    src = pl.BlockSpec((None, hp, gd), lambda b, g, i: (b, 0, g))
    dst = pl.BlockSpec((None, tm, gd), lambda b, g, i: (b, i, g))
    return pl.pallas_call(
        _dft_seq_kernel,
        grid=(bsz, w // gd, hp // tm),
        in_specs=[mat, mat, src, src],
        out_specs=[dst, dst],
        out_shape=[out, out],
        compiler_params=_params("parallel", "parallel", "arbitrary"),
        name="dft_seq",
    )(cs, ss, p, q)


def fourier_mix(u3, z_cb0, gd):
    bsz, n, _ = u3.shape
    h = n // 2
    hp = min(-(-(h + 1) // 256) * 256, n)
    d = F_GROUPS * gd
    z = u3[:, :, z_cb0 * gd:]
    zrev = jnp.concatenate([jnp.zeros((bsz, 1, d), BF16), jnp.flip(z[:, h + 1:], axis=1),
                            jnp.zeros((bsz, hp - h, d), BF16)], axis=1)
    cc, sc = _dft_mats(gd, gd ** -0.5)
    cs, ss = _dft_mats(n, n ** -0.5)
    pad = ((0, hp - h - 1), (0, hp - h - 1))
    cs = jnp.pad(cs[:h + 1, :h + 1], pad)
    ss = jnp.pad(ss[:h + 1, :h + 1], pad)
    p, q = dft_chan(u3, z_cb0, zrev, cc, sc, hp)
    tm = hp // 3 if hp % 24 == 0 else hp
    lo, hi = dft_seq(cs, ss, p, q, gd, tm)
    return jnp.concatenate([lo[:, :h + 1], jnp.flip(hi[:, 1:h], axis=1)], axis=1)


def _tile(n, pref):
    return pref if n % pref == 0 else n


def _layer(x, xn, p_i, lw, g_next, *, bsz, seq):
    m, d = x.shape
    nh = M_HEADS
    dv = d // nh
    dk = dv // 2
    qk_w, v_w = nh * dk, nh * dv
    gd = d // F_GROUPS
    n_main = 2 * qk_w + 2 * v_w
    g0 = n_main
    z0 = n_main + 4 * nh
    chunk = min(MLSTM_CHUNK, seq)
    tm = _tile(m, 1024)

    li = lw["layer"]
    tn = _tile(d, 512)
    w_in_t = lw["w_in_t"]
    u = dot_wt(xn, w_in_t, li, lambda j: j * tn + jnp.where(j * tn >= g0, z0 - g0, 0),
               n_main + d, BF16, tm, tn, "in_proj")
    graw = dot_wt(xn, w_in_t, li, lambda j: g0, 128, F32, tm, 128, "gate_proj")

    graw = graw[:, :4 * nh].reshape(bsz, seq, 2, 2, nh)
    gi = graw[:, :, :, 0].reshape(bsz, seq, 2 * nh)
    gf = graw[:, :, :, 1].reshape(bsz, seq, 2 * nh)
    b_arr, e_arr, r_arr = gate_prep(gi, gf, lw["b_igate"][li].reshape(1, 2 * nh),
                                    lw["b_fgate"][li].reshape(1, 2 * nh), chunk)
    b4 = b_arr.reshape(bsz, seq, 2, nh)
    e4 = e_arr.reshape(bsz, seq, 2, nh)
    gcol = jnp.stack([b4[:, :, 0], e4[:, :, 0], b4[:, :, 1], e4[:, :, 1]], axis=-1)
    gcol = gcol.transpose(0, 2, 1, 3)
    grow = r_arr.reshape(bsz, seq // chunk, chunk, 2, nh).transpose(0, 4, 3, 1, 2)
    grow = grow[:, :, :, :, None, :]
    u3 = u.reshape(bsz, seq, n_main + d)
    hg = mlstm(u3, gcol, grow, lw["g_head"][li], dk=dk, dv=dv, q_cb0=0, k_cb0=qk_w // dk,
               v_cb0=2 * qk_w // dv, o_cb0=(2 * qk_w + v_w) // dv, chunk=chunk)
    hg = hg.reshape(m, v_w)

    zf = fourier_mix(u3, n_main // gd, gd).reshape(m, d)

    gates = fused_dots([(xn, 0, 0, lw["w_merge"], li, 0, 1)],
                       [(lw["b_merge"][li].reshape(1, 2 * d), "row", 0)],
                       _ep_bias_sigmoid, (m, 2 * d), BF16, tm, tn, "merge_gate")
    ym = fused_dots([(hg, 0, 0, lw["w_branch_m"], li, 0, 1)], [(gates, "tile", 0)],
                    _ep_gate_mul, (m, d), BF16, tm, tn, "branch_m")
    mix = fused_dots([(zf, 0, 0, lw["w_branch_f"], li, 0, 1)], [(gates, "tile", d // tn), (ym, "tile", 0)],
                     _ep_gate_mul_add, (m, d), BF16, tm, tn, "branch_f")
    y = fused_dots([(mix, 0, 0, lw["w_out"], li, 0, 1)], [], _ep_identity, (m, d), BF16, tm, tn, "out_proj")
    x, hn = resnorm(x, y, lw["g_mix_post"][li], lw["g_mlp_pre"][li])

    d_ff = lw["w_up"].shape[-1]
    up = fused_dots([(hn, 0, 0, lw["w_up"], li, 0, 1)], [], _ep_relu2, (m, d_ff), BF16, tm, tn, "mlp_up")
    w_down = cast_rows(lw["w_down"], li, BF16, _tile(d_ff, 512))
    y = matmul_ktiled(up, w_down, BF16, tm, _tile(d, 1024), _tile(d_ff, 4096), "mlp_down")
    x, xb = resnorm(x, y, lw["g_mlp_post"][li], None)

    ple = fused_dots([(xb, 0, 0, lw["w_ple_gate"], li, 0, 1), (p_i, 0, 0, lw["w_ple_in"], li, 0, 1)],
                     [], _ep_sigmoid_mul, (m, d), BF16, tm, tn, "ple")
    if g_next is None:
        x, _ = resnorm(x, ple, lw["g_ple_post"][li], None)
        return x, None
    return resnorm(x, ple, lw["g_ple_post"][li], g_next)


def kernel(x, p, g_mix_pre, g_mix_post, g_mlp_pre, g_mlp_post, g_ple_post, w_in, b_igate, b_fgate,
           g_head, w_branch_m, w_branch_f, w_merge, b_merge, w_out, w_up, w_down, w_ple_in, w_ple_gate):
    bsz, seq, d = x.shape
    depth = w_in.shape[0]
    m = bsz * seq
    xf = x.reshape(m, d)
    xn = rmsnorm(xf, g_mix_pre[0])
    lw = dict(w_in_t=jnp.swapaxes(w_in, 1, 2), b_igate=b_igate, b_fgate=b_fgate, g_head=g_head, w_branch_m=w_branch_m,
              w_branch_f=w_branch_f, w_merge=w_merge, b_merge=b_merge, w_out=w_out, w_up=w_up,
              w_down=w_down, w_ple_in=w_ple_in, w_ple_gate=w_ple_gate, g_mix_post=g_mix_post,
              g_mlp_pre=g_mlp_pre, g_mlp_post=g_mlp_post, g_ple_post=g_ple_post)
    for i in range(depth):
        g_next = g_mix_pre[i + 1] if i + 1 < depth else None
        xf, xn = _layer(xf, xn, p[i].reshape(m, -1), dict(lw, layer=i), g_next, bsz=bsz, seq=seq)
    return xf.reshape(bsz, seq, d)
```

```python
import functools
import math

import jax
import jax.numpy as jnp
from jax import lax
from jax.experimental import pallas as pl
from jax.experimental.pallas import tpu as pltpu

F32 = jnp.float32
BF16 = jnp.bfloat16

M_HEADS = 8
F_GROUPS = 4
GATE_CAP = 15.0
EPS = 1e-6
MLSTM_CHUNK = 256
V7X_VMEM_BYTES = 64 * 1024 * 1024
VMEM_LIMIT = V7X_VMEM_BYTES - 8 * 1024 * 1024


def _params(*semantics):
    return pltpu.CompilerParams(dimension_semantics=semantics, vmem_limit_bytes=VMEM_LIMIT)


def _rms(x, g):
    return x * lax.rsqrt(jnp.mean(x * x, axis=-1, keepdims=True) + EPS) * g


def _rmsnorm_kernel(x_ref, g_ref, o_ref):
    o_ref[...] = _rms(x_ref[...], g_ref[...]).astype(o_ref.dtype)


def rmsnorm(x, g, tm=256):
    m, d = x.shape
    return pl.pallas_call(
        _rmsnorm_kernel,
        grid=(m // tm,),
        in_specs=[pl.BlockSpec((tm, d), lambda i: (i, 0)),
                  pl.BlockSpec((1, d), lambda i: (0, 0))],
        out_specs=pl.BlockSpec((tm, d), lambda i: (i, 0)),
        out_shape=jax.ShapeDtypeStruct((m, d), BF16),
        compiler_params=_params("parallel"),
        name="rmsnorm",
    )(x, g.reshape(1, d))


def _resnorm_kernel(x_ref, y_ref, gp_ref, gn_ref, xo_ref, xn_ref, *, norm_next):
    xnew = x_ref[...] + _rms(y_ref[...].astype(F32), gp_ref[...])
    xo_ref[...] = xnew
    if norm_next:
        xn_ref[...] = _rms(xnew, gn_ref[...]).astype(xn_ref.dtype)
    else:
        xn_ref[...] = xnew.astype(xn_ref.dtype)


def resnorm(x, y, g_post, g_next, tm=256):
    m, d = x.shape
    norm_next = g_next is not None
    gn = g_next if norm_next else g_post
    row = pl.BlockSpec((tm, d), lambda i: (i, 0))
    vec = pl.BlockSpec((1, d), lambda i: (0, 0))
    return pl.pallas_call(
        functools.partial(_resnorm_kernel, norm_next=norm_next),
        grid=(m // tm,),
        in_specs=[row, row, vec, vec],
        out_specs=[row, row],
        out_shape=[jax.ShapeDtypeStruct((m, d), F32), jax.ShapeDtypeStruct((m, d), BF16)],
        compiler_params=_params("parallel"),
        name="resnorm",
    )(x, y, g_post.reshape(1, d), gn.reshape(1, d))


def _dots_kernel(*refs, n_dots, n_extra, cast, epilogue):
    out_ref = refs[2 * n_dots + n_extra]
    scratch = list(refs[2 * n_dots + n_extra + 1:])
    w_refs = []
    for p in range(n_dots):
        w_refs.append(scratch.pop(0) if cast[p] else refs[2 * p + 1])

    @pl.when(pl.program_id(1) == 0)
    def _():
        for p in range(n_dots):
            if cast[p]:
                w_refs[p][...] = refs[2 * p + 1][...].astype(BF16)

    accs = [jnp.dot(refs[2 * p][...].astype(BF16), w_refs[p][...], preferred_element_type=F32)
            for p in range(n_dots)]
    extras = [r[...] for r in refs[2 * n_dots:2 * n_dots + n_extra]]
    out_ref[...] = epilogue(accs, extras).astype(out_ref.dtype)


def fused_dots(pairs, extras, epilogue, out_shape, out_dtype, tm, tn, name):
    m, n = out_shape
    in_specs, args, scratch, cast = [], [], [], []
    for a, a_cb, a_step, w, layer, w_cb, w_step in pairs:
        k = w.shape[-2]
        in_specs.append(pl.BlockSpec((tm, k), lambda j, i, c=a_cb, s=a_step: (i, c + j * s)))
        if layer is None:
            in_specs.append(pl.BlockSpec((k, tn), lambda j, i, c=w_cb, s=w_step: (0, c + j * s)))
        else:
            in_specs.append(pl.BlockSpec((None, k, tn),
                                         lambda j, i, l=layer, c=w_cb, s=w_step: (l, 0, c + j * s)))
        args += [a, w]
        cast.append(w.dtype != BF16)
        if cast[-1]:
            scratch.append(pltpu.VMEM((k, tn), BF16))
    for arr, kind, cb0 in extras:
        if kind == "row":
            in_specs.append(pl.BlockSpec((1, tn), lambda j, i, cb0=cb0: (0, cb0 + j)))
        else:
            in_specs.append(pl.BlockSpec((tm, tn), lambda j, i, cb0=cb0: (i, cb0 + j)))
        args.append(arr)
    return pl.pallas_call(
        functools.partial(_dots_kernel, n_dots=len(pairs), n_extra=len(extras), cast=tuple(cast),
                          epilogue=epilogue),
        grid=(n // tn, m // tm),
        in_specs=in_specs,
        out_specs=pl.BlockSpec((tm, tn), lambda j, i: (i, j)),
        out_shape=jax.ShapeDtypeStruct((m, n), out_dtype),
        scratch_shapes=scratch,
        compiler_params=_params("parallel", "arbitrary"),
        name=name,
    )(*args)


def _dot_wt_kernel(a_ref, wt_ref, o_ref, wbf_ref):
    @pl.when(pl.program_id(1) == 0)
    def _():
        wbf_ref[...] = wt_ref[0].T.astype(BF16)

    o_ref[...] = jnp.dot(a_ref[...], wbf_ref[...], preferred_element_type=F32).astype(o_ref.dtype)


def dot_wt(a, wt, layer, row_offset, n_out, out_dtype, tm, tn, name):
    m, k = a.shape
    return pl.pallas_call(
        _dot_wt_kernel,
        grid=(n_out // tn, m // tm),
        in_specs=[pl.BlockSpec((tm, k), lambda j, i: (i, 0)),
                  pl.BlockSpec((pl.Element(1), pl.Element(tn), pl.Element(k)),
                               lambda j, i: (layer, pl.multiple_of(row_offset(j), 8), 0))],
        out_specs=pl.BlockSpec((tm, tn), lambda j, i: (i, j)),
        out_shape=jax.ShapeDtypeStruct((m, n_out), out_dtype),
        scratch_shapes=[pltpu.VMEM((k, tn), BF16)],
        compiler_params=_params("parallel", "arbitrary"),
        name=name,
    )(a, wt)


def _cast_kernel(x_ref, o_ref):
    o_ref[...] = x_ref[...].astype(o_ref.dtype)


def cast_rows(w, layer, dtype, tr):
    _, k, n = w.shape
    return pl.pallas_call(
        _cast_kernel,
        grid=(k // tr,),
        in_specs=[pl.BlockSpec((None, tr, n), lambda r, l=layer: (l, r, 0))],
        out_specs=pl.BlockSpec((tr, n), lambda r: (r, 0)),
        out_shape=jax.ShapeDtypeStruct((k, n), dtype),
        compiler_params=_params("parallel"),
        name="cast_rows",
    )(w)


def _ep_identity(accs, extras):
    return accs[0]


def _ep_bias_sigmoid(accs, extras):
    return jax.nn.sigmoid(accs[0] + extras[0])


def _ep_relu2(accs, extras):
    return jnp.square(jnp.maximum(accs[0], 0.0))


def _ep_gate_mul(accs, extras):
    return extras[0].astype(F32) * accs[0]


def _ep_gate_mul_add(accs, extras):
    return extras[0].astype(F32) * accs[0] + extras[1].astype(F32)


def _ep_sigmoid_mul(accs, extras):
    return jax.nn.sigmoid(accs[0]) * accs[1]


def _ep_sub(accs, extras):
    return accs[0] - accs[1]


def _matmul_ktiled_kernel(a_ref, w_ref, o_ref, acc_ref):
    kk = pl.program_id(2)

    @pl.when(kk == 0)
    def _():
        acc_ref[...] = jnp.zeros_like(acc_ref)

    acc_ref[...] += jnp.dot(a_ref[...], w_ref[...], preferred_element_type=F32)

    @pl.when(kk == pl.num_programs(2) - 1)
    def _():
        o_ref[...] = acc_ref[...].astype(o_ref.dtype)


def matmul_ktiled(a, w, out_dtype, tm, tn, tk, name):
    m, k = a.shape
    n = w.shape[1]
    return pl.pallas_call(
        _matmul_ktiled_kernel,
        grid=(m // tm, n // tn, k // tk),
        in_specs=[pl.BlockSpec((tm, tk), lambda i, j, kk: (i, kk)),
                  pl.BlockSpec((tk, tn), lambda i, j, kk: (kk, j))],
        out_specs=pl.BlockSpec((tm, tn), lambda i, j, kk: (i, j)),
        out_shape=jax.ShapeDtypeStruct((m, n), out_dtype),
        scratch_shapes=[pltpu.VMEM((tm, tn), F32)],
        compiler_params=_params("parallel", "parallel", "arbitrary"),
        name=name,
    )(a, w)


def _split3(x):
    hi = x.astype(BF16)
    r1 = x - hi.astype(F32)
    mid = r1.astype(BF16)
    lo = (r1 - mid.astype(F32)).astype(BF16)
    return hi, mid, lo


def _gate_prep_kernel(gi_ref, gf_ref, bi_ref, bf_ref, b_ref, e_ref, r_ref, *, chunk):
    s, w = gi_ref.shape
    nh = w // 2
    log_i = GATE_CAP * jnp.tanh((gi_ref[...] + bi_ref[...]) * (1.0 / GATE_CAP))
    af = GATE_CAP * jnp.tanh((gf_ref[...] + bf_ref[...]) * (1.0 / GATE_CAP))
    log_f = jnp.minimum(af, 0.0) - jnp.log1p(jnp.exp(-jnp.abs(af)))
    ri = lax.broadcasted_iota(jnp.int32, (chunk, chunk), 0)
    ci = lax.broadcasted_iota(jnp.int32, (chunk, chunk), 1)
    tril = (ci <= ri).astype(BF16)
    triu = (ci >= ri).astype(BF16)
    is_fwd = lax.broadcasted_iota(jnp.int32, (chunk, w), 1) < nh
    for c in range(s // chunk):
        rows = slice(c * chunk, (c + 1) * chunk)
        parts = _split3(log_f[rows, :])
        cum_f = sum(jnp.dot(tril, p, preferred_element_type=F32) for p in parts)
        cum_b = sum(jnp.dot(triu, p, preferred_element_type=F32) for p in parts)
        b = jnp.where(is_fwd, cum_f, cum_b)
        tot = jnp.where(is_fwd[:1], cum_f[chunk - 1:chunk, :], cum_b[0:1, :])
        li = log_i[rows, :]
        b_ref[rows, :] = b
        e_ref[rows, :] = tot - b + li
        r_ref[rows, :] = li - b


def gate_prep(gi, gf, bi, bf, chunk):
    bsz, s, w = gi.shape
    blk = pl.BlockSpec((None, s, w), lambda b: (b, 0, 0))
    vec = pl.BlockSpec((1, w), lambda b: (0, 0))
    out = jax.ShapeDtypeStruct((bsz, s, w), F32)
    return pl.pallas_call(
        functools.partial(_gate_prep_kernel, chunk=chunk),
        grid=(bsz,),
        in_specs=[blk, blk, vec, vec],
        out_specs=[blk, blk, blk],
        out_shape=[out, out, out],
        compiler_params=_params("parallel"),
        name="gate_prep",
    )(gi, gf, bi, bf)


def _mlstm_kernel(q_ref, k_ref, v_ref, o_ref, gcol_ref, grow_ref, gh_ref, out_ref,
                  hf_ref, c_ref, n_ref, *, chunk, q_scale):
    s = q_ref.shape[0]
    nc = s // chunk
    ri = lax.broadcasted_iota(jnp.int32, (chunk, chunk), 0)
    ci = lax.broadcasted_iota(jnp.int32, (chunk, chunk), 1)

    for d in (0, 1):
        mask = (ci <= ri) if d == 0 else (ci >= ri)
        c_ref[...] = jnp.zeros_like(c_ref)
        n_ref[...] = jnp.zeros_like(n_ref)

        def body(t, carry, d=d, mask=mask):
            c = t if d == 0 else nc - 1 - t
            rows = pl.ds(pl.multiple_of(c * chunk, chunk), chunk)
            q = q_ref[rows, :] * q_scale
            k = k_ref[rows, :]
            v = v_ref[rows, :]
            gc = gcol_ref[rows, :]
            bcol = gc[:, 2 * d:2 * d + 1]
            ecol = gc[:, 2 * d + 1:2 * d + 2]
            rrow = grow_ref[d, c]
            last = chunk - 1 if d == 0 else 0
            decay = jnp.exp(bcol[last:last + 1, :])

            w_intra = jnp.exp(jnp.where(mask, bcol + rrow, -jnp.inf))
            w_inter = jnp.exp(bcol)
            sc = lax.dot_general(q, k, (((1,), (1,)), ((), ())), preferred_element_type=F32) * w_intra
            c_prev = c_ref[...]
            n_prev = n_ref[...]
            inter = jnp.dot(q, c_prev.astype(BF16), preferred_element_type=F32)
            num = jnp.dot(sc.astype(BF16), v, preferred_element_type=F32) + w_inter * inter
            qn = jnp.sum(q.astype(F32) * n_prev, axis=1, keepdims=True)
            den = jnp.sum(sc, axis=1, keepdims=True) + w_inter * qn
            h = num * (1.0 / jnp.maximum(jnp.abs(den), 1.0))

            kw = k.astype(F32) * jnp.exp(ecol)
            c_ref[...] = decay * c_prev + lax.dot_general(
                kw.astype(BF16), v, (((0,), (0,)), ((), ())), preferred_element_type=F32)
            n_ref[...] = decay * n_prev + jnp.sum(kw, axis=0, keepdims=True)

            if d == 0:
                hf_ref[rows, :] = h
            else:
                tot = hf_ref[rows, :] + h
                y = _rms(tot, gh_ref[...]) * jax.nn.sigmoid(o_ref[rows, :].astype(F32))
                out_ref[rows, :] = y.astype(out_ref.dtype)
            return carry

        lax.fori_loop(0, nc, body, 0)


def mlstm(u3, gcol, grow, g_head, *, dk, dv, q_cb0, k_cb0, v_cb0, o_cb0, chunk):
    bsz, s, _ = u3.shape
    nh = gcol.shape[1]
    nc = s // chunk

    def col(width, cb0):
        return pl.BlockSpec((None, s, width), lambda b, h, cb0=cb0: (b, 0, cb0 + h))

    return pl.pallas_call(
        functools.partial(_mlstm_kernel, chunk=chunk, q_scale=dk ** -0.5),
        grid=(bsz, nh),
        in_specs=[col(dk, q_cb0), col(dk, k_cb0), col(dv, v_cb0), col(dv, o_cb0),
                  pl.BlockSpec((None, None, s, 4), lambda b, h: (b, h, 0, 0)),
                  pl.BlockSpec((None, None, 2, nc, 1, chunk), lambda b, h: (b, h, 0, 0, 0, 0)),
                  pl.BlockSpec((1, dv), lambda b, h: (0, h))],
        out_specs=pl.BlockSpec((None, s, dv), lambda b, h: (b, 0, h)),
        out_shape=jax.ShapeDtypeStruct((bsz, s, nh * dv), BF16),
        scratch_shapes=[pltpu.VMEM((s, dv), F32), pltpu.VMEM((dk, dv), F32), pltpu.VMEM((1, dk), F32)],
        compiler_params=_params("parallel", "parallel"),
        name="mlstm",
    )(u3, u3, u3, u3, gcol, grow, g_head.reshape(1, nh * dv))


def _cos_sin(rows, n_cols, period):
    k = jnp.arange(n_cols, dtype=jnp.int32)
    ang = ((rows[:, None] * k[None, :]) % period).astype(F32) * (2.0 * math.pi / period)
    return jnp.cos(ang), jnp.sin(ang)


def _dft_mats(n, scale):
    n1 = 1 << ((n.bit_length() - 1) // 2)
    n2 = n // n1
    ca, sa = _cos_sin(jnp.arange(n1, dtype=jnp.int32), n, n1)
    cb, sb = _cos_sin(jnp.arange(n2, dtype=jnp.int32), n, n)
    ca, sa, cb, sb = ca[:, None, :], sa[:, None, :], cb[None, :, :], sb[None, :, :]
    c = (ca * cb - sa * sb) * scale
    s = (sa * cb + ca * sb) * scale
    return c.reshape(n, n).astype(BF16), s.reshape(n, n).astype(BF16)


FLIP_ROWS = 256
SUBLANES = 8


def _flip_matrix(r):
    t = jnp.arange(r, dtype=jnp.int32)
    return ((t[:, None] + t[None, :]) == r).astype(BF16)


def _shifted_flip(j_ref, blk_ref, nxt_ref, use_next):
    rev = jnp.dot(j_ref[...], blk_ref[...], preferred_element_type=F32)
    first = lax.broadcasted_iota(jnp.int32, rev.shape, 0) == 0
    row0 = jnp.where(use_next, nxt_ref[0:1, :].astype(F32), 0.0)
    return jnp.where(first, row0, rev)


def _dft_chan_kernel(j_ref, zlo_ref, za_ref, zb_ref, cc_ref, sc_ref, p_ref, q_ref, *, n_fold):
    i = pl.program_id(2)
    a = zlo_ref[...]
    r = _shifted_flip(j_ref, za_ref, zb_ref, i > 0)
    r = jnp.where(i < n_fold, r, 0.0).astype(BF16)
    p_ref[...] = jnp.dot(a + r, cc_ref[...], preferred_element_type=F32).astype(p_ref.dtype)
    q_ref[...] = jnp.dot(a - r, sc_ref[...], preferred_element_type=F32).astype(q_ref.dtype)


def dft_chan(u3, z_cb0, cc, sc, hp, r):
    bsz, n, _ = u3.shape
    gd = cc.shape[0]
    nb = n // r
    out = jax.ShapeDtypeStruct((bsz, hp, F_GROUPS * gd), BF16)
    dst = pl.BlockSpec((None, r, gd), lambda b, g, i: (b, i, g))
    mat = pl.BlockSpec((gd, gd), lambda b, g, i: (0, 0))
    return pl.pallas_call(
        functools.partial(_dft_chan_kernel, n_fold=nb // 2),
        grid=(bsz, F_GROUPS, hp // r),
        in_specs=[pl.BlockSpec((r, r), lambda b, g, i: (0, 0)),
                  pl.BlockSpec((None, r, gd), lambda b, g, i: (b, i, z_cb0 + g)),
                  pl.BlockSpec((None, r, gd), lambda b, g, i: (b, nb - 1 - i, z_cb0 + g)),
                  pl.BlockSpec((None, SUBLANES, gd),
                               lambda b, g, i: (b, jnp.minimum(nb - i, nb - 1) * (r // SUBLANES), z_cb0 + g)),
                  mat, mat],
        out_specs=[dst, dst],
        out_shape=[out, out],
        compiler_params=_params("parallel", "parallel", "arbitrary"),
        name="dft_chan",
    )(_flip_matrix(r), u3, u3, u3, cc, sc)


def _dft_unfold_kernel(j_ref, lo_ref, hi_ref, nxt_ref, o_ref, *, n_fold):
    t = pl.program_id(1)

    @pl.when(t < n_fold)
    def _():
        o_ref[...] = lo_ref[...]

    @pl.when(t >= n_fold)
    def _():
        o_ref[...] = _shifted_flip(j_ref, hi_ref, nxt_ref, True).astype(o_ref.dtype)


def dft_unfold(lo, hi, n, r):
    bsz, _, d = lo.shape
    nb = n // r
    n_fold = nb // 2
    return pl.pallas_call(
        functools.partial(_dft_unfold_kernel, n_fold=n_fold),
        grid=(bsz, nb),
        in_specs=[pl.BlockSpec((r, r), lambda b, t: (0, 0)),
                  pl.BlockSpec((None, r, d), lambda b, t: (b, jnp.minimum(t, n_fold - 1), 0)),
                  pl.BlockSpec((None, r, d), lambda b, t: (b, jnp.minimum(nb - 1 - t, n_fold - 1), 0)),
                  pl.BlockSpec((None, SUBLANES, d),
                               lambda b, t: (b, jnp.minimum(nb - t, n_fold) * (r // SUBLANES), 0))],
        out_specs=pl.BlockSpec((None, r, d), lambda b, t: (b, t, 0)),
        out_shape=jax.ShapeDtypeStruct((bsz, n, d), BF16),
        compiler_params=_params("parallel", "arbitrary"),
        name="dft_unfold",
    )(_flip_matrix(r), lo, hi, hi)


def _dft_seq_kernel(cs_ref, ss_ref, p_ref, q_ref, lo_ref, hi_ref):
    a = jnp.dot(cs_ref[...], p_ref[...], preferred_element_type=F32)
    b = jnp.dot(ss_ref[...], q_ref[...], preferred_element_type=F32)
    lo_ref[...] = (a - b).astype(lo_ref.dtype)
    hi_ref[...] = (a + b).astype(hi_ref.dtype)


def dft_seq(cs, ss, p, q, gd, tm):
    bsz, hp, w = p.shape
    out = jax.ShapeDtypeStruct((bsz, hp, w), BF16)
    mat = pl.BlockSpec((tm, hp), lambda b, g, i: (i, 0))
    src = pl.BlockSpec((None, hp, gd), lambda b, g, i: (b, 0, g))
    dst = pl.BlockSpec((None, tm, gd), lambda b, g, i: (b, i, g))
    return pl.pallas_call(
        _dft_seq_kernel,
        grid=(bsz, w // gd, hp // tm),
        in_specs=[mat, mat, src, src],
        out_specs=[dst, dst],
        out_shape=[out, out],
        compiler_params=_params("parallel", "parallel", "arbitrary"),
        name="dft_seq",
    )(cs, ss, p, q)


def fourier_mix(u3, z_cb0, gd):
    bsz, n, _ = u3.shape
    h = n // 2
    r = min(FLIP_ROWS, h)
    hp = h + r
    cc, sc = _dft_mats(gd, gd ** -0.5)
    cs, ss = _dft_mats(n, n ** -0.5)
    pad = ((0, hp - h - 1), (0, hp - h - 1))
    cs = jnp.pad(cs[:h + 1, :h + 1], pad)
    ss = jnp.pad(ss[:h + 1, :h + 1], pad)
    p, q = dft_chan(u3, z_cb0, cc, sc, hp, r)
    tm = hp // 3 if hp % (3 * SUBLANES) == 0 else hp
    lo, hi = dft_seq(cs, ss, p, q, gd, tm)
    return dft_unfold(lo, hi, n, r)


def _tile(n, pref):
    return pref if n % pref == 0 else n


def _layer(x, xn, p_i, lw, g_next, *, bsz, seq):
    m, d = x.shape
    nh = M_HEADS
    dv = d // nh
    dk = dv // 2
    qk_w, v_w = nh * dk, nh * dv
    gd = d // F_GROUPS
    n_main = 2 * qk_w + 2 * v_w
    g0 = n_main
    z0 = n_main + 4 * nh
    chunk = min(MLSTM_CHUNK, seq)
    tm = _tile(m, 1024)

    li = lw["layer"]
    tn = _tile(d, 512)
    w_in_t = lw["w_in_t"]
    u = dot_wt(xn, w_in_t, li, lambda j: j * tn + jnp.where(j * tn >= g0, z0 - g0, 0),
               n_main + d, BF16, tm, tn, "in_proj")
    graw = dot_wt(xn, w_in_t, li, lambda j: g0, 128, F32, tm, 128, "gate_proj")

    graw = graw[:, :4 * nh].reshape(bsz, seq, 2, 2, nh)
    gi = graw[:, :, :, 0].reshape(bsz, seq, 2 * nh)
    gf = graw[:, :, :, 1].reshape(bsz, seq, 2 * nh)
    b_arr, e_arr, r_arr = gate_prep(gi, gf, lw["b_igate"][li].reshape(1, 2 * nh),
                                    lw["b_fgate"][li].reshape(1, 2 * nh), chunk)
    b4 = b_arr.reshape(bsz, seq, 2, nh)
    e4 = e_arr.reshape(bsz, seq, 2, nh)
    gcol = jnp.stack([b4[:, :, 0], e4[:, :, 0], b4[:, :, 1], e4[:, :, 1]], axis=-1)
    gcol = gcol.transpose(0, 2, 1, 3)
    grow = r_arr.reshape(bsz, seq // chunk, chunk, 2, nh).transpose(0, 4, 3, 1, 2)
    grow = grow[:, :, :, :, None, :]
    u3 = u.reshape(bsz, seq, n_main + d)
    hg = mlstm(u3, gcol, grow, lw["g_head"][li], dk=dk, dv=dv, q_cb0=0, k_cb0=qk_w // dk,
               v_cb0=2 * qk_w // dv, o_cb0=(2 * qk_w + v_w) // dv, chunk=chunk)
    hg = hg.reshape(m, v_w)

    zf = fourier_mix(u3, n_main // gd, gd).reshape(m, d)

    gates = fused_dots([(xn, 0, 0, lw["w_merge"], li, 0, 1)],
                       [(lw["b_merge"][li].reshape(1, 2 * d), "row", 0)],
                       _ep_bias_sigmoid, (m, 2 * d), BF16, tm, tn, "merge_gate")
    ym = fused_dots([(hg, 0, 0, lw["w_branch_m"], li, 0, 1)], [(gates, "tile", 0)],
                    _ep_gate_mul, (m, d), BF16, tm, tn, "branch_m")
    mix = fused_dots([(zf, 0, 0, lw["w_branch_f"], li, 0, 1)], [(gates, "tile", d // tn), (ym, "tile", 0)],
                     _ep_gate_mul_add, (m, d), BF16, tm, tn, "branch_f")
    y = fused_dots([(mix, 0, 0, lw["w_out"], li, 0, 1)], [], _ep_identity, (m, d), BF16, tm, tn, "out_proj")
    x, hn = resnorm(x, y, lw["g_mix_post"][li], lw["g_mlp_pre"][li])

    d_ff = lw["w_up"].shape[-1]
    up = fused_dots([(hn, 0, 0, lw["w_up"], li, 0, 1)], [], _ep_relu2, (m, d_ff), BF16, tm, tn, "mlp_up")
    w_down = cast_rows(lw["w_down"], li, BF16, _tile(d_ff, 512))
    y = matmul_ktiled(up, w_down, BF16, tm, _tile(d, 1024), _tile(d_ff, 4096), "mlp_down")
    x, xb = resnorm(x, y, lw["g_mlp_post"][li], None)

    ple = fused_dots([(xb, 0, 0, lw["w_ple_gate"], li, 0, 1), (p_i, 0, 0, lw["w_ple_in"], li, 0, 1)],
                     [], _ep_sigmoid_mul, (m, d), BF16, tm, tn, "ple")
    if g_next is None:
        x, _ = resnorm(x, ple, lw["g_ple_post"][li], None)
        return x, None
    return resnorm(x, ple, lw["g_ple_post"][li], g_next)


def kernel(x, p, g_mix_pre, g_mix_post, g_mlp_pre, g_mlp_post, g_ple_post, w_in, b_igate, b_fgate,
           g_head, w_branch_m, w_branch_f, w_merge, b_merge, w_out, w_up, w_down, w_ple_in, w_ple_gate):
    bsz, seq, d = x.shape
    depth = w_in.shape[0]
    m = bsz * seq
    xf = x.reshape(m, d)
    xn = rmsnorm(xf, g_mix_pre[0])
    lw = dict(w_in_t=jnp.swapaxes(w_in, 1, 2), b_igate=b_igate, b_fgate=b_fgate, g_head=g_head, w_branch_m=w_branch_m,
              w_branch_f=w_branch_f, w_merge=w_merge, b_merge=b_merge, w_out=w_out, w_up=w_up,
              w_down=w_down, w_ple_in=w_ple_in, w_ple_gate=w_ple_gate, g_mix_post=g_mix_post,
              g_mlp_pre=g_mlp_pre, g_mlp_post=g_mlp_post, g_ple_post=g_ple_post)
    for i in range(depth):
        g_next = g_mix_pre[i + 1] if i + 1 < depth else None
        xf, xn = _layer(xf, xn, p[i].reshape(m, -1), dict(lw, layer=i), g_next, bsz=bsz, seq=seq)
    return xf.reshape(bsz, seq, d)
```

```python
import functools
import math

import jax
import jax.numpy as jnp
from jax import lax
from jax.experimental import pallas as pl
from jax.experimental.pallas import tpu as pltpu

F32 = jnp.float32
BF16 = jnp.bfloat16

M_HEADS = 8
F_GROUPS = 4
GATE_CAP = 15.0
EPS = 1e-6
MLSTM_CHUNK = 256
V7X_VMEM_BYTES = 64 * 1024 * 1024
VMEM_LIMIT = V7X_VMEM_BYTES - 4 * 1024 * 1024


def _params(*semantics):
    return pltpu.CompilerParams(dimension_semantics=semantics, vmem_limit_bytes=VMEM_LIMIT)


def _rms(x, g):
    return x * lax.rsqrt(jnp.mean(x * x, axis=-1, keepdims=True) + EPS) * g


def _rmsnorm_kernel(x_ref, g_ref, o_ref):
    o_ref[...] = _rms(x_ref[...], g_ref[...]).astype(o_ref.dtype)


def rmsnorm(x, g, tm=256):
    m, d = x.shape
    return pl.pallas_call(
        _rmsnorm_kernel,
        grid=(m // tm,),
        in_specs=[pl.BlockSpec((tm, d), lambda i: (i, 0)),
                  pl.BlockSpec((1, d), lambda i: (0, 0))],
        out_specs=pl.BlockSpec((tm, d), lambda i: (i, 0)),
        out_shape=jax.ShapeDtypeStruct((m, d), BF16),
        compiler_params=_params("parallel"),
        name="rmsnorm",
    )(x, g.reshape(1, d))


def _resnorm_kernel(x_ref, y_ref, gp_ref, gn_ref, xo_ref, xn_ref, *, norm_next):
    xnew = x_ref[...] + _rms(y_ref[...].astype(F32), gp_ref[...])
    xo_ref[...] = xnew
    if norm_next:
        xn_ref[...] = _rms(xnew, gn_ref[...]).astype(xn_ref.dtype)
    else:
        xn_ref[...] = xnew.astype(xn_ref.dtype)


def resnorm(x, y, g_post, g_next, tm=256):
    m, d = x.shape
    norm_next = g_next is not None
    gn = g_next if norm_next else g_post
    row = pl.BlockSpec((tm, d), lambda i: (i, 0))
    vec = pl.BlockSpec((1, d), lambda i: (0, 0))
    return pl.pallas_call(
        functools.partial(_resnorm_kernel, norm_next=norm_next),
        grid=(m // tm,),
        in_specs=[row, row, vec, vec],
        out_specs=[row, row],
        out_shape=[jax.ShapeDtypeStruct((m, d), F32), jax.ShapeDtypeStruct((m, d), BF16)],
        compiler_params=_params("parallel"),
        name="resnorm",
    )(x, y, g_post.reshape(1, d), gn.reshape(1, d))


def _panel_dots_kernel(*refs, kinds, n_extra, epilogue, ck):
    n_dots = len(kinds)
    out_ref = refs[2 * n_dots + n_extra]
    scratch = list(refs[2 * n_dots + n_extra + 1:])
    bufs = [(scratch.pop(0), scratch.pop(0)) if kinds[p] != "small" else None for p in range(n_dots)]
    g = pl.program_id(0)
    i = pl.program_id(1)
    rows = pl.ds(pl.multiple_of(i * ck, ck), ck)

    def stage(parity):
        for p in range(n_dots):
            w_ref = refs[2 * p + 1]
            if kinds[p] == "rows":
                bufs[p][parity][rows, :] = w_ref[...].astype(BF16)
            elif kinds[p] == "cols":
                bufs[p][parity][rows, :] = w_ref[0].T.astype(BF16)

    def compute(parity):
        accs = []
        for p in range(n_dots):
            w = refs[2 * p + 1][...].astype(BF16) if kinds[p] == "small" else bufs[p][parity][...]
            accs.append(jnp.dot(refs[2 * p][...].astype(BF16), w, preferred_element_type=F32))
        extras = [r[...] for r in refs[2 * n_dots:2 * n_dots + n_extra]]
        out_ref[...] = epilogue(accs, extras).astype(out_ref.dtype)

    @pl.when(g == 0)
    def _():
        stage(0)

    for parity in (0, 1):
        @pl.when((g > 0) & (g % 2 == parity))
        def _(parity=parity):
            stage(parity)
            compute(1 - parity)


def panel_dots(pairs, extras, epilogue, out_shape, out_dtype, tm, tn, name):
    m, n = out_shape
    n_i, n_j = m // tm, n // tn
    ck = None
    in_specs, args, scratch, kinds = [], [], [], []

    def jj(g):
        return jnp.maximum(g - 1, 0)

    def ii(g, i):
        return jnp.where(g == 0, 0, i)

    def pj(g):
        return jnp.minimum(g, n_j - 1)

    for a, w, kind, layer, where in pairs:
        k = a.shape[1]
        lead = () if layer is None else (layer,)
        lead_blk = () if layer is None else (None,)
        in_specs.append(pl.BlockSpec((tm, k), lambda g, i: (ii(g, i), 0)))
        if kind == "small":
            in_specs.append(pl.BlockSpec(lead_blk + (k, tn),
                                         lambda g, i, lead=lead, c=where: lead + (0, c + jj(g))))
        else:
            assert ck in (None, k // n_i) and k % n_i == 0
            ck = k // n_i
            scratch += [pltpu.VMEM((k, tn), BF16), pltpu.VMEM((k, tn), BF16)]
            if kind == "rows":
                in_specs.append(pl.BlockSpec(lead_blk + (ck, tn),
                                             lambda g, i, lead=lead, c=where: lead + (i, c + pj(g))))
            else:
                in_specs.append(pl.BlockSpec(
                    (pl.Element(1), pl.Element(tn), pl.Element(ck)),
                    lambda g, i, l=layer, f=where, ck=ck: (l, pl.multiple_of(f(pj(g)), 8), i * ck)))
        args += [a, w]
        kinds.append(kind)
    for arr, kind, cb0 in extras:
        if kind == "row":
            in_specs.append(pl.BlockSpec((1, tn), lambda g, i, cb0=cb0: (0, cb0 + jj(g))))
        else:
            in_specs.append(pl.BlockSpec((tm, tn), lambda g, i, cb0=cb0: (ii(g, i), cb0 + jj(g))))
        args.append(arr)
    return pl.pallas_call(
        functools.partial(_panel_dots_kernel, kinds=tuple(kinds), n_extra=len(extras),
                          epilogue=epilogue, ck=ck),
        grid=(n_j + 1, n_i),
        in_specs=in_specs,
        out_specs=pl.BlockSpec((tm, tn), lambda g, i: (ii(g, i), jj(g))),
        out_shape=jax.ShapeDtypeStruct((m, n), out_dtype),
        scratch_shapes=scratch,
        compiler_params=_params("arbitrary", "arbitrary"),
        name=name,
    )(*args)


def _cast_kernel(x_ref, o_ref):
    o_ref[...] = x_ref[...].astype(o_ref.dtype)


def cast_rows(w, layer, dtype, tr):
    _, k, n = w.shape
    return pl.pallas_call(
        _cast_kernel,
        grid=(k // tr,),
        in_specs=[pl.BlockSpec((None, tr, n), lambda r, l=layer: (l, r, 0))],
        out_specs=pl.BlockSpec((tr, n), lambda r: (r, 0)),
        out_shape=jax.ShapeDtypeStruct((k, n), dtype),
        compiler_params=_params("parallel"),
        name="cast_rows",
    )(w)


def _ep_identity(accs, extras):
    return accs[0]


def _ep_bias_sigmoid(accs, extras):
    return jax.nn.sigmoid(accs[0] + extras[0])


def _ep_relu2(accs, extras):
    return jnp.square(jnp.maximum(accs[0], 0.0))


def _ep_gate_mul(accs, extras):
    return extras[0].astype(F32) * accs[0]


def _ep_gate_mul_add(accs, extras):
    return extras[0].astype(F32) * accs[0] + extras[1].astype(F32)


def _ep_sigmoid_mul(accs, extras):
    return jax.nn.sigmoid(accs[0]) * accs[1]


def _ep_sub(accs, extras):
    return accs[0] - accs[1]


def _matmul_ktiled_kernel(a_ref, w_ref, o_ref, acc_ref):
    kk = pl.program_id(2)

    @pl.when(kk == 0)
    def _():
        acc_ref[...] = jnp.zeros_like(acc_ref)

    acc_ref[...] += jnp.dot(a_ref[...], w_ref[...], preferred_element_type=F32)

    @pl.when(kk == pl.num_programs(2) - 1)
    def _():
        o_ref[...] = acc_ref[...].astype(o_ref.dtype)


def matmul_ktiled(a, w, out_dtype, tm, tn, tk, name):
    m, k = a.shape
    n = w.shape[1]
    return pl.pallas_call(
        _matmul_ktiled_kernel,
        grid=(m // tm, n // tn, k // tk),
        in_specs=[pl.BlockSpec((tm, tk), lambda i, j, kk: (i, kk)),
                  pl.BlockSpec((tk, tn), lambda i, j, kk: (kk, j))],
        out_specs=pl.BlockSpec((tm, tn), lambda i, j, kk: (i, j)),
        out_shape=jax.ShapeDtypeStruct((m, n), out_dtype),
        scratch_shapes=[pltpu.VMEM((tm, tn), F32)],
        compiler_params=_params("parallel", "parallel", "arbitrary"),
        name=name,
    )(a, w)


def _split3(x):
    hi = x.astype(BF16)
    r1 = x - hi.astype(F32)
    mid = r1.astype(BF16)
    lo = (r1 - mid.astype(F32)).astype(BF16)
    return hi, mid, lo


def _gate_prep_kernel(gi_ref, gf_ref, bi_ref, bf_ref, b_ref, e_ref, r_ref, *, chunk):
    s, w = gi_ref.shape
    nh = w // 2
    log_i = GATE_CAP * jnp.tanh((gi_ref[...] + bi_ref[...]) * (1.0 / GATE_CAP))
    af = GATE_CAP * jnp.tanh((gf_ref[...] + bf_ref[...]) * (1.0 / GATE_CAP))
    log_f = jnp.minimum(af, 0.0) - jnp.log1p(jnp.exp(-jnp.abs(af)))
    ri = lax.broadcasted_iota(jnp.int32, (chunk, chunk), 0)
    ci = lax.broadcasted_iota(jnp.int32, (chunk, chunk), 1)
    tril = (ci <= ri).astype(BF16)
    triu = (ci >= ri).astype(BF16)
    is_fwd = lax.broadcasted_iota(jnp.int32, (chunk, w), 1) < nh
    for c in range(s // chunk):
        rows = slice(c * chunk, (c + 1) * chunk)
        parts = _split3(log_f[rows, :])
        cum_f = sum(jnp.dot(tril, p, preferred_element_type=F32) for p in parts)
        cum_b = sum(jnp.dot(triu, p, preferred_element_type=F32) for p in parts)
        b = jnp.where(is_fwd, cum_f, cum_b)
        tot = jnp.where(is_fwd[:1], cum_f[chunk - 1:chunk, :], cum_b[0:1, :])
        li = log_i[rows, :]
        b_ref[rows, :] = b
        e_ref[rows, :] = tot - b + li
        r_ref[rows, :] = li - b


def gate_prep(gi, gf, bi, bf, chunk):
    bsz, s, w = gi.shape
    blk = pl.BlockSpec((None, s, w), lambda b: (b, 0, 0))
    vec = pl.BlockSpec((1, w), lambda b: (0, 0))
    out = jax.ShapeDtypeStruct((bsz, s, w), F32)
    return pl.pallas_call(
        functools.partial(_gate_prep_kernel, chunk=chunk),
        grid=(bsz,),
        in_specs=[blk, blk, vec, vec],
        out_specs=[blk, blk, blk],
        out_shape=[out, out, out],
        compiler_params=_params("parallel"),
        name="gate_prep",
    )(gi, gf, bi, bf)


def _mlstm_kernel(q_ref, k_ref, v_ref, o_ref, gcol_ref, grow_ref, gh_ref, out_ref,
                  hf_ref, c_ref, n_ref, *, chunk, q_scale):
    s = q_ref.shape[0]
    nc = s // chunk
    ri = lax.broadcasted_iota(jnp.int32, (chunk, chunk), 0)
    ci = lax.broadcasted_iota(jnp.int32, (chunk, chunk), 1)

    for d in (0, 1):
        mask = (ci <= ri) if d == 0 else (ci >= ri)
        c_ref[...] = jnp.zeros_like(c_ref)
        n_ref[...] = jnp.zeros_like(n_ref)

        def body(t, carry, d=d, mask=mask):
            c = t if d == 0 else nc - 1 - t
            rows = pl.ds(pl.multiple_of(c * chunk, chunk), chunk)
            q = q_ref[rows, :] * q_scale
            k = k_ref[rows, :]
            v = v_ref[rows, :]
            gc = gcol_ref[rows, :]
            bcol = gc[:, 2 * d:2 * d + 1]
            ecol = gc[:, 2 * d + 1:2 * d + 2]
            rrow = grow_ref[d, c]
            last = chunk - 1 if d == 0 else 0
            decay = jnp.exp(bcol[last:last + 1, :])

            w_intra = jnp.exp(jnp.where(mask, bcol + rrow, -jnp.inf))
            w_inter = jnp.exp(bcol)
            sc = lax.dot_general(q, k, (((1,), (1,)), ((), ())), preferred_element_type=F32) * w_intra
            c_prev = c_ref[...]
            n_prev = n_ref[...]
            inter = jnp.dot(q, c_prev.astype(BF16), preferred_element_type=F32)
            num = jnp.dot(sc.astype(BF16), v, preferred_element_type=F32) + w_inter * inter
            qn = jnp.sum(q.astype(F32) * n_prev, axis=1, keepdims=True)
            den = jnp.sum(sc, axis=1, keepdims=True) + w_inter * qn
            h = num * (1.0 / jnp.maximum(jnp.abs(den), 1.0))

            kw = k.astype(F32) * jnp.exp(ecol)
            c_ref[...] = decay * c_prev + lax.dot_general(
                kw.astype(BF16), v, (((0,), (0,)), ((), ())), preferred_element_type=F32)
            n_ref[...] = decay * n_prev + jnp.sum(kw, axis=0, keepdims=True)

            if d == 0:
                hf_ref[rows, :] = h
            else:
                tot = hf_ref[rows, :] + h
                y = _rms(tot, gh_ref[...]) * jax.nn.sigmoid(o_ref[rows, :].astype(F32))
                out_ref[rows, :] = y.astype(out_ref.dtype)
            return carry

        lax.fori_loop(0, nc, body, 0)


def mlstm(u3, gcol, grow, g_head, *, dk, dv, q_cb0, k_cb0, v_cb0, o_cb0, chunk):
    bsz, s, _ = u3.shape
    nh = gcol.shape[1]
    nc = s // chunk

    def col(width, cb0):
        return pl.BlockSpec((None, s, width), lambda b, h, cb0=cb0: (b, 0, cb0 + h))

    return pl.pallas_call(
        functools.partial(_mlstm_kernel, chunk=chunk, q_scale=dk ** -0.5),
        grid=(bsz, nh),
        in_specs=[col(dk, q_cb0), col(dk, k_cb0), col(dv, v_cb0), col(dv, o_cb0),
                  pl.BlockSpec((None, None, s, 4), lambda b, h: (b, h, 0, 0)),
                  pl.BlockSpec((None, None, 2, nc, 1, chunk), lambda b, h: (b, h, 0, 0, 0, 0)),
                  pl.BlockSpec((1, dv), lambda b, h: (0, h))],
        out_specs=pl.BlockSpec((None, s, dv), lambda b, h: (b, 0, h)),
        out_shape=jax.ShapeDtypeStruct((bsz, s, nh * dv), BF16),
        scratch_shapes=[pltpu.VMEM((s, dv), F32), pltpu.VMEM((dk, dv), F32), pltpu.VMEM((1, dk), F32)],
        compiler_params=_params("parallel", "parallel"),
        name="mlstm",
    )(u3, u3, u3, u3, gcol, grow, g_head.reshape(1, nh * dv))


def _cos_sin(rows, n_cols, period):
    k = jnp.arange(n_cols, dtype=jnp.int32)
    ang = ((rows[:, None] * k[None, :]) % period).astype(F32) * (2.0 * math.pi / period)
    return jnp.cos(ang), jnp.sin(ang)


def _dft_mats(n, scale):
    n1 = 1 << ((n.bit_length() - 1) // 2)
    n2 = n // n1
    ca, sa = _cos_sin(jnp.arange(n1, dtype=jnp.int32), n, n1)
    cb, sb = _cos_sin(jnp.arange(n2, dtype=jnp.int32), n, n)
    ca, sa, cb, sb = ca[:, None, :], sa[:, None, :], cb[None, :, :], sb[None, :, :]
    c = (ca * cb - sa * sb) * scale
    s = (sa * cb + ca * sb) * scale
    return c.reshape(n, n).astype(BF16), s.reshape(n, n).astype(BF16)


FLIP_ROWS = 256
SUBLANES = 8


def _flip_matrix(r):
    t = jnp.arange(r, dtype=jnp.int32)
    return ((t[:, None] + t[None, :]) == r).astype(BF16)


def _shifted_flip(j_ref, blk_ref, nxt_ref, use_next):
    rev = jnp.dot(j_ref[...], blk_ref[...], preferred_element_type=F32)
    first = lax.broadcasted_iota(jnp.int32, rev.shape, 0) == 0
    row0 = jnp.where(use_next, nxt_ref[0:1, :].astype(F32), 0.0)
    return jnp.where(first, row0, rev)


def _dft_chan_kernel(j_ref, zlo_ref, za_ref, zb_ref, cc_ref, sc_ref, p_ref, q_ref, *, n_fold):
    i = pl.program_id(2)
    a = zlo_ref[...]
    r = _shifted_flip(j_ref, za_ref, zb_ref, i > 0)
    r = jnp.where(i < n_fold, r, 0.0).astype(BF16)
    p_ref[...] = jnp.dot(a + r, cc_ref[...], preferred_element_type=F32).astype(p_ref.dtype)
    q_ref[...] = jnp.dot(a - r, sc_ref[...], preferred_element_type=F32).astype(q_ref.dtype)


def dft_chan(u3, z_cb0, cc, sc, hp, r):
    bsz, n, _ = u3.shape
    gd = cc.shape[0]
    nb = n // r
    out = jax.ShapeDtypeStruct((bsz, hp, F_GROUPS * gd), BF16)
    dst = pl.BlockSpec((None, r, gd), lambda b, g, i: (b, i, g))
    mat = pl.BlockSpec((gd, gd), lambda b, g, i: (0, 0))
    return pl.pallas_call(
        functools.partial(_dft_chan_kernel, n_fold=nb // 2),
        grid=(bsz, F_GROUPS, hp // r),
        in_specs=[pl.BlockSpec((r, r), lambda b, g, i: (0, 0)),
                  pl.BlockSpec((None, r, gd), lambda b, g, i: (b, i, z_cb0 + g)),
                  pl.BlockSpec((None, r, gd), lambda b, g, i: (b, nb - 1 - i, z_cb0 + g)),
                  pl.BlockSpec((None, SUBLANES, gd),
                               lambda b, g, i: (b, jnp.minimum(nb - i, nb - 1) * (r // SUBLANES), z_cb0 + g)),
                  mat, mat],
        out_specs=[dst, dst],
        out_shape=[out, out],
        compiler_params=_params("parallel", "parallel", "arbitrary"),
        name="dft_chan",
    )(_flip_matrix(r), u3, u3, u3, cc, sc)


def _dft_unfold_kernel(j_ref, lo_ref, hi_ref, nxt_ref, o_ref, *, n_fold):
    t = pl.program_id(1)

    @pl.when(t < n_fold)
    def _():
        o_ref[...] = lo_ref[...]

    @pl.when(t >= n_fold)
    def _():
        o_ref[...] = _shifted_flip(j_ref, hi_ref, nxt_ref, True).astype(o_ref.dtype)


def dft_unfold(lo, hi, n, r):
    bsz, _, d = lo.shape
    nb = n // r
    n_fold = nb // 2
    return pl.pallas_call(
        functools.partial(_dft_unfold_kernel, n_fold=n_fold),
        grid=(bsz, nb),
        in_specs=[pl.BlockSpec((r, r), lambda b, t: (0, 0)),
                  pl.BlockSpec((None, r, d), lambda b, t: (b, jnp.minimum(t, n_fold - 1), 0)),
                  pl.BlockSpec((None, r, d), lambda b, t: (b, jnp.minimum(nb - 1 - t, n_fold - 1), 0)),
                  pl.BlockSpec((None, SUBLANES, d),
                               lambda b, t: (b, jnp.minimum(nb - t, n_fold) * (r // SUBLANES), 0))],
        out_specs=pl.BlockSpec((None, r, d), lambda b, t: (b, t, 0)),
        out_shape=jax.ShapeDtypeStruct((bsz, n, d), BF16),
        compiler_params=_params("parallel", "arbitrary"),
        name="dft_unfold",
    )(_flip_matrix(r), lo, hi, hi)


def _dft_seq_kernel(cs_ref, ss_ref, p_ref, q_ref, lo_ref, hi_ref):
    a = jnp.dot(cs_ref[...], p_ref[...], preferred_element_type=F32)
    b = jnp.dot(ss_ref[...], q_ref[...], preferred_element_type=F32)
    lo_ref[...] = (a - b).astype(lo_ref.dtype)
    hi_ref[...] = (a + b).astype(hi_ref.dtype)


def dft_seq(cs, ss, p, q, gd, tm):
    bsz, hp, w = p.shape
    out = jax.ShapeDtypeStruct((bsz, hp, w), BF16)
    mat = pl.BlockSpec((tm, hp), lambda b, g, i: (i, 0))
    src = pl.BlockSpec((None, hp, gd), lambda b, g, i: (b, 0, g))
    dst = pl.BlockSpec((None, tm, gd), lambda b, g, i: (b, i, g))
    return pl.pallas_call(
        _dft_seq_kernel,
        grid=(bsz, w // gd, hp // tm),
        in_specs=[mat, mat, src, src],
        out_specs=[dst, dst],
        out_shape=[out, out],
        compiler_params=_params("parallel", "parallel", "arbitrary"),
        name="dft_seq",
    )(cs, ss, p, q)


def fourier_mix(u3, z_cb0, gd):
    bsz, n, _ = u3.shape
    h = n // 2
    r = min(FLIP_ROWS, h)
    hp = h + r
    cc, sc = _dft_mats(gd, gd ** -0.5)
    cs, ss = _dft_mats(n, n ** -0.5)
    pad = ((0, hp - h - 1), (0, hp - h - 1))
    cs = jnp.pad(cs[:h + 1, :h + 1], pad)
    ss = jnp.pad(ss[:h + 1, :h + 1], pad)
    p, q = dft_chan(u3, z_cb0, cc, sc, hp, r)
    tm = hp // 3 if hp % (3 * SUBLANES) == 0 else hp
    lo, hi = dft_seq(cs, ss, p, q, gd, tm)
    return dft_unfold(lo, hi, n, r)


def _tile(n, pref):
    return pref if n % pref == 0 else n


def _layer(x, xn, p_i, lw, g_next, *, bsz, seq):
    m, d = x.shape
    nh = M_HEADS
    dv = d // nh
    dk = dv // 2
    qk_w, v_w = nh * dk, nh * dv
    gd = d // F_GROUPS
    n_main = 2 * qk_w + 2 * v_w
    g0 = n_main
    z0 = n_main + 4 * nh
    chunk = min(MLSTM_CHUNK, seq)
    tm = _tile(m, 1024)

    li = lw["layer"]
    tn = _tile(d, 1024)
    w_in_t = lw["w_in_t"]
    u = panel_dots([(xn, w_in_t, "cols", li, lambda j: j * tn + jnp.where(j * tn >= g0, z0 - g0, 0))],
                   [], _ep_identity, (m, n_main + d), BF16, tm, tn, "in_proj")
    graw = panel_dots([(xn, w_in_t, "cols", li, lambda j: g0)], [], _ep_identity, (m, 128), F32,
                      tm, 128, "gate_proj")

    graw = graw[:, :4 * nh].reshape(bsz, seq, 2, 2, nh)
    gi = graw[:, :, :, 0].reshape(bsz, seq, 2 * nh)
    gf = graw[:, :, :, 1].reshape(bsz, seq, 2 * nh)
    b_arr, e_arr, r_arr = gate_prep(gi, gf, lw["b_igate"][li].reshape(1, 2 * nh),
                                    lw["b_fgate"][li].reshape(1, 2 * nh), chunk)
    b4 = b_arr.reshape(bsz, seq, 2, nh)
    e4 = e_arr.reshape(bsz, seq, 2, nh)
    gcol = jnp.stack([b4[:, :, 0], e4[:, :, 0], b4[:, :, 1], e4[:, :, 1]], axis=-1)
    gcol = gcol.transpose(0, 2, 1, 3)
    grow = r_arr.reshape(bsz, seq // chunk, chunk, 2, nh).transpose(0, 4, 3, 1, 2)
    grow = grow[:, :, :, :, None, :]
    u3 = u.reshape(bsz, seq, n_main + d)
    hg = mlstm(u3, gcol, grow, lw["g_head"][li], dk=dk, dv=dv, q_cb0=0, k_cb0=qk_w // dk,
               v_cb0=2 * qk_w // dv, o_cb0=(2 * qk_w + v_w) // dv, chunk=chunk)
    hg = hg.reshape(m, v_w)

    zf = fourier_mix(u3, n_main // gd, gd).reshape(m, d)

    gates = panel_dots([(xn, lw["w_merge"], "rows", li, 0)],
                       [(lw["b_merge"][li].reshape(1, 2 * d), "row", 0)],
                       _ep_bias_sigmoid, (m, 2 * d), BF16, tm, tn, "merge_gate")
    ym = panel_dots([(hg, lw["w_branch_m"], "rows", li, 0)], [(gates, "tile", 0)],
                    _ep_gate_mul, (m, d), BF16, tm, tn, "branch_m")
    mix = panel_dots([(zf, lw["w_branch_f"], "rows", li, 0)], [(gates, "tile", d // tn), (ym, "tile", 0)],
                     _ep_gate_mul_add, (m, d), BF16, tm, tn, "branch_f")
    y = panel_dots([(mix, lw["w_out"], "rows", li, 0)], [], _ep_identity, (m, d), BF16, tm, tn, "out_proj")
    x, hn = resnorm(x, y, lw["g_mix_post"][li], lw["g_mlp_pre"][li])

    d_ff = lw["w_up"].shape[-1]
    up = panel_dots([(hn, lw["w_up"], "rows", li, 0)], [], _ep_relu2, (m, d_ff), BF16, tm, tn, "mlp_up")
    w_down = cast_rows(lw["w_down"], li, BF16, _tile(d_ff, 512))
    y = matmul_ktiled(up, w_down, BF16, tm, _tile(d, 1024), _tile(d_ff, 4096), "mlp_down")
    x, xb = resnorm(x, y, lw["g_mlp_post"][li], None)

    ple = panel_dots([(xb, lw["w_ple_gate"], "rows", li, 0), (p_i, lw["w_ple_in"], "small", li, 0)],
                     [], _ep_sigmoid_mul, (m, d), BF16, tm, tn, "ple")
    if g_next is None:
        x, _ = resnorm(x, ple, lw["g_ple_post"][li], None)
        return x, None
    return resnorm(x, ple, lw["g_ple_post"][li], g_next)


def kernel(x, p, g_mix_pre, g_mix_post, g_mlp_pre, g_mlp_post, g_ple_post, w_in, b_igate, b_fgate,
           g_head, w_branch_m, w_branch_f, w_merge, b_merge, w_out, w_up, w_down, w_ple_in, w_ple_gate):
    bsz, seq, d = x.shape
    depth = w_in.shape[0]
    m = bsz * seq
    xf = x.reshape(m, d)
    xn = rmsnorm(xf, g_mix_pre[0])
    lw = dict(w_in_t=jnp.swapaxes(w_in, 1, 2), b_igate=b_igate, b_fgate=b_fgate, g_head=g_head, w_branch_m=w_branch_m,
              w_branch_f=w_branch_f, w_merge=w_merge, b_merge=b_merge, w_out=w_out, w_up=w_up,
              w_down=w_down, w_ple_in=w_ple_in, w_ple_gate=w_ple_gate, g_mix_post=g_mix_post,
              g_mlp_pre=g_mlp_pre, g_mlp_post=g_mlp_post, g_ple_post=g_ple_post)
    for i in range(depth):
        g_next = g_mix_pre[i + 1] if i + 1 < depth else None
        xf, xn = _layer(xf, xn, p[i].reshape(m, -1), dict(lw, layer=i), g_next, bsz=bsz, seq=seq)
    return xf.reshape(bsz, seq, d)
```

```python
import functools
import math

import jax
import jax.numpy as jnp
from jax import lax
from jax.experimental import pallas as pl
from jax.experimental.pallas import tpu as pltpu

F32 = jnp.float32
BF16 = jnp.bfloat16

M_HEADS = 8
F_GROUPS = 4
GATE_CAP = 15.0
EPS = 1e-6
MLSTM_CHUNK = 256
V7X_VMEM_BYTES = 64 * 1024 * 1024
VMEM_LIMIT = V7X_VMEM_BYTES - 4 * 1024 * 1024


def _params(*semantics):
    return pltpu.CompilerParams(dimension_semantics=semantics, vmem_limit_bytes=VMEM_LIMIT)


def _rms(x, g):
    return x * lax.rsqrt(jnp.mean(x * x, axis=-1, keepdims=True) + EPS) * g


def _rmsnorm_kernel(x_ref, g_ref, o_ref):
    o_ref[...] = _rms(x_ref[...], g_ref[...]).astype(o_ref.dtype)


def rmsnorm(x, g, tm=256):
    m, d = x.shape
    return pl.pallas_call(
        _rmsnorm_kernel,
        grid=(m // tm,),
        in_specs=[pl.BlockSpec((tm, d), lambda i: (i, 0)),
                  pl.BlockSpec((1, d), lambda i: (0, 0))],
        out_specs=pl.BlockSpec((tm, d), lambda i: (i, 0)),
        out_shape=jax.ShapeDtypeStruct((m, d), BF16),
        compiler_params=_params("parallel"),
        name="rmsnorm",
    )(x, g.reshape(1, d))


def _resnorm_kernel(x_ref, y_ref, gp_ref, gn_ref, xo_ref, xn_ref, *, norm_next):
    xnew = x_ref[...] + _rms(y_ref[...].astype(F32), gp_ref[...])
    xo_ref[...] = xnew
    if norm_next:
        xn_ref[...] = _rms(xnew, gn_ref[...]).astype(xn_ref.dtype)
    else:
        xn_ref[...] = xnew.astype(xn_ref.dtype)


def resnorm(x, y, g_post, g_next, tm=256):
    m, d = x.shape
    norm_next = g_next is not None
    gn = g_next if norm_next else g_post
    row = pl.BlockSpec((tm, d), lambda i: (i, 0))
    vec = pl.BlockSpec((1, d), lambda i: (0, 0))
    return pl.pallas_call(
        functools.partial(_resnorm_kernel, norm_next=norm_next),
        grid=(m // tm,),
        in_specs=[row, row, vec, vec],
        out_specs=[row, row],
        out_shape=[jax.ShapeDtypeStruct((m, d), F32), jax.ShapeDtypeStruct((m, d), BF16)],
        compiler_params=_params("parallel"),
        name="resnorm",
    )(x, y, g_post.reshape(1, d), gn.reshape(1, d))


def _panel_dots_kernel(*refs, kinds, n_extra, side, epilogue, ck):
    n_dots = len(kinds)
    n_in = 2 * n_dots + n_extra + side
    out_ref = refs[n_in]
    scratch = list(refs[n_in + 1 + side:])
    bufs = [(scratch.pop(0), scratch.pop(0)) if kinds[p] != "small" else None for p in range(n_dots)]
    g = pl.program_id(0)
    i = pl.program_id(1)
    rows = pl.ds(pl.multiple_of(i * ck, ck), ck)

    def stage(parity):
        for p in range(n_dots):
            w_ref = refs[2 * p + 1]
            if kinds[p] == "rows":
                bufs[p][parity][rows, :] = w_ref[...].astype(BF16)
            elif kinds[p] == "cols":
                bufs[p][parity][rows, :] = w_ref[0].T.astype(BF16)
        if side:
            refs[n_in + 1][...] = refs[n_in - 1][...].astype(refs[n_in + 1].dtype)

    def compute(parity):
        accs = []
        for p in range(n_dots):
            w = refs[2 * p + 1][...].astype(BF16) if kinds[p] == "small" else bufs[p][parity][...]
            accs.append(jnp.dot(refs[2 * p][...].astype(BF16), w, preferred_element_type=F32))
        extras = [r[...] for r in refs[2 * n_dots:2 * n_dots + n_extra]]
        out_ref[...] = epilogue(accs, extras).astype(out_ref.dtype)

    @pl.when(g == 0)
    def _():
        stage(0)

    for parity in (0, 1):
        @pl.when((g > 0) & (g % 2 == parity))
        def _(parity=parity):
            stage(parity)
            compute(1 - parity)


def panel_dots(pairs, extras, epilogue, out_shape, out_dtype, tm, tn, name, side_cast=None):
    m, n = out_shape
    n_i, n_j = m // tm, n // tn
    ck = None
    in_specs, args, scratch, kinds = [], [], [], []

    def jj(g):
        return jnp.maximum(g - 1, 0)

    def ii(g, i):
        return jnp.where(g == 0, 0, i)

    def pj(g):
        return jnp.minimum(g, n_j - 1)

    for a, w, kind, layer, where in pairs:
        k = a.shape[1]
        lead = () if layer is None else (layer,)
        lead_blk = () if layer is None else (None,)
        in_specs.append(pl.BlockSpec((tm, k), lambda g, i: (ii(g, i), 0)))
        if kind == "small":
            in_specs.append(pl.BlockSpec(lead_blk + (k, tn),
                                         lambda g, i, lead=lead, c=where: lead + (0, c + jj(g))))
        else:
            assert ck in (None, k // n_i) and k % n_i == 0
            ck = k // n_i
            scratch += [pltpu.VMEM((k, tn), BF16), pltpu.VMEM((k, tn), BF16)]
            if kind == "rows":
                in_specs.append(pl.BlockSpec(lead_blk + (ck, tn),
                                             lambda g, i, lead=lead, c=where: lead + (i, c + pj(g))))
            else:
                in_specs.append(pl.BlockSpec(
                    (pl.Element(1), pl.Element(tn), pl.Element(ck)),
                    lambda g, i, l=layer, f=where, ck=ck: (l, pl.multiple_of(f(pj(g)), 8), i * ck)))
        args += [a, w]
        kinds.append(kind)
    for arr, kind, cb0 in extras:
        if kind == "row":
            in_specs.append(pl.BlockSpec((1, tn), lambda g, i, cb0=cb0: (0, cb0 + jj(g))))
        else:
            in_specs.append(pl.BlockSpec((tm, tn), lambda g, i, cb0=cb0: (ii(g, i), cb0 + jj(g))))
        args.append(arr)
    out_specs = [pl.BlockSpec((tm, tn), lambda g, i: (ii(g, i), jj(g)))]
    out_shapes = [jax.ShapeDtypeStruct((m, n), out_dtype)]
    if side_cast is not None:
        sw, s_layer, s_rows = side_cast
        _, s_k, s_n = sw.shape
        n_blk = s_k // s_rows
        assert s_k % s_rows == 0 and n_blk <= (n_j + 1) * n_i

        def blk(g, i):
            return jnp.minimum(g * n_i + i, n_blk - 1)

        in_specs.append(pl.BlockSpec((None, s_rows, s_n), lambda g, i: (s_layer, blk(g, i), 0)))
        args.append(sw)
        out_specs.append(pl.BlockSpec((s_rows, s_n), lambda g, i: (blk(g, i), 0)))
        out_shapes.append(jax.ShapeDtypeStruct((s_k, s_n), BF16))
    res = pl.pallas_call(
        functools.partial(_panel_dots_kernel, kinds=tuple(kinds), n_extra=len(extras),
                          side=int(side_cast is not None), epilogue=epilogue, ck=ck),
        grid=(n_j + 1, n_i),
        in_specs=in_specs,
        out_specs=out_specs,
        out_shape=out_shapes,
        scratch_shapes=scratch,
        compiler_params=_params("arbitrary", "arbitrary"),
        name=name,
    )(*args)
    return res[0] if side_cast is None else res


def _ep_identity(accs, extras):
    return accs[0]


def _ep_bias_sigmoid(accs, extras):
    return jax.nn.sigmoid(accs[0] + extras[0])


def _ep_relu2(accs, extras):
    return jnp.square(jnp.maximum(accs[0], 0.0))


def _ep_gate_mul(accs, extras):
    return extras[0].astype(F32) * accs[0]


def _ep_gate_mul_add(accs, extras):
    return extras[0].astype(F32) * accs[0] + extras[1].astype(F32)


def _ep_sigmoid_mul(accs, extras):
    return jax.nn.sigmoid(accs[0]) * accs[1]


def _matmul_ktiled_kernel(a_ref, w_ref, o_ref, acc_ref):
    kk = pl.program_id(2)

    @pl.when(kk == 0)
    def _():
        acc_ref[...] = jnp.zeros_like(acc_ref)

    acc_ref[...] += jnp.dot(a_ref[...], w_ref[...], preferred_element_type=F32)

    @pl.when(kk == pl.num_programs(2) - 1)
    def _():
        o_ref[...] = acc_ref[...].astype(o_ref.dtype)


def matmul_ktiled(a, w, out_dtype, tm, tn, tk, name):
    m, k = a.shape
    n = w.shape[1]
    return pl.pallas_call(
        _matmul_ktiled_kernel,
        grid=(m // tm, n // tn, k // tk),
        in_specs=[pl.BlockSpec((tm, tk), lambda i, j, kk: (i, kk)),
                  pl.BlockSpec((tk, tn), lambda i, j, kk: (kk, j))],
        out_specs=pl.BlockSpec((tm, tn), lambda i, j, kk: (i, j)),
        out_shape=jax.ShapeDtypeStruct((m, n), out_dtype),
        scratch_shapes=[pltpu.VMEM((tm, tn), F32)],
        compiler_params=_params("parallel", "parallel", "arbitrary"),
        name=name,
    )(a, w)


def _split3(x):
    hi = x.astype(BF16)
    r1 = x - hi.astype(F32)
    mid = r1.astype(BF16)
    lo = (r1 - mid.astype(F32)).astype(BF16)
    return hi, mid, lo


def _gate_prep_kernel(gi_ref, gf_ref, bi_ref, bf_ref, b_ref, e_ref, r_ref, *, chunk):
    s, w = gi_ref.shape
    nh = w // 2
    log_i = GATE_CAP * jnp.tanh((gi_ref[...] + bi_ref[...]) * (1.0 / GATE_CAP))
    af = GATE_CAP * jnp.tanh((gf_ref[...] + bf_ref[...]) * (1.0 / GATE_CAP))
    log_f = jnp.minimum(af, 0.0) - jnp.log1p(jnp.exp(-jnp.abs(af)))
    ri = lax.broadcasted_iota(jnp.int32, (chunk, chunk), 0)
    ci = lax.broadcasted_iota(jnp.int32, (chunk, chunk), 1)
    tril = (ci <= ri).astype(BF16)
    triu = (ci >= ri).astype(BF16)
    is_fwd = lax.broadcasted_iota(jnp.int32, (chunk, w), 1) < nh
    for c in range(s // chunk):
        rows = slice(c * chunk, (c + 1) * chunk)
        parts = _split3(log_f[rows, :])
        cum_f = sum(jnp.dot(tril, p, preferred_element_type=F32) for p in parts)
        cum_b = sum(jnp.dot(triu, p, preferred_element_type=F32) for p in parts)
        b = jnp.where(is_fwd, cum_f, cum_b)
        tot = jnp.where(is_fwd[:1], cum_f[chunk - 1:chunk, :], cum_b[0:1, :])
        li = log_i[rows, :]
        b_ref[rows, :] = b
        e_ref[rows, :] = tot - b + li
        r_ref[rows, :] = li - b


def gate_prep(gi, gf, bi, bf, chunk):
    bsz, s, w = gi.shape
    blk = pl.BlockSpec((None, s, w), lambda b: (b, 0, 0))
    vec = pl.BlockSpec((1, w), lambda b: (0, 0))
    out = jax.ShapeDtypeStruct((bsz, s, w), F32)
    return pl.pallas_call(
        functools.partial(_gate_prep_kernel, chunk=chunk),
        grid=(bsz,),
        in_specs=[blk, blk, vec, vec],
        out_specs=[blk, blk, blk],
        out_shape=[out, out, out],
        compiler_params=_params("parallel"),
        name="gate_prep",
    )(gi, gf, bi, bf)


def _mlstm_kernel(q_ref, k_ref, v_ref, o_ref, gcol_ref, grow_ref, gh_ref, out_ref,
                  h_ref, c_ref, n_ref, *, chunk, q_scale):
    s = q_ref.shape[0]
    nc = s // chunk
    assert nc % 2 == 0
    ri = lax.broadcasted_iota(jnp.int32, (chunk, chunk), 0)
    ci = lax.broadcasted_iota(jnp.int32, (chunk, chunk), 1)
    masks = (ci <= ri, ci >= ri)
    c_ref[...] = jnp.zeros_like(c_ref)
    n_ref[...] = jnp.zeros_like(n_ref)

    def rows_of(c):
        return pl.ds(pl.multiple_of(c * chunk, chunk), chunk)

    def scan_pair(cf, cb):
        dirs = (0, 1)
        rows = [rows_of(c) for c in (cf, cb)]
        q = [q_ref[r, :] * q_scale for r in rows]
        k = [k_ref[r, :] for r in rows]
        v = [v_ref[r, :] for r in rows]
        gc = [gcol_ref[r, :] for r in rows]
        bcol = [gc[d][:, 2 * d:2 * d + 1] for d in dirs]
        ecol = [gc[d][:, 2 * d + 1:2 * d + 2] for d in dirs]
        rrow = [grow_ref[0, cf], grow_ref[1, cb]]
        last = (chunk - 1, 0)
        decay = [jnp.exp(bcol[d][last[d]:last[d] + 1, :]) for d in dirs]
        c_prev = [c_ref[d] for d in dirs]
        n_prev = [n_ref[d] for d in dirs]

        qk = [lax.dot_general(q[d], k[d], (((1,), (1,)), ((), ())), preferred_element_type=F32)
              for d in dirs]
        w_intra = [jnp.exp(jnp.where(masks[d], bcol[d] + rrow[d], -jnp.inf)) for d in dirs]
        w_inter = [jnp.exp(bcol[d]) for d in dirs]
        sc = [qk[d] * w_intra[d] for d in dirs]
        qf = [q[d].astype(F32) for d in dirs]
        lhs = [jnp.concatenate([sc[d].astype(BF16), (qf[d] * w_inter[d]).astype(BF16)], axis=1) for d in dirs]
        rhs = [jnp.concatenate([v[d], c_prev[d].astype(BF16)], axis=0) for d in dirs]
        num = [jnp.dot(lhs[d], rhs[d], preferred_element_type=F32) for d in dirs]
        qn = [jnp.sum(qf[d] * n_prev[d], axis=1, keepdims=True) for d in dirs]
        den = [jnp.sum(sc[d], axis=1, keepdims=True) + w_inter[d] * qn[d] for d in dirs]
        h = [num[d] * (1.0 / jnp.maximum(jnp.abs(den[d]), 1.0)) for d in dirs]

        kw = [k[d].astype(F32) * jnp.exp(ecol[d]) for d in dirs]
        upd = [lax.dot_general(kw[d].astype(BF16), v[d], (((0,), (0,)), ((), ())),
                               preferred_element_type=F32) for d in dirs]
        for d in dirs:
            c_ref[d] = decay[d] * c_prev[d] + upd[d]
            n_ref[d] = decay[d] * n_prev[d] + jnp.sum(kw[d], axis=0, keepdims=True)
        return h

    def finish(c, tot):
        rows = rows_of(c)
        y = _rms(tot, gh_ref[...]) * jax.nn.sigmoid(o_ref[rows, :].astype(F32))
        out_ref[rows, :] = y.astype(out_ref.dtype)

    def first_half(t, carry):
        hf, hb = scan_pair(t, nc - 1 - t)
        h_ref[rows_of(t), :] = hf
        h_ref[rows_of(nc - 1 - t), :] = hb
        return carry

    def second_half(t, carry):
        cf, cb = t, nc - 1 - t
        hf, hb = scan_pair(cf, cb)
        finish(cf, hf + h_ref[rows_of(cf), :])
        finish(cb, hb + h_ref[rows_of(cb), :])
        return carry

    lax.fori_loop(0, nc // 2, first_half, 0)
    lax.fori_loop(nc // 2, nc, second_half, 0)


def mlstm(u3, gcol, grow, g_head, *, dk, dv, q_cb0, k_cb0, v_cb0, o_cb0, chunk):
    bsz, s, _ = u3.shape
    nh = gcol.shape[1]
    nc = s // chunk

    def col(width, cb0):
        return pl.BlockSpec((None, s, width), lambda b, h, cb0=cb0: (b, 0, cb0 + h))

    return pl.pallas_call(
        functools.partial(_mlstm_kernel, chunk=chunk, q_scale=dk ** -0.5),
        grid=(bsz, nh),
        in_specs=[col(dk, q_cb0), col(dk, k_cb0), col(dv, v_cb0), col(dv, o_cb0),
                  pl.BlockSpec((None, None, s, 4), lambda b, h: (b, h, 0, 0)),
                  pl.BlockSpec((None, None, 2, nc, 1, chunk), lambda b, h: (b, h, 0, 0, 0, 0)),
                  pl.BlockSpec((1, dv), lambda b, h: (0, h))],
        out_specs=pl.BlockSpec((None, s, dv), lambda b, h: (b, 0, h)),
        out_shape=jax.ShapeDtypeStruct((bsz, s, nh * dv), BF16),
        scratch_shapes=[pltpu.VMEM((s, dv), F32), pltpu.VMEM((2, dk, dv), F32), pltpu.VMEM((2, 1, dk), F32)],
        compiler_params=_params("parallel", "parallel"),
        name="mlstm",
    )(u3, u3, u3, u3, gcol, grow, g_head.reshape(1, nh * dv))


def _cos_sin(rows, n_cols, period):
    k = jnp.arange(n_cols, dtype=jnp.int32)
    ang = ((rows[:, None] * k[None, :]) % period).astype(F32) * (2.0 * math.pi / period)
    return jnp.cos(ang), jnp.sin(ang)


def _dft_mats(n, scale):
    n1 = 1 << ((n.bit_length() - 1) // 2)
    n2 = n // n1
    ca, sa = _cos_sin(jnp.arange(n1, dtype=jnp.int32), n, n1)
    cb, sb = _cos_sin(jnp.arange(n2, dtype=jnp.int32), n, n)
    ca, sa, cb, sb = ca[:, None, :], sa[:, None, :], cb[None, :, :], sb[None, :, :]
    c = (ca * cb - sa * sb) * scale
    s = (sa * cb + ca * sb) * scale
    return c.reshape(n, n).astype(BF16), s.reshape(n, n).astype(BF16)


FLIP_ROWS = 256
SUBLANES = 8


def _flip_matrix(r):
    t = jnp.arange(r, dtype=jnp.int32)
    return ((t[:, None] + t[None, :]) == r).astype(BF16)


def _shifted_flip(j_ref, blk_ref, nxt_ref, use_next):
    rev = jnp.dot(j_ref[...], blk_ref[...], preferred_element_type=F32)
    first = lax.broadcasted_iota(jnp.int32, rev.shape, 0) == 0
    row0 = jnp.where(use_next, nxt_ref[0:1, :].astype(F32), 0.0)
    return jnp.where(first, row0, rev)


def _dft_chan_kernel(j_ref, zlo_ref, za_ref, zb_ref, cc_ref, sc_ref, p_ref, q_ref, *, n_fold):
    i = pl.program_id(2)
    a = zlo_ref[...]
    r = _shifted_flip(j_ref, za_ref, zb_ref, i > 0)
    r = jnp.where(i < n_fold, r, 0.0).astype(BF16)
    p_ref[...] = jnp.dot(a + r, cc_ref[...], preferred_element_type=F32).astype(p_ref.dtype)
    q_ref[...] = jnp.dot(a - r, sc_ref[...], preferred_element_type=F32).astype(q_ref.dtype)


def dft_chan(u3, z_cb0, cc, sc, hp, r):
    bsz, n, _ = u3.shape
    gd = cc.shape[0]
    nb = n // r
    out = jax.ShapeDtypeStruct((bsz, hp, F_GROUPS * gd), BF16)
    dst = pl.BlockSpec((None, r, gd), lambda b, g, i: (b, i, g))
    mat = pl.BlockSpec((gd, gd), lambda b, g, i: (0, 0))
    return pl.pallas_call(
        functools.partial(_dft_chan_kernel, n_fold=nb // 2),
        grid=(bsz, F_GROUPS, hp // r),
        in_specs=[pl.BlockSpec((r, r), lambda b, g, i: (0, 0)),
                  pl.BlockSpec((None, r, gd), lambda b, g, i: (b, i, z_cb0 + g)),
                  pl.BlockSpec((None, r, gd), lambda b, g, i: (b, nb - 1 - i, z_cb0 + g)),
                  pl.BlockSpec((None, SUBLANES, gd),
                               lambda b, g, i: (b, jnp.minimum(nb - i, nb - 1) * (r // SUBLANES), z_cb0 + g)),
                  mat, mat],
        out_specs=[dst, dst],
        out_shape=[out, out],
        compiler_params=_params("parallel", "parallel", "arbitrary"),
        name="dft_chan",
    )(_flip_matrix(r), u3, u3, u3, cc, sc)


def _dft_unfold_kernel(j_ref, lo_ref, hi_ref, nxt_ref, o_ref, *, n_fold):
    t = pl.program_id(1)

    @pl.when(t < n_fold)
    def _():
        o_ref[...] = lo_ref[...]

    @pl.when(t >= n_fold)
    def _():
        o_ref[...] = _shifted_flip(j_ref, hi_ref, nxt_ref, True).astype(o_ref.dtype)


def dft_unfold(lo, hi, n, r):
    bsz, _, d = lo.shape
    nb = n // r
    n_fold = nb // 2
    return pl.pallas_call(
        functools.partial(_dft_unfold_kernel, n_fold=n_fold),
        grid=(bsz, nb),
        in_specs=[pl.BlockSpec((r, r), lambda b, t: (0, 0)),
                  pl.BlockSpec((None, r, d), lambda b, t: (b, jnp.minimum(t, n_fold - 1), 0)),
                  pl.BlockSpec((None, r, d), lambda b, t: (b, jnp.minimum(nb - 1 - t, n_fold - 1), 0)),
                  pl.BlockSpec((None, SUBLANES, d),
                               lambda b, t: (b, jnp.minimum(nb - t, n_fold) * (r // SUBLANES), 0))],
        out_specs=pl.BlockSpec((None, r, d), lambda b, t: (b, t, 0)),
        out_shape=jax.ShapeDtypeStruct((bsz, n, d), BF16),
        compiler_params=_params("parallel", "arbitrary"),
        name="dft_unfold",
    )(_flip_matrix(r), lo, hi, hi)


def _dft_seq_kernel(cs_ref, ss_ref, p_ref, q_ref, lo_ref, hi_ref):
    a = jnp.dot(cs_ref[...], p_ref[...], preferred_element_type=F32)
    b = jnp.dot(ss_ref[...], q_ref[...], preferred_element_type=F32)
    lo_ref[...] = (a - b).astype(lo_ref.dtype)
    hi_ref[...] = (a + b).astype(hi_ref.dtype)


def dft_seq(cs, ss, p, q, gd, tm):
    bsz, hp, w = p.shape
    out = jax.ShapeDtypeStruct((bsz, hp, w), BF16)
    mat = pl.BlockSpec((tm, hp), lambda b, g, i: (i, 0))
    src = pl.BlockSpec((None, hp, gd), lambda b, g, i: (b, 0, g))
    dst = pl.BlockSpec((None, tm, gd), lambda b, g, i: (b, i, g))
    return pl.pallas_call(
        _dft_seq_kernel,
        grid=(bsz, w // gd, hp // tm),
        in_specs=[mat, mat, src, src],
        out_specs=[dst, dst],
        out_shape=[out, out],
        compiler_params=_params("parallel", "parallel", "arbitrary"),
        name="dft_seq",
    )(cs, ss, p, q)


def fourier_mix(u3, z_cb0, gd):
    bsz, n, _ = u3.shape
    h = n // 2
    r = min(FLIP_ROWS, h)
    hp = h + r
    cc, sc = _dft_mats(gd, gd ** -0.5)
    cs, ss = _dft_mats(n, n ** -0.5)
    pad = ((0, hp - h - 1), (0, hp - h - 1))
    cs = jnp.pad(cs[:h + 1, :h + 1], pad)
    ss = jnp.pad(ss[:h + 1, :h + 1], pad)
    p, q = dft_chan(u3, z_cb0, cc, sc, hp, r)
    tm = hp // 3 if hp % (3 * SUBLANES) == 0 else hp
    lo, hi = dft_seq(cs, ss, p, q, gd, tm)
    return dft_unfold(lo, hi, n, r)


def _tile(n, pref):
    return pref if n % pref == 0 else n


def _layer(x, xn, p_i, lw, g_next, *, bsz, seq):
    m, d = x.shape
    nh = M_HEADS
    dv = d // nh
    dk = dv // 2
    qk_w, v_w = nh * dk, nh * dv
    gd = d // F_GROUPS
    n_main = 2 * qk_w + 2 * v_w
    g0 = n_main
    z0 = n_main + 4 * nh
    chunk = min(MLSTM_CHUNK, seq)
    tm = _tile(m, 1024)

    li = lw["layer"]
    tn = _tile(d, 1024)
    w_in_t = lw["w_in_t"]
    u = panel_dots([(xn, w_in_t, "cols", li, lambda j: j * tn + jnp.where(j * tn >= g0, z0 - g0, 0))],
                   [], _ep_identity, (m, n_main + d), BF16, tm, tn, "in_proj")
    graw = panel_dots([(xn, w_in_t, "cols", li, lambda j: g0)], [], _ep_identity, (m, 128), F32,
                      tm, 128, "gate_proj")

    graw = graw[:, :4 * nh].reshape(bsz, seq, 2, 2, nh)
    gi = graw[:, :, :, 0].reshape(bsz, seq, 2 * nh)
    gf = graw[:, :, :, 1].reshape(bsz, seq, 2 * nh)
    b_arr, e_arr, r_arr = gate_prep(gi, gf, lw["b_igate"][li].reshape(1, 2 * nh),
                                    lw["b_fgate"][li].reshape(1, 2 * nh), chunk)
    b4 = b_arr.reshape(bsz, seq, 2, nh)
    e4 = e_arr.reshape(bsz, seq, 2, nh)
    gcol = jnp.stack([b4[:, :, 0], e4[:, :, 0], b4[:, :, 1], e4[:, :, 1]], axis=-1)
    gcol = gcol.transpose(0, 2, 1, 3)
    grow = r_arr.reshape(bsz, seq // chunk, chunk, 2, nh).transpose(0, 4, 3, 1, 2)
    grow = grow[:, :, :, :, None, :]
    u3 = u.reshape(bsz, seq, n_main + d)
    hg = mlstm(u3, gcol, grow, lw["g_head"][li], dk=dk, dv=dv, q_cb0=0, k_cb0=qk_w // dk,
               v_cb0=2 * qk_w // dv, o_cb0=(2 * qk_w + v_w) // dv, chunk=chunk)
    hg = hg.reshape(m, v_w)

    zf = fourier_mix(u3, n_main // gd, gd).reshape(m, d)

    gates = panel_dots([(xn, lw["w_merge"], "rows", li, 0)],
                       [(lw["b_merge"][li].reshape(1, 2 * d), "row", 0)],
                       _ep_bias_sigmoid, (m, 2 * d), BF16, tm, tn, "merge_gate")
    ym = panel_dots([(hg, lw["w_branch_m"], "rows", li, 0)], [(gates, "tile", 0)],
                    _ep_gate_mul, (m, d), BF16, tm, tn, "branch_m")
    mix = panel_dots([(zf, lw["w_branch_f"], "rows", li, 0)], [(gates, "tile", d // tn), (ym, "tile", 0)],
                     _ep_gate_mul_add, (m, d), BF16, tm, tn, "branch_f")
    y = panel_dots([(mix, lw["w_out"], "rows", li, 0)], [], _ep_identity, (m, d), BF16, tm, tn, "out_proj")
    x, hn = resnorm(x, y, lw["g_mix_post"][li], lw["g_mlp_pre"][li])

    d_ff = lw["w_up"].shape[-1]
    n_steps = (d_ff // tn + 1) * (m // tm)
    cast_rows_per_step = next(r for r in (128, 256, 512, 1024, d_ff) if d_ff % r == 0 and d_ff // r <= n_steps)
    up, w_down = panel_dots([(hn, lw["w_up"], "rows", li, 0)], [], _ep_relu2, (m, d_ff), BF16, tm, tn,
                            "mlp_up", side_cast=(lw["w_down"], li, cast_rows_per_step))
    y = matmul_ktiled(up, w_down, BF16, tm, _tile(d, 1024), _tile(d_ff, 4096), "mlp_down")
    x, xb = resnorm(x, y, lw["g_mlp_post"][li], None)

    ple = panel_dots([(xb, lw["w_ple_gate"], "rows", li, 0), (p_i, lw["w_ple_in"], "small", li, 0)],
                     [], _ep_sigmoid_mul, (m, d), BF16, tm, tn, "ple")
    if g_next is None:
        x, _ = resnorm(x, ple, lw["g_ple_post"][li], None)
        return x, None
    return resnorm(x, ple, lw["g_ple_post"][li], g_next)


def kernel(x, p, g_mix_pre, g_mix_post, g_mlp_pre, g_mlp_post, g_ple_post, w_in, b_igate, b_fgate,
           g_head, w_branch_m, w_branch_f, w_merge, b_merge, w_out, w_up, w_down, w_ple_in, w_ple_gate):
    bsz, seq, d = x.shape
    depth = w_in.shape[0]
    m = bsz * seq
    xf = x.reshape(m, d)
    xn = rmsnorm(xf, g_mix_pre[0])
    lw = dict(w_in_t=jnp.swapaxes(w_in, 1, 2), b_igate=b_igate, b_fgate=b_fgate, g_head=g_head, w_branch_m=w_branch_m,
              w_branch_f=w_branch_f, w_merge=w_merge, b_merge=b_merge, w_out=w_out, w_up=w_up,
              w_down=w_down, w_ple_in=w_ple_in, w_ple_gate=w_ple_gate, g_mix_post=g_mix_post,
              g_mlp_pre=g_mlp_pre, g_mlp_post=g_mlp_post, g_ple_post=g_ple_post)
    for i in range(depth):
        g_next = g_mix_pre[i + 1] if i + 1 < depth else None
        xf, xn = _layer(xf, xn, p[i].reshape(m, -1), dict(lw, layer=i), g_next, bsz=bsz, seq=seq)
    return xf.reshape(bsz, seq, d)
```

```python
import functools
import math

import jax
import jax.numpy as jnp
from jax import lax
from jax.experimental import pallas as pl
from jax.experimental.pallas import tpu as pltpu

F32 = jnp.float32
BF16 = jnp.bfloat16

M_HEADS = 8
F_GROUPS = 4
GATE_CAP = 15.0
EPS = 1e-6
MLSTM_CHUNK = 256
V7X_VMEM_BYTES = 64 * 1024 * 1024
VMEM_LIMIT = V7X_VMEM_BYTES - 4 * 1024 * 1024


def _params(*semantics):
    return pltpu.CompilerParams(dimension_semantics=semantics, vmem_limit_bytes=VMEM_LIMIT)


def _rms(x, g):
    return x * lax.rsqrt(jnp.mean(x * x, axis=-1, keepdims=True) + EPS) * g


def _sigmoid(x):
    return 0.5 * jnp.tanh(0.5 * x) + 0.5


def _rmsnorm_kernel(x_ref, g_ref, o_ref):
    o_ref[...] = _rms(x_ref[...], g_ref[...]).astype(o_ref.dtype)


def rmsnorm(x, g, tm=256):
    m, d = x.shape
    return pl.pallas_call(
        _rmsnorm_kernel,
        grid=(m // tm,),
        in_specs=[pl.BlockSpec((tm, d), lambda i: (i, 0)),
                  pl.BlockSpec((1, d), lambda i: (0, 0))],
        out_specs=pl.BlockSpec((tm, d), lambda i: (i, 0)),
        out_shape=jax.ShapeDtypeStruct((m, d), BF16),
        compiler_params=_params("parallel"),
        name="rmsnorm",
    )(x, g.reshape(1, d))


def _resnorm_kernel(x_ref, y_ref, gp_ref, gn_ref, xo_ref, *xn_ref):
    xnew = x_ref[...].astype(F32) + _rms(y_ref[...].astype(F32), gp_ref[...])
    xo_ref[...] = xnew.astype(xo_ref.dtype)
    if xn_ref:
        xn_ref[0][...] = _rms(xnew, gn_ref[...]).astype(xn_ref[0].dtype)


def resnorm(x, y, g_post, g_next, x_dtype, tm=256):
    m, d = x.shape
    gn = g_post if g_next is None else g_next
    row = pl.BlockSpec((tm, d), lambda i: (i, 0))
    vec = pl.BlockSpec((1, d), lambda i: (0, 0))
    out_shape = [jax.ShapeDtypeStruct((m, d), x_dtype)]
    if g_next is not None:
        out_shape.append(jax.ShapeDtypeStruct((m, d), BF16))
    return pl.pallas_call(
        _resnorm_kernel,
        grid=(m // tm,),
        in_specs=[row, row, vec, vec],
        out_specs=[row] * len(out_shape),
        out_shape=out_shape,
        compiler_params=_params("parallel"),
        name="resnorm",
    )(x, y, g_post.reshape(1, d), gn.reshape(1, d))


def _panel_dots_kernel(*refs, kinds, n_extra, side, epilogue, ck):
    n_dots = len(kinds)
    n_in = 2 * n_dots + n_extra + side
    out_ref = refs[n_in]
    scratch = list(refs[n_in + 1 + side:])
    bufs = [(scratch.pop(0), scratch.pop(0)) if kinds[p] != "small" else None for p in range(n_dots)]
    g = pl.program_id(0)
    i = pl.program_id(1)
    rows = pl.ds(pl.multiple_of(i * ck, ck), ck)

    def stage(parity):
        for p in range(n_dots):
            w_ref = refs[2 * p + 1]
            if kinds[p] == "rows":
                bufs[p][parity][rows, :] = w_ref[...].astype(BF16)
            elif kinds[p] == "cols":
                bufs[p][parity][rows, :] = w_ref[0].T.astype(BF16)
        if side:
            refs[n_in + 1][...] = refs[n_in - 1][...].astype(refs[n_in + 1].dtype)

    def compute(parity):
        accs = []
        for p in range(n_dots):
            w = refs[2 * p + 1][...].astype(BF16) if kinds[p] == "small" else bufs[p][parity][...]
            accs.append(jnp.dot(refs[2 * p][...].astype(BF16), w, preferred_element_type=F32))
        extras = [r[...] for r in refs[2 * n_dots:2 * n_dots + n_extra]]
        out_ref[...] = epilogue(accs, extras).astype(out_ref.dtype)

    @pl.when(g == 0)
    def _():
        stage(0)

    for parity in (0, 1):
        @pl.when((g > 0) & (g % 2 == parity))
        def _(parity=parity):
            stage(parity)
            compute(1 - parity)


def panel_dots(pairs, extras, epilogue, out_shape, out_dtype, tm, tn, name, side_cast=None):
    m, n = out_shape
    n_i, n_j = m // tm, n // tn
    ck = None
    in_specs, args, scratch, kinds = [], [], [], []

    def jj(g):
        return jnp.maximum(g - 1, 0)

    def ii(g, i):
        return jnp.where(g == 0, 0, i)

    def pj(g):
        return jnp.minimum(g, n_j - 1)

    for a, w, kind, layer, where in pairs:
        k = a.shape[1]
        lead = () if layer is None else (layer,)
        lead_blk = () if layer is None else (None,)
        in_specs.append(pl.BlockSpec((tm, k), lambda g, i: (ii(g, i), 0)))
        if kind == "small":
            in_specs.append(pl.BlockSpec(lead_blk + (k, tn),
                                         lambda g, i, lead=lead, c=where: lead + (0, c + jj(g))))
        else:
            assert ck in (None, k // n_i) and k % n_i == 0
            ck = k // n_i
            scratch += [pltpu.VMEM((k, tn), BF16), pltpu.VMEM((k, tn), BF16)]
            if kind == "rows":
                in_specs.append(pl.BlockSpec(lead_blk + (ck, tn),
                                             lambda g, i, lead=lead, c=where: lead + (i, c + pj(g))))
            else:
                in_specs.append(pl.BlockSpec(
                    (pl.Element(1), pl.Element(tn), pl.Element(ck)),
                    lambda g, i, l=layer, f=where, ck=ck: (l, pl.multiple_of(f(pj(g)), 8), i * ck)))
        args += [a, w]
        kinds.append(kind)
    for arr, kind, cb0 in extras:
        if kind == "row":
            in_specs.append(pl.BlockSpec((1, tn), lambda g, i, cb0=cb0: (0, cb0 + jj(g))))
        else:
            in_specs.append(pl.BlockSpec((tm, tn), lambda g, i, cb0=cb0: (ii(g, i), cb0 + jj(g))))
        args.append(arr)
    out_specs = [pl.BlockSpec((tm, tn), lambda g, i: (ii(g, i), jj(g)))]
    out_shapes = [jax.ShapeDtypeStruct((m, n), out_dtype)]
    if side_cast is not None:
        sw, s_layer, s_rows = side_cast
        _, s_k, s_n = sw.shape
        n_blk = s_k // s_rows
        assert s_k % s_rows == 0 and n_blk <= (n_j + 1) * n_i

        def blk(g, i):
            return jnp.minimum(g * n_i + i, n_blk - 1)

        in_specs.append(pl.BlockSpec((None, s_rows, s_n), lambda g, i: (s_layer, blk(g, i), 0)))
        args.append(sw)
        out_specs.append(pl.BlockSpec((s_rows, s_n), lambda g, i: (blk(g, i), 0)))
        out_shapes.append(jax.ShapeDtypeStruct((s_k, s_n), BF16))
    res = pl.pallas_call(
        functools.partial(_panel_dots_kernel, kinds=tuple(kinds), n_extra=len(extras),
                          side=int(side_cast is not None), epilogue=epilogue, ck=ck),
        grid=(n_j + 1, n_i),
        in_specs=in_specs,
        out_specs=out_specs,
        out_shape=out_shapes,
        scratch_shapes=scratch,
        compiler_params=_params("arbitrary", "arbitrary"),
        name=name,
    )(*args)
    return res[0] if side_cast is None else res


def _ep_identity(accs, extras):
    return accs[0]


def _ep_bias_sigmoid(accs, extras):
    return _sigmoid(accs[0] + extras[0])


def _ep_relu2(accs, extras):
    return jnp.square(jnp.maximum(accs[0], 0.0))


def _ep_gate_mul(accs, extras):
    return extras[0].astype(F32) * accs[0]


def _ep_gate_mul_add(accs, extras):
    return extras[0].astype(F32) * accs[0] + extras[1].astype(F32)


def _ep_sigmoid_mul(accs, extras):
    return _sigmoid(accs[0]) * accs[1]


def _matmul_ktiled_kernel(a_ref, w_ref, o_ref, acc_ref):
    kk = pl.program_id(2)

    @pl.when(kk == 0)
    def _():
        acc_ref[...] = jnp.zeros_like(acc_ref)

    acc_ref[...] += jnp.dot(a_ref[...], w_ref[...], preferred_element_type=F32)

    @pl.when(kk == pl.num_programs(2) - 1)
    def _():
        o_ref[...] = acc_ref[...].astype(o_ref.dtype)


def matmul_ktiled(a, w, out_dtype, tm, tn, tk, name):
    m, k = a.shape
    n = w.shape[1]
    return pl.pallas_call(
        _matmul_ktiled_kernel,
        grid=(m // tm, n // tn, k // tk),
        in_specs=[pl.BlockSpec((tm, tk), lambda i, j, kk: (i, kk)),
                  pl.BlockSpec((tk, tn), lambda i, j, kk: (kk, j))],
        out_specs=pl.BlockSpec((tm, tn), lambda i, j, kk: (i, j)),
        out_shape=jax.ShapeDtypeStruct((m, n), out_dtype),
        scratch_shapes=[pltpu.VMEM((tm, tn), F32)],
        compiler_params=_params("parallel", "parallel", "arbitrary"),
        name=name,
    )(a, w)


def _split3(x):
    hi = x.astype(BF16)
    r1 = x - hi.astype(F32)
    mid = r1.astype(BF16)
    lo = (r1 - mid.astype(F32)).astype(BF16)
    return hi, mid, lo


def _gate_prep_kernel(gi_ref, gf_ref, bi_ref, bf_ref, b_ref, e_ref, r_ref, *, chunk):
    s, w = gi_ref.shape
    nh = w // 2
    log_i = GATE_CAP * jnp.tanh((gi_ref[...] + bi_ref[...]) * (1.0 / GATE_CAP))
    af = GATE_CAP * jnp.tanh((gf_ref[...] + bf_ref[...]) * (1.0 / GATE_CAP))
    log_f = jnp.minimum(af, 0.0) - jnp.log1p(jnp.exp(-jnp.abs(af)))
    ri = lax.broadcasted_iota(jnp.int32, (chunk, chunk), 0)
    ci = lax.broadcasted_iota(jnp.int32, (chunk, chunk), 1)
    tril = (ci <= ri).astype(BF16)
    triu = (ci >= ri).astype(BF16)
    is_fwd = lax.broadcasted_iota(jnp.int32, (chunk, w), 1) < nh
    for c in range(s // chunk):
        rows = slice(c * chunk, (c + 1) * chunk)
        parts = _split3(log_f[rows, :])
        cum_f = sum(jnp.dot(tril, p, preferred_element_type=F32) for p in parts)
        cum_b = sum(jnp.dot(triu, p, preferred_element_type=F32) for p in parts)
        b = jnp.where(is_fwd, cum_f, cum_b)
        tot = jnp.where(is_fwd[:1], cum_f[chunk - 1:chunk, :], cum_b[0:1, :])
        li = log_i[rows, :]
        b_ref[rows, :] = b
        e_ref[rows, :] = tot - b + li
        r_ref[rows, :] = li - b


def gate_prep(gi, gf, bi, bf, chunk):
    bsz, s, w = gi.shape
    blk = pl.BlockSpec((None, s, w), lambda b: (b, 0, 0))
    vec = pl.BlockSpec((1, w), lambda b: (0, 0))
    out = jax.ShapeDtypeStruct((bsz, s, w), F32)
    return pl.pallas_call(
        functools.partial(_gate_prep_kernel, chunk=chunk),
        grid=(bsz,),
        in_specs=[blk, blk, vec, vec],
        out_specs=[blk, blk, blk],
        out_shape=[out, out, out],
        compiler_params=_params("parallel"),
        name="gate_prep",
    )(gi, gf, bi, bf)


def _mlstm_kernel(q_ref, k_ref, v_ref, o_ref, gcol_ref, grow_ref, gh_ref, out_ref,
                  h_ref, c_ref, n_ref, *, chunk, q_scale):
    s = q_ref.shape[0]
    nc = s // chunk
    assert nc % 2 == 0
    ri = lax.broadcasted_iota(jnp.int32, (chunk, chunk), 0)
    ci = lax.broadcasted_iota(jnp.int32, (chunk, chunk), 1)
    masks = (ci <= ri, ci >= ri)
    c_ref[...] = jnp.zeros_like(c_ref)
    n_ref[...] = jnp.zeros_like(n_ref)

    def rows_of(c):
        return pl.ds(pl.multiple_of(c * chunk, chunk), chunk)

    def scan_pair(cf, cb):
        dirs = (0, 1)
        rows = [rows_of(c) for c in (cf, cb)]
        q = [q_ref[r, :] * q_scale for r in rows]
        k = [k_ref[r, :] for r in rows]
        v = [v_ref[r, :] for r in rows]
        gc = [gcol_ref[r, :] for r in rows]
        bcol = [gc[d][:, 2 * d:2 * d + 1] for d in dirs]
        ecol = [gc[d][:, 2 * d + 1:2 * d + 2] for d in dirs]
        rrow = [grow_ref[0, cf], grow_ref[1, cb]]
        last = (chunk - 1, 0)
        decay = [jnp.exp(bcol[d][last[d]:last[d] + 1, :]) for d in dirs]
        c_prev = [c_ref[d] for d in dirs]
        n_prev = [n_ref[d] for d in dirs]

        qk = [lax.dot_general(q[d], k[d], (((1,), (1,)), ((), ())), preferred_element_type=F32)
              for d in dirs]
        w_intra = [jnp.exp(jnp.where(masks[d], bcol[d] + rrow[d], -jnp.inf)) for d in dirs]
        w_inter = [jnp.exp(bcol[d]) for d in dirs]
        sc = [qk[d] * w_intra[d] for d in dirs]
        qf = [q[d].astype(F32) for d in dirs]
        lhs = [jnp.concatenate([sc[d].astype(BF16), (qf[d] * w_inter[d]).astype(BF16)], axis=1) for d in dirs]
        rhs = [jnp.concatenate([v[d], c_prev[d].astype(BF16)], axis=0) for d in dirs]
        num = [jnp.dot(lhs[d], rhs[d], preferred_element_type=F32) for d in dirs]
        qn = [jnp.sum(qf[d] * n_prev[d], axis=1, keepdims=True) for d in dirs]
        den = [jnp.sum(sc[d], axis=1, keepdims=True) + w_inter[d] * qn[d] for d in dirs]
        h = [num[d] * (1.0 / jnp.maximum(jnp.abs(den[d]), 1.0)) for d in dirs]

        kw = [k[d].astype(F32) * jnp.exp(ecol[d]) for d in dirs]
        upd = [lax.dot_general(kw[d].astype(BF16), v[d], (((0,), (0,)), ((), ())),
                               preferred_element_type=F32) for d in dirs]
        for d in dirs:
            c_ref[d] = decay[d] * c_prev[d] + upd[d]
            n_ref[d] = decay[d] * n_prev[d] + jnp.sum(kw[d], axis=0, keepdims=True)
        return h

    def finish(c, tot):
        rows = rows_of(c)
        y = _rms(tot, gh_ref[...]) * _sigmoid(o_ref[rows, :].astype(F32))
        out_ref[rows, :] = y.astype(out_ref.dtype)

    def first_half(t, carry):
        hf, hb = scan_pair(t, nc - 1 - t)
        h_ref[rows_of(t), :] = hf
        h_ref[rows_of(nc - 1 - t), :] = hb
        return carry

    def second_half(t, carry):
        cf, cb = t, nc - 1 - t
        hf, hb = scan_pair(cf, cb)
        finish(cf, hf + h_ref[rows_of(cf), :])
        finish(cb, hb + h_ref[rows_of(cb), :])
        return carry

    lax.fori_loop(0, nc // 2, first_half, 0)
    lax.fori_loop(nc // 2, nc, second_half, 0)


def mlstm(u3, gcol, grow, g_head, *, dk, dv, q_cb0, k_cb0, v_cb0, o_cb0, chunk):
    bsz, s, _ = u3.shape
    nh = gcol.shape[1]
    nc = s // chunk

    def col(width, cb0):
        return pl.BlockSpec((None, s, width), lambda b, h, cb0=cb0: (b, 0, cb0 + h))

    return pl.pallas_call(
        functools.partial(_mlstm_kernel, chunk=chunk, q_scale=dk ** -0.5),
        grid=(bsz, nh),
        in_specs=[col(dk, q_cb0), col(dk, k_cb0), col(dv, v_cb0), col(dv, o_cb0),
                  pl.BlockSpec((None, None, s, 4), lambda b, h: (b, h, 0, 0)),
                  pl.BlockSpec((None, None, 2, nc, 1, chunk), lambda b, h: (b, h, 0, 0, 0, 0)),
                  pl.BlockSpec((1, dv), lambda b, h: (0, h))],
        out_specs=pl.BlockSpec((None, s, dv), lambda b, h: (b, 0, h)),
        out_shape=jax.ShapeDtypeStruct((bsz, s, nh * dv), BF16),
        scratch_shapes=[pltpu.VMEM((s, dv), F32), pltpu.VMEM((2, dk, dv), F32), pltpu.VMEM((2, 1, dk), F32)],
        compiler_params=_params("parallel", "parallel"),
        name="mlstm",
    )(u3, u3, u3, u3, gcol, grow, g_head.reshape(1, nh * dv))


def _cos_sin(rows, n_cols, period):
    k = jnp.arange(n_cols, dtype=jnp.int32)
    ang = ((rows[:, None] * k[None, :]) % period).astype(F32) * (2.0 * math.pi / period)
    return jnp.cos(ang), jnp.sin(ang)


def _dft_mats(n, scale):
    n1 = 1 << ((n.bit_length() - 1) // 2)
    n2 = n // n1
    ca, sa = _cos_sin(jnp.arange(n1, dtype=jnp.int32), n, n1)
    cb, sb = _cos_sin(jnp.arange(n2, dtype=jnp.int32), n, n)
    ca, sa, cb, sb = ca[:, None, :], sa[:, None, :], cb[None, :, :], sb[None, :, :]
    c = (ca * cb - sa * sb) * scale
    s = (sa * cb + ca * sb) * scale
    return c.reshape(n, n).astype(BF16), s.reshape(n, n).astype(BF16)


FLIP_ROWS = 256
SUBLANES = 8


def _flip_matrix(r):
    t = jnp.arange(r, dtype=jnp.int32)
    return ((t[:, None] + t[None, :]) == r).astype(BF16)


def _shifted_flip(j_ref, blk_ref, nxt_ref, use_next):
    rev = jnp.dot(j_ref[...], blk_ref[...], preferred_element_type=F32)
    first = lax.broadcasted_iota(jnp.int32, rev.shape, 0) == 0
    row0 = jnp.where(use_next, nxt_ref[0:1, :].astype(F32), 0.0)
    return jnp.where(first, row0, rev)


def _dft_chan_kernel(j_ref, zlo_ref, za_ref, zb_ref, cc_ref, sc_ref, p_ref, q_ref, *, n_fold):
    i = pl.program_id(2)
    a = zlo_ref[...]
    r = _shifted_flip(j_ref, za_ref, zb_ref, i > 0)
    r = jnp.where(i < n_fold, r, 0.0).astype(BF16)
    p_ref[...] = jnp.dot(a + r, cc_ref[...], preferred_element_type=F32).astype(p_ref.dtype)
    q_ref[...] = jnp.dot(a - r, sc_ref[...], preferred_element_type=F32).astype(q_ref.dtype)


def dft_chan(u3, z_cb0, cc, sc, hp, r):
    bsz, n, _ = u3.shape
    gd = cc.shape[0]
    nb = n // r
    out = jax.ShapeDtypeStruct((bsz, hp, F_GROUPS * gd), BF16)
    dst = pl.BlockSpec((None, r, gd), lambda b, g, i: (b, i, g))
    mat = pl.BlockSpec((gd, gd), lambda b, g, i: (0, 0))
    return pl.pallas_call(
        functools.partial(_dft_chan_kernel, n_fold=nb // 2),
        grid=(bsz, F_GROUPS, hp // r),
        in_specs=[pl.BlockSpec((r, r), lambda b, g, i: (0, 0)),
                  pl.BlockSpec((None, r, gd), lambda b, g, i: (b, i, z_cb0 + g)),
                  pl.BlockSpec((None, r, gd), lambda b, g, i: (b, nb - 1 - i, z_cb0 + g)),
                  pl.BlockSpec((None, SUBLANES, gd),
                               lambda b, g, i: (b, jnp.minimum(nb - i, nb - 1) * (r // SUBLANES), z_cb0 + g)),
                  mat, mat],
        out_specs=[dst, dst],
        out_shape=[out, out],
        compiler_params=_params("parallel", "parallel", "arbitrary"),
        name="dft_chan",
    )(_flip_matrix(r), u3, u3, u3, cc, sc)


def _dft_unfold_kernel(j_ref, lo_ref, hi_ref, nxt_ref, o_ref, *, n_fold):
    t = pl.program_id(1)

    @pl.when(t < n_fold)
    def _():
        o_ref[...] = lo_ref[...]

    @pl.when(t >= n_fold)
    def _():
        o_ref[...] = _shifted_flip(j_ref, hi_ref, nxt_ref, True).astype(o_ref.dtype)


def dft_unfold(lo, hi, n, r):
    bsz, _, d = lo.shape
    nb = n // r
    n_fold = nb // 2
    return pl.pallas_call(
        functools.partial(_dft_unfold_kernel, n_fold=n_fold),
        grid=(bsz, nb),
        in_specs=[pl.BlockSpec((r, r), lambda b, t: (0, 0)),
                  pl.BlockSpec((None, r, d), lambda b, t: (b, jnp.minimum(t, n_fold - 1), 0)),
                  pl.BlockSpec((None, r, d), lambda b, t: (b, jnp.minimum(nb - 1 - t, n_fold - 1), 0)),
                  pl.BlockSpec((None, SUBLANES, d),
                               lambda b, t: (b, jnp.minimum(nb - t, n_fold) * (r // SUBLANES), 0))],
        out_specs=pl.BlockSpec((None, r, d), lambda b, t: (b, t, 0)),
        out_shape=jax.ShapeDtypeStruct((bsz, n, d), BF16),
        compiler_params=_params("parallel", "arbitrary"),
        name="dft_unfold",
    )(_flip_matrix(r), lo, hi, hi)


def _dft_seq_kernel(cs_ref, ss_ref, p_ref, q_ref, lo_ref, hi_ref):
    a = jnp.dot(cs_ref[...], p_ref[...], preferred_element_type=F32)
    b = jnp.dot(ss_ref[...], q_ref[...], preferred_element_type=F32)
    lo_ref[...] = (a - b).astype(lo_ref.dtype)
    hi_ref[...] = (a + b).astype(hi_ref.dtype)


def dft_seq(cs, ss, p, q, gd, tm):
    bsz, hp, w = p.shape
    out = jax.ShapeDtypeStruct((bsz, hp, w), BF16)
    mat = pl.BlockSpec((tm, hp), lambda b, g, i: (i, 0))
    src = pl.BlockSpec((None, hp, gd), lambda b, g, i: (b, 0, g))
    dst = pl.BlockSpec((None, tm, gd), lambda b, g, i: (b, i, g))
    return pl.pallas_call(
        _dft_seq_kernel,
        grid=(bsz, w // gd, hp // tm),
        in_specs=[mat, mat, src, src],
        out_specs=[dst, dst],
        out_shape=[out, out],
        compiler_params=_params("parallel", "parallel", "arbitrary"),
        name="dft_seq",
    )(cs, ss, p, q)


def fourier_mix(u3, z_cb0, gd):
    bsz, n, _ = u3.shape
    h = n // 2
    r = min(FLIP_ROWS, h)
    hp = h + r
    cc, sc = _dft_mats(gd, gd ** -0.5)
    cs, ss = _dft_mats(n, n ** -0.5)
    pad = ((0, hp - h - 1), (0, hp - h - 1))
    cs = jnp.pad(cs[:h + 1, :h + 1], pad)
    ss = jnp.pad(ss[:h + 1, :h + 1], pad)
    p, q = dft_chan(u3, z_cb0, cc, sc, hp, r)
    tm = hp // 3 if hp % (3 * SUBLANES) == 0 else hp
    lo, hi = dft_seq(cs, ss, p, q, gd, tm)
    return dft_unfold(lo, hi, n, r)


def _tile(n, pref):
    return pref if n % pref == 0 else n


def _layer(x, xn, p_i, lw, g_next, *, bsz, seq):
    m, d = x.shape
    nh = M_HEADS
    dv = d // nh
    dk = dv // 2
    qk_w, v_w = nh * dk, nh * dv
    gd = d // F_GROUPS
    n_main = 2 * qk_w + 2 * v_w
    g0 = n_main
    z0 = n_main + 4 * nh
    chunk = min(MLSTM_CHUNK, seq)
    tm = _tile(m, 1024)

    li = lw["layer"]
    tn = _tile(d, 1024)
    w_in_t = lw["w_in_t"]
    u = panel_dots([(xn, w_in_t, "cols", li, lambda j: j * tn + jnp.where(j * tn >= g0, z0 - g0, 0))],
                   [], _ep_identity, (m, n_main + d), BF16, tm, tn, "in_proj")
    graw = panel_dots([(xn, w_in_t, "cols", li, lambda j: g0)], [], _ep_identity, (m, 128), F32,
                      tm, 128, "gate_proj")

    graw = graw[:, :4 * nh].reshape(bsz, seq, 2, 2, nh)
    gi = graw[:, :, :, 0].reshape(bsz, seq, 2 * nh)
    gf = graw[:, :, :, 1].reshape(bsz, seq, 2 * nh)
    b_arr, e_arr, r_arr = gate_prep(gi, gf, lw["b_igate"][li].reshape(1, 2 * nh),
                                    lw["b_fgate"][li].reshape(1, 2 * nh), chunk)
    b4 = b_arr.reshape(bsz, seq, 2, nh)
    e4 = e_arr.reshape(bsz, seq, 2, nh)
    gcol = jnp.stack([b4[:, :, 0], e4[:, :, 0], b4[:, :, 1], e4[:, :, 1]], axis=-1)
    gcol = gcol.transpose(0, 2, 1, 3)
    grow = r_arr.reshape(bsz, seq // chunk, chunk, 2, nh).transpose(0, 4, 3, 1, 2)
    grow = grow[:, :, :, :, None, :]
    u3 = u.reshape(bsz, seq, n_main + d)
    hg = mlstm(u3, gcol, grow, lw["g_head"][li], dk=dk, dv=dv, q_cb0=0, k_cb0=qk_w // dk,
               v_cb0=2 * qk_w // dv, o_cb0=(2 * qk_w + v_w) // dv, chunk=chunk)
    hg = hg.reshape(m, v_w)

    zf = fourier_mix(u3, n_main // gd, gd).reshape(m, d)

    gates = panel_dots([(xn, lw["w_merge"], "rows", li, 0)],
                       [(lw["b_merge"][li].reshape(1, 2 * d), "row", 0)],
                       _ep_bias_sigmoid, (m, 2 * d), BF16, tm, tn, "merge_gate")
    ym = panel_dots([(hg, lw["w_branch_m"], "rows", li, 0)], [(gates, "tile", 0)],
                    _ep_gate_mul, (m, d), BF16, tm, tn, "branch_m")
    mix = panel_dots([(zf, lw["w_branch_f"], "rows", li, 0)], [(gates, "tile", d // tn), (ym, "tile", 0)],
                     _ep_gate_mul_add, (m, d), BF16, tm, tn, "branch_f")
    y = panel_dots([(mix, lw["w_out"], "rows", li, 0)], [], _ep_identity, (m, d), BF16, tm, tn, "out_proj")
    x, hn = resnorm(x, y, lw["g_mix_post"][li], lw["g_mlp_pre"][li], BF16)

    d_ff = lw["w_up"].shape[-1]
    n_steps = (d_ff // tn + 1) * (m // tm)
    cast_rows_per_step = next(r for r in (128, 256, 512, 1024, d_ff) if d_ff % r == 0 and d_ff // r <= n_steps)
    up, w_down = panel_dots([(hn, lw["w_up"], "rows", li, 0)], [], _ep_relu2, (m, d_ff), BF16, tm, tn,
                            "mlp_up", side_cast=(lw["w_down"], li, cast_rows_per_step))
    y = matmul_ktiled(up, w_down, BF16, tm, _tile(d, 1024), _tile(d_ff, 4096), "mlp_down")
    x, = resnorm(x, y, lw["g_mlp_post"][li], None, BF16)

    ple = panel_dots([(x, lw["w_ple_gate"], "rows", li, 0), (p_i, lw["w_ple_in"], "small", li, 0)],
                     [], _ep_sigmoid_mul, (m, d), BF16, tm, tn, "ple")
    if g_next is None:
        x, = resnorm(x, ple, lw["g_ple_post"][li], None, F32)
        return x, None
    return resnorm(x, ple, lw["g_ple_post"][li], g_next, BF16)


def kernel(x, p, g_mix_pre, g_mix_post, g_mlp_pre, g_mlp_post, g_ple_post, w_in, b_igate, b_fgate,
           g_head, w_branch_m, w_branch_f, w_merge, b_merge, w_out, w_up, w_down, w_ple_in, w_ple_gate):
    bsz, seq, d = x.shape
    depth = w_in.shape[0]
    m = bsz * seq
    xf = x.reshape(m, d)
    xn = rmsnorm(xf, g_mix_pre[0])
    lw = dict(w_in_t=jnp.swapaxes(w_in, 1, 2), b_igate=b_igate, b_fgate=b_fgate, g_head=g_head, w_branch_m=w_branch_m,
              w_branch_f=w_branch_f, w_merge=w_merge, b_merge=b_merge, w_out=w_out, w_up=w_up,
              w_down=w_down, w_ple_in=w_ple_in, w_ple_gate=w_ple_gate, g_mix_post=g_mix_post,
              g_mlp_pre=g_mlp_pre, g_mlp_post=g_mlp_post, g_ple_post=g_ple_post)
    for i in range(depth):
        g_next = g_mix_pre[i + 1] if i + 1 < depth else None
        xf, xn = _layer(xf, xn, p[i].reshape(m, -1), dict(lw, layer=i), g_next, bsz=bsz, seq=seq)
    return xf.reshape(bsz, seq, d)
```

```python
import functools
import math

import jax
import jax.numpy as jnp
from jax import lax
from jax.experimental import pallas as pl
from jax.experimental.pallas import tpu as pltpu

F32 = jnp.float32
BF16 = jnp.bfloat16

M_HEADS = 8
F_GROUPS = 4
GATE_CAP = 15.0
EPS = 1e-6
MLSTM_CHUNK = 256
V7X_VMEM_BYTES = 64 * 1024 * 1024
VMEM_LIMIT = V7X_VMEM_BYTES - 4 * 1024 * 1024


def _params(*semantics):
    return pltpu.CompilerParams(dimension_semantics=semantics, vmem_limit_bytes=VMEM_LIMIT)


def _rms(x, g):
    return x * lax.rsqrt(jnp.mean(x * x, axis=-1, keepdims=True) + EPS) * g


def _sigmoid(x):
    return 0.5 * jnp.tanh(0.5 * x) + 0.5


def _rmsnorm_kernel(x_ref, g_ref, o_ref):
    o_ref[...] = _rms(x_ref[...], g_ref[...]).astype(o_ref.dtype)


def rmsnorm(x, g, tm=256):
    m, d = x.shape
    return pl.pallas_call(
        _rmsnorm_kernel,
        grid=(m // tm,),
        in_specs=[pl.BlockSpec((tm, d), lambda i: (i, 0)),
                  pl.BlockSpec((1, d), lambda i: (0, 0))],
        out_specs=pl.BlockSpec((tm, d), lambda i: (i, 0)),
        out_shape=jax.ShapeDtypeStruct((m, d), BF16),
        compiler_params=_params("parallel"),
        name="rmsnorm",
    )(x, g.reshape(1, d))


def _resnorm_kernel(x_ref, y_ref, gp_ref, gn_ref, xo_ref, *xn_ref):
    xnew = x_ref[...].astype(F32) + _rms(y_ref[...].astype(F32), gp_ref[...])
    xo_ref[...] = xnew.astype(xo_ref.dtype)
    if xn_ref:
        xn_ref[0][...] = _rms(xnew, gn_ref[...]).astype(xn_ref[0].dtype)


def resnorm(x, y, g_post, g_next, x_dtype, tm=256):
    m, d = x.shape
    gn = g_post if g_next is None else g_next
    row = pl.BlockSpec((tm, d), lambda i: (i, 0))
    vec = pl.BlockSpec((1, d), lambda i: (0, 0))
    out_shape = [jax.ShapeDtypeStruct((m, d), x_dtype)]
    if g_next is not None:
        out_shape.append(jax.ShapeDtypeStruct((m, d), BF16))
    return pl.pallas_call(
        _resnorm_kernel,
        grid=(m // tm,),
        in_specs=[row, row, vec, vec],
        out_specs=[row] * len(out_shape),
        out_shape=out_shape,
        compiler_params=_params("parallel"),
        name="resnorm",
    )(x, y, g_post.reshape(1, d), gn.reshape(1, d))


def _panel_dots_kernel(*refs, kinds, n_extra, side, epilogue, ck, out_cw):
    n_dots = len(kinds)
    n_in = 2 * n_dots + n_extra + side
    out_ref = refs[n_in]
    scratch = list(refs[n_in + 1 + side:])
    bufs = [(scratch.pop(0), scratch.pop(0)) if kinds[p] != "small" else None for p in range(n_dots)]
    g = pl.program_id(0)
    i = pl.program_id(1)
    rows = pl.ds(pl.multiple_of(i * ck, ck), ck)

    def stage(parity):
        for p in range(n_dots):
            w_ref = refs[2 * p + 1]
            if kinds[p] == "rows":
                bufs[p][parity][rows, :] = w_ref[...].astype(BF16)
            elif kinds[p] == "cols":
                bufs[p][parity][rows, :] = w_ref[0].T.astype(BF16)
        if side:
            refs[n_in + 1][...] = refs[n_in - 1][...].astype(refs[n_in + 1].dtype)

    def compute(parity):
        accs = []
        for p in range(n_dots):
            w = refs[2 * p + 1][...].astype(BF16) if kinds[p] == "small" else bufs[p][parity][...]
            accs.append(jnp.dot(refs[2 * p][...].astype(BF16), w, preferred_element_type=F32))
        extras = [r[...] for r in refs[2 * n_dots:2 * n_dots + n_extra]]
        res = epilogue(accs, extras).astype(out_ref.dtype)
        if out_cw is None:
            out_ref[...] = res
        else:
            for c in range(res.shape[1] // out_cw):
                out_ref[c] = res[:, c * out_cw:(c + 1) * out_cw]

    @pl.when(g == 0)
    def _():
        stage(0)

    for parity in (0, 1):
        @pl.when((g > 0) & (g % 2 == parity))
        def _(parity=parity):
            stage(parity)
            compute(1 - parity)


def panel_dots(pairs, extras, epilogue, out_shape, out_dtype, tm, tn, name, side_cast=None,
               out_cw=None):
    m, n = out_shape
    n_i, n_j = m // tm, n // tn
    ck = None
    in_specs, args, scratch, kinds = [], [], [], []

    def jj(g):
        return jnp.maximum(g - 1, 0)

    def ii(g, i):
        return jnp.where(g == 0, 0, i)

    def pj(g):
        return jnp.minimum(g, n_j - 1)

    for a, w, kind, layer, where in pairs:
        k = a.shape[1]
        lead = () if layer is None else (layer,)
        lead_blk = () if layer is None else (None,)
        in_specs.append(pl.BlockSpec((tm, k), lambda g, i: (ii(g, i), 0)))
        if kind == "small":
            in_specs.append(pl.BlockSpec(lead_blk + (k, tn),
                                         lambda g, i, lead=lead, c=where: lead + (0, c + jj(g))))
        else:
            assert ck in (None, k // n_i) and k % n_i == 0
            ck = k // n_i
            scratch += [pltpu.VMEM((k, tn), BF16), pltpu.VMEM((k, tn), BF16)]
            if kind == "rows":
                in_specs.append(pl.BlockSpec(lead_blk + (ck, tn),
                                             lambda g, i, lead=lead, c=where: lead + (i, c + pj(g))))
            else:
                in_specs.append(pl.BlockSpec(
                    (pl.Element(1), pl.Element(tn), pl.Element(ck)),
                    lambda g, i, l=layer, f=where, ck=ck: (l, pl.multiple_of(f(pj(g)), 8), i * ck)))
        args += [a, w]
        kinds.append(kind)
    for arr, kind, cb0 in extras:
        if kind == "row":
            in_specs.append(pl.BlockSpec((1, tn), lambda g, i, cb0=cb0: (0, cb0 + jj(g))))
        else:
            in_specs.append(pl.BlockSpec((tm, tn), lambda g, i, cb0=cb0: (ii(g, i), cb0 + jj(g))))
        args.append(arr)
    if out_cw is None:
        out_specs = [pl.BlockSpec((tm, tn), lambda g, i: (ii(g, i), jj(g)))]
        out_shapes = [jax.ShapeDtypeStruct((m, n), out_dtype)]
    else:
        out_specs = [pl.BlockSpec((tn // out_cw, tm, out_cw), lambda g, i: (jj(g), ii(g, i), 0))]
        out_shapes = [jax.ShapeDtypeStruct((n // out_cw, m, out_cw), out_dtype)]
    if side_cast is not None:
        sw, s_layer, s_rows = side_cast
        _, s_k, s_n = sw.shape
        n_blk = s_k // s_rows
        assert s_k % s_rows == 0 and n_blk <= (n_j + 1) * n_i

        def blk(g, i):
            return jnp.minimum(g * n_i + i, n_blk - 1)

        in_specs.append(pl.BlockSpec((None, s_rows, s_n), lambda g, i: (s_layer, blk(g, i), 0)))
        args.append(sw)
        out_specs.append(pl.BlockSpec((s_rows, s_n), lambda g, i: (blk(g, i), 0)))
        out_shapes.append(jax.ShapeDtypeStruct((s_k, s_n), BF16))
    res = pl.pallas_call(
        functools.partial(_panel_dots_kernel, kinds=tuple(kinds), n_extra=len(extras),
                          side=int(side_cast is not None), epilogue=epilogue, ck=ck, out_cw=out_cw),
        grid=(n_j + 1, n_i),
        in_specs=in_specs,
        out_specs=out_specs,
        out_shape=out_shapes,
        scratch_shapes=scratch,
        compiler_params=_params("arbitrary", "arbitrary"),
        name=name,
    )(*args)
    return res[0] if side_cast is None else res


def _ep_identity(accs, extras):
    return accs[0]


def _ep_bias_sigmoid(accs, extras):
    return _sigmoid(accs[0] + extras[0])


def _ep_relu2(accs, extras):
    return jnp.square(jnp.maximum(accs[0], 0.0))


def _ep_gate_mul(accs, extras):
    return extras[0].astype(F32) * accs[0]


def _ep_gate_mul_add(accs, extras):
    return extras[0].astype(F32) * accs[0] + extras[1].astype(F32)


def _ep_sigmoid_mul(accs, extras):
    return _sigmoid(accs[0]) * accs[1]


def _matmul_ktiled_kernel(a_ref, w_ref, o_ref, acc_ref):
    kk = pl.program_id(2)

    @pl.when(kk == 0)
    def _():
        acc_ref[...] = jnp.zeros_like(acc_ref)

    acc_ref[...] += jnp.dot(a_ref[...], w_ref[...], preferred_element_type=F32)

    @pl.when(kk == pl.num_programs(2) - 1)
    def _():
        o_ref[...] = acc_ref[...].astype(o_ref.dtype)


def matmul_ktiled(a, w, out_dtype, tm, tn, tk, name):
    m, k = a.shape
    n = w.shape[1]
    return pl.pallas_call(
        _matmul_ktiled_kernel,
        grid=(m // tm, n // tn, k // tk),
        in_specs=[pl.BlockSpec((tm, tk), lambda i, j, kk: (i, kk)),
                  pl.BlockSpec((tk, tn), lambda i, j, kk: (kk, j))],
        out_specs=pl.BlockSpec((tm, tn), lambda i, j, kk: (i, j)),
        out_shape=jax.ShapeDtypeStruct((m, n), out_dtype),
        scratch_shapes=[pltpu.VMEM((tm, tn), F32)],
        compiler_params=_params("parallel", "parallel", "arbitrary"),
        name=name,
    )(a, w)


def _split3(x):
    hi = x.astype(BF16)
    r1 = x - hi.astype(F32)
    mid = r1.astype(BF16)
    lo = (r1 - mid.astype(F32)).astype(BF16)
    return hi, mid, lo


def _gate_prep_kernel(gi_ref, gf_ref, bi_ref, bf_ref, b_ref, e_ref, r_ref, *, chunk):
    s, w = gi_ref.shape
    nh = w // 2
    log_i = GATE_CAP * jnp.tanh((gi_ref[...] + bi_ref[...]) * (1.0 / GATE_CAP))
    af = GATE_CAP * jnp.tanh((gf_ref[...] + bf_ref[...]) * (1.0 / GATE_CAP))
    log_f = jnp.minimum(af, 0.0) - jnp.log1p(jnp.exp(-jnp.abs(af)))
    ri = lax.broadcasted_iota(jnp.int32, (chunk, chunk), 0)
    ci = lax.broadcasted_iota(jnp.int32, (chunk, chunk), 1)
    tril = (ci <= ri).astype(BF16)
    triu = (ci >= ri).astype(BF16)
    is_fwd = lax.broadcasted_iota(jnp.int32, (chunk, w), 1) < nh
    for c in range(s // chunk):
        rows = slice(c * chunk, (c + 1) * chunk)
        parts = _split3(log_f[rows, :])
        cum_f = sum(jnp.dot(tril, p, preferred_element_type=F32) for p in parts)
        cum_b = sum(jnp.dot(triu, p, preferred_element_type=F32) for p in parts)
        b = jnp.where(is_fwd, cum_f, cum_b)
        tot = jnp.where(is_fwd[:1], cum_f[chunk - 1:chunk, :], cum_b[0:1, :])
        li = log_i[rows, :]
        b_ref[rows, :] = b
        e_ref[rows, :] = tot - b + li
        r_ref[rows, :] = li - b


def gate_prep(gi, gf, bi, bf, chunk):
    bsz, s, w = gi.shape
    blk = pl.BlockSpec((None, s, w), lambda b: (b, 0, 0))
    vec = pl.BlockSpec((1, w), lambda b: (0, 0))
    out = jax.ShapeDtypeStruct((bsz, s, w), F32)
    return pl.pallas_call(
        functools.partial(_gate_prep_kernel, chunk=chunk),
        grid=(bsz,),
        in_specs=[blk, blk, vec, vec],
        out_specs=[blk, blk, blk],
        out_shape=[out, out, out],
        compiler_params=_params("parallel"),
        name="gate_prep",
    )(gi, gf, bi, bf)


def _mlstm_kernel(q_ref, k_ref, v_ref, o_ref, gcol_ref, grow_ref, gh_ref, out_ref,
                  h_ref, c_ref, n_ref, *, chunk, q_scale):
    s = q_ref.shape[0]
    nc = s // chunk
    assert nc % 2 == 0
    ri = lax.broadcasted_iota(jnp.int32, (chunk, chunk), 0)
    ci = lax.broadcasted_iota(jnp.int32, (chunk, chunk), 1)
    masks = (ci <= ri, ci >= ri)
    c_ref[...] = jnp.zeros_like(c_ref)
    n_ref[...] = jnp.zeros_like(n_ref)

    def rows_of(c):
        return pl.ds(pl.multiple_of(c * chunk, chunk), chunk)

    def scan_pair(cf, cb):
        dirs = (0, 1)
        rows = [rows_of(c) for c in (cf, cb)]
        q = [q_ref[r, :] * q_scale for r in rows]
        k = [k_ref[r, :] for r in rows]
        v = [jnp.concatenate([v_ref[c, r, :] for c in range(v_ref.shape[0])], axis=1) for r in rows]
        gc = [gcol_ref[r, :] for r in rows]
        bcol = [gc[d][:, 2 * d:2 * d + 1] for d in dirs]
        ecol = [gc[d][:, 2 * d + 1:2 * d + 2] for d in dirs]
        rrow = [grow_ref[0, cf], grow_ref[1, cb]]
        last = (chunk - 1, 0)
        decay = [jnp.exp(bcol[d][last[d]:last[d] + 1, :]) for d in dirs]
        c_prev = [c_ref[d] for d in dirs]
        n_prev = [n_ref[d] for d in dirs]

        qk = [lax.dot_general(q[d], k[d], (((1,), (1,)), ((), ())), preferred_element_type=F32)
              for d in dirs]
        w_intra = [jnp.exp(jnp.where(masks[d], bcol[d] + rrow[d], -jnp.inf)) for d in dirs]
        w_inter = [jnp.exp(bcol[d]) for d in dirs]
        sc = [qk[d] * w_intra[d] for d in dirs]
        qf = [q[d].astype(F32) for d in dirs]
        lhs = [jnp.concatenate([sc[d].astype(BF16), (qf[d] * w_inter[d]).astype(BF16)], axis=1) for d in dirs]
        rhs = [jnp.concatenate([v[d], c_prev[d].astype(BF16)], axis=0) for d in dirs]
        num = [jnp.dot(lhs[d], rhs[d], preferred_element_type=F32) for d in dirs]
        qn = [jnp.sum(qf[d] * n_prev[d], axis=1, keepdims=True) for d in dirs]
        den = [jnp.sum(sc[d], axis=1, keepdims=True) + w_inter[d] * qn[d] for d in dirs]
        h = [num[d] * (1.0 / jnp.maximum(jnp.abs(den[d]), 1.0)) for d in dirs]

        kw = [k[d].astype(F32) * jnp.exp(ecol[d]) for d in dirs]
        upd = [lax.dot_general(kw[d].astype(BF16), v[d], (((0,), (0,)), ((), ())),
                               preferred_element_type=F32) for d in dirs]
        for d in dirs:
            c_ref[d] = decay[d] * c_prev[d] + upd[d]
            n_ref[d] = decay[d] * n_prev[d] + jnp.sum(kw[d], axis=0, keepdims=True)
        return h

    def finish(c, tot):
        rows = rows_of(c)
        o = jnp.concatenate([o_ref[c, rows, :] for c in range(o_ref.shape[0])], axis=1)
        y = _rms(tot, gh_ref[...]) * _sigmoid(o.astype(F32))
        out_ref[rows, :] = y.astype(out_ref.dtype)

    def first_half(t, carry):
        hf, hb = scan_pair(t, nc - 1 - t)
        h_ref[rows_of(t), :] = hf
        h_ref[rows_of(nc - 1 - t), :] = hb
        return carry

    def second_half(t, carry):
        cf, cb = t, nc - 1 - t
        hf, hb = scan_pair(cf, cb)
        finish(cf, hf + h_ref[rows_of(cf), :])
        finish(cb, hb + h_ref[rows_of(cb), :])
        return carry

    lax.fori_loop(0, nc // 2, first_half, 0)
    lax.fori_loop(nc // 2, nc, second_half, 0)


def mlstm(u4, gcol, grow, g_head, *, dk, dv, q_cb0, k_cb0, v_cb0, o_cb0, chunk):
    _, bsz, s, _ = u4.shape
    nh = gcol.shape[1]
    nc = s // chunk
    wide = dv // dk

    def narrow(cb0):
        return pl.BlockSpec((None, None, s, dk), lambda b, h, cb0=cb0: (cb0 + h, b, 0, 0))

    def wide_blk(cb0):
        return pl.BlockSpec((wide, None, s, dk), lambda b, h, cb0=cb0: (cb0 // wide + h, b, 0, 0))

    return pl.pallas_call(
        functools.partial(_mlstm_kernel, chunk=chunk, q_scale=dk ** -0.5),
        grid=(bsz, nh),
        in_specs=[narrow(q_cb0), narrow(k_cb0), wide_blk(v_cb0), wide_blk(o_cb0),
                  pl.BlockSpec((None, None, s, 4), lambda b, h: (b, h, 0, 0)),
                  pl.BlockSpec((None, None, 2, nc, 1, chunk), lambda b, h: (b, h, 0, 0, 0, 0)),
                  pl.BlockSpec((1, dv), lambda b, h: (0, h))],
        out_specs=pl.BlockSpec((None, s, dv), lambda b, h: (b, 0, h)),
        out_shape=jax.ShapeDtypeStruct((bsz, s, nh * dv), BF16),
        scratch_shapes=[pltpu.VMEM((s, dv), F32), pltpu.VMEM((2, dk, dv), F32), pltpu.VMEM((2, 1, dk), F32)],
        compiler_params=_params("parallel", "parallel"),
        name="mlstm",
    )(u4, u4, u4, u4, gcol, grow, g_head.reshape(1, nh * dv))


def _cos_sin(rows, n_cols, period):
    k = jnp.arange(n_cols, dtype=jnp.int32)
    ang = ((rows[:, None] * k[None, :]) % period).astype(F32) * (2.0 * math.pi / period)
    return jnp.cos(ang), jnp.sin(ang)


def _dft_mats(n, scale):
    n1 = 1 << ((n.bit_length() - 1) // 2)
    n2 = n // n1
    ca, sa = _cos_sin(jnp.arange(n1, dtype=jnp.int32), n, n1)
    cb, sb = _cos_sin(jnp.arange(n2, dtype=jnp.int32), n, n)
    ca, sa, cb, sb = ca[:, None, :], sa[:, None, :], cb[None, :, :], sb[None, :, :]
    c = (ca * cb - sa * sb) * scale
    s = (sa * cb + ca * sb) * scale
    return c.reshape(n, n).astype(BF16), s.reshape(n, n).astype(BF16)


FLIP_ROWS = 256
SUBLANES = 8


def _flip_matrix(r):
    t = jnp.arange(r, dtype=jnp.int32)
    return ((t[:, None] + t[None, :]) == r).astype(BF16)


def _shifted_flip(j_ref, blk_ref, nxt_ref, use_next):
    rev = jnp.dot(j_ref[...], blk_ref[...], preferred_element_type=F32)
    first = lax.broadcasted_iota(jnp.int32, rev.shape, 0) == 0
    row0 = jnp.where(use_next, nxt_ref[0:1, :].astype(F32), 0.0)
    return jnp.where(first, row0, rev)


def _dft_chan_kernel(j_ref, zlo_ref, za_ref, zb_ref, cc_ref, sc_ref, p_ref, q_ref, *, n_fold):
    i = pl.program_id(2)
    a = zlo_ref[...]
    r = _shifted_flip(j_ref, za_ref, zb_ref, i > 0)
    r = jnp.where(i < n_fold, r, 0.0).astype(BF16)
    p_ref[...] = jnp.dot(a + r, cc_ref[...], preferred_element_type=F32).astype(p_ref.dtype)
    q_ref[...] = jnp.dot(a - r, sc_ref[...], preferred_element_type=F32).astype(q_ref.dtype)


def dft_chan(u3, z_cb0, cc, sc, hp, r):
    bsz, n, _ = u3.shape
    gd = cc.shape[0]
    nb = n // r
    out = jax.ShapeDtypeStruct((bsz, hp, F_GROUPS * gd), BF16)
    dst = pl.BlockSpec((None, r, gd), lambda b, g, i: (b, i, g))
    mat = pl.BlockSpec((gd, gd), lambda b, g, i: (0, 0))
    return pl.pallas_call(
        functools.partial(_dft_chan_kernel, n_fold=nb // 2),
        grid=(bsz, F_GROUPS, hp // r),
        in_specs=[pl.BlockSpec((r, r), lambda b, g, i: (0, 0)),
                  pl.BlockSpec((None, r, gd), lambda b, g, i: (b, i, z_cb0 + g)),
                  pl.BlockSpec((None, r, gd), lambda b, g, i: (b, nb - 1 - i, z_cb0 + g)),
                  pl.BlockSpec((None, SUBLANES, gd),
                               lambda b, g, i: (b, jnp.minimum(nb - i, nb - 1) * (r // SUBLANES), z_cb0 + g)),
                  mat, mat],
        out_specs=[dst, dst],
        out_shape=[out, out],
        compiler_params=_params("parallel", "parallel", "arbitrary"),
        name="dft_chan",
    )(_flip_matrix(r), u3, u3, u3, cc, sc)


def _dft_unfold_kernel(j_ref, lo_ref, hi_ref, nxt_ref, o_ref, *, n_fold):
    t = pl.program_id(1)

    @pl.when(t < n_fold)
    def _():
        o_ref[...] = lo_ref[...]

    @pl.when(t >= n_fold)
    def _():
        o_ref[...] = _shifted_flip(j_ref, hi_ref, nxt_ref, True).astype(o_ref.dtype)


def dft_unfold(lo, hi, n, r):
    bsz, _, d = lo.shape
    nb = n // r
    n_fold = nb // 2
    return pl.pallas_call(
        functools.partial(_dft_unfold_kernel, n_fold=n_fold),
        grid=(bsz, nb),
        in_specs=[pl.BlockSpec((r, r), lambda b, t: (0, 0)),
                  pl.BlockSpec((None, r, d), lambda b, t: (b, jnp.minimum(t, n_fold - 1), 0)),
                  pl.BlockSpec((None, r, d), lambda b, t: (b, jnp.minimum(nb - 1 - t, n_fold - 1), 0)),
                  pl.BlockSpec((None, SUBLANES, d),
                               lambda b, t: (b, jnp.minimum(nb - t, n_fold) * (r // SUBLANES), 0))],
        out_specs=pl.BlockSpec((None, r, d), lambda b, t: (b, t, 0)),
        out_shape=jax.ShapeDtypeStruct((bsz, n, d), BF16),
        compiler_params=_params("parallel", "arbitrary"),
        name="dft_unfold",
    )(_flip_matrix(r), lo, hi, hi)


def _dft_seq_kernel(cs_ref, ss_ref, p_ref, q_ref, lo_ref, hi_ref):
    a = jnp.dot(cs_ref[...], p_ref[...], preferred_element_type=F32)
    b = jnp.dot(ss_ref[...], q_ref[...], preferred_element_type=F32)
    lo_ref[...] = (a - b).astype(lo_ref.dtype)
    hi_ref[...] = (a + b).astype(hi_ref.dtype)


def dft_seq(cs, ss, p, q, gd, tm):
    bsz, hp, w = p.shape
    out = jax.ShapeDtypeStruct((bsz, hp, w), BF16)
    mat = pl.BlockSpec((tm, hp), lambda b, g, i: (i, 0))
    src = pl.BlockSpec((None, hp, gd), lambda b, g, i: (b, 0, g))
    dst = pl.BlockSpec((None, tm, gd), lambda b, g, i: (b, i, g))
    return pl.pallas_call(
        _dft_seq_kernel,
        grid=(bsz, w // gd, hp // tm),
        in_specs=[mat, mat, src, src],
        out_specs=[dst, dst],
        out_shape=[out, out],
        compiler_params=_params("parallel", "parallel", "arbitrary"),
        name="dft_seq",
    )(cs, ss, p, q)


def fourier_mix(u3, z_cb0, gd):
    bsz, n, _ = u3.shape
    h = n // 2
    r = min(FLIP_ROWS, h)
    hp = h + r
    cc, sc = _dft_mats(gd, gd ** -0.5)
    cs, ss = _dft_mats(n, n ** -0.5)
    pad = ((0, hp - h - 1), (0, hp - h - 1))
    cs = jnp.pad(cs[:h + 1, :h + 1], pad)
    ss = jnp.pad(ss[:h + 1, :h + 1], pad)
    p, q = dft_chan(u3, z_cb0, cc, sc, hp, r)
    tm = hp // 3 if hp % (3 * SUBLANES) == 0 else hp
    lo, hi = dft_seq(cs, ss, p, q, gd, tm)
    return dft_unfold(lo, hi, n, r)


def _tile(n, pref):
    return pref if n % pref == 0 else n


def _layer(x, xn, p_i, lw, g_next, *, bsz, seq):
    m, d = x.shape
    nh = M_HEADS
    dv = d // nh
    dk = dv // 2
    qk_w, v_w = nh * dk, nh * dv
    gd = d // F_GROUPS
    n_main = 2 * qk_w + 2 * v_w
    g0 = n_main
    z0 = n_main + 4 * nh
    chunk = min(MLSTM_CHUNK, seq)
    tm = _tile(m, 1024)

    li = lw["layer"]
    tn = _tile(d, 1024)
    w_in_t = lw["w_in_t"]
    u = panel_dots([(xn, w_in_t, "cols", li, lambda j: j * tn)], [], _ep_identity, (m, n_main), BF16,
                   tm, tn, "in_proj", out_cw=dk)
    z = panel_dots([(xn, w_in_t, "cols", li, lambda j: z0 + j * tn)], [], _ep_identity, (m, d), BF16,
                   tm, tn, "in_proj_z")
    graw = panel_dots([(xn, w_in_t, "cols", li, lambda j: g0)], [], _ep_identity, (m, 128), F32,
                      tm, 128, "gate_proj")

    graw = graw[:, :4 * nh].reshape(bsz, seq, 2, 2, nh)
    gi = graw[:, :, :, 0].reshape(bsz, seq, 2 * nh)
    gf = graw[:, :, :, 1].reshape(bsz, seq, 2 * nh)
    b_arr, e_arr, r_arr = gate_prep(gi, gf, lw["b_igate"][li].reshape(1, 2 * nh),
                                    lw["b_fgate"][li].reshape(1, 2 * nh), chunk)
    b4 = b_arr.reshape(bsz, seq, 2, nh)
    e4 = e_arr.reshape(bsz, seq, 2, nh)
    gcol = jnp.stack([b4[:, :, 0], e4[:, :, 0], b4[:, :, 1], e4[:, :, 1]], axis=-1)
    gcol = gcol.transpose(0, 2, 1, 3)
    grow = r_arr.reshape(bsz, seq // chunk, chunk, 2, nh).transpose(0, 4, 3, 1, 2)
    grow = grow[:, :, :, :, None, :]
    u4 = u.reshape(n_main // dk, bsz, seq, dk)
    hg = mlstm(u4, gcol, grow, lw["g_head"][li], dk=dk, dv=dv, q_cb0=0, k_cb0=qk_w // dk,
               v_cb0=2 * qk_w // dk, o_cb0=(2 * qk_w + v_w) // dk, chunk=chunk)
    hg = hg.reshape(m, v_w)

    zf = fourier_mix(z.reshape(bsz, seq, d), 0, gd).reshape(m, d)

    gates = panel_dots([(xn, lw["w_merge"], "rows", li, 0)],
                       [(lw["b_merge"][li].reshape(1, 2 * d), "row", 0)],
                       _ep_bias_sigmoid, (m, 2 * d), BF16, tm, tn, "merge_gate")
    ym = panel_dots([(hg, lw["w_branch_m"], "rows", li, 0)], [(gates, "tile", 0)],
                    _ep_gate_mul, (m, d), BF16, tm, tn, "branch_m")
    mix = panel_dots([(zf, lw["w_branch_f"], "rows", li, 0)], [(gates, "tile", d // tn), (ym, "tile", 0)],
                     _ep_gate_mul_add, (m, d), BF16, tm, tn, "branch_f")
    y = panel_dots([(mix, lw["w_out"], "rows", li, 0)], [], _ep_identity, (m, d), BF16, tm, tn, "out_proj")
    x, hn = resnorm(x, y, lw["g_mix_post"][li], lw["g_mlp_pre"][li], BF16)

    d_ff = lw["w_up"].shape[-1]
    n_steps = (d_ff // tn + 1) * (m // tm)
    cast_rows_per_step = next(r for r in (128, 256, 512, 1024, d_ff) if d_ff % r == 0 and d_ff // r <= n_steps)
    up, w_down = panel_dots([(hn, lw["w_up"], "rows", li, 0)], [], _ep_relu2, (m, d_ff), BF16, tm, tn,
                            "mlp_up", side_cast=(lw["w_down"], li, cast_rows_per_step))
    y = matmul_ktiled(up, w_down, BF16, tm, _tile(d, 1024), _tile(d_ff, 4096), "mlp_down")
    x, = resnorm(x, y, lw["g_mlp_post"][li], None, BF16)

    ple = panel_dots([(x, lw["w_ple_gate"], "rows", li, 0), (p_i, lw["w_ple_in"], "small", li, 0)],
                     [], _ep_sigmoid_mul, (m, d), BF16, tm, tn, "ple")
    if g_next is None:
        x, = resnorm(x, ple, lw["g_ple_post"][li], None, F32)
        return x, None
    return resnorm(x, ple, lw["g_ple_post"][li], g_next, BF16)


def kernel(x, p, g_mix_pre, g_mix_post, g_mlp_pre, g_mlp_post, g_ple_post, w_in, b_igate, b_fgate,
           g_head, w_branch_m, w_branch_f, w_merge, b_merge, w_out, w_up, w_down, w_ple_in, w_ple_gate):
    bsz, seq, d = x.shape
    depth = w_in.shape[0]
    m = bsz * seq
    xf = x.reshape(m, d)
    xn = rmsnorm(xf, g_mix_pre[0])
    lw = dict(w_in_t=jnp.swapaxes(w_in, 1, 2), b_igate=b_igate, b_fgate=b_fgate, g_head=g_head, w_branch_m=w_branch_m,
              w_branch_f=w_branch_f, w_merge=w_merge, b_merge=b_merge, w_out=w_out, w_up=w_up,
              w_down=w_down, w_ple_in=w_ple_in, w_ple_gate=w_ple_gate, g_mix_post=g_mix_post,
              g_mlp_pre=g_mlp_pre, g_mlp_post=g_mlp_post, g_ple_post=g_ple_post)
    for i in range(depth):
        g_next = g_mix_pre[i + 1] if i + 1 < depth else None
        xf, xn = _layer(xf, xn, p[i].reshape(m, -1), dict(lw, layer=i), g_next, bsz=bsz, seq=seq)
    return xf.reshape(bsz, seq, d)
```

```python
import functools
import math

import jax
import jax.numpy as jnp
from jax import lax
from jax.experimental import pallas as pl
from jax.experimental.pallas import tpu as pltpu

F32 = jnp.float32
BF16 = jnp.bfloat16

M_HEADS = 8
F_GROUPS = 4
GATE_CAP = 15.0
EPS = 1e-6
MLSTM_CHUNK = 256
V7X_VMEM_BYTES = 64 * 1024 * 1024
VMEM_LIMIT = V7X_VMEM_BYTES - 4 * 1024 * 1024
LANES = 128
MM_ROWS = 1024
MM_PANEL = 1024
MM_K_TILE = 4096
NORM_ROWS = 256


def _params(*semantics):
    return pltpu.CompilerParams(dimension_semantics=semantics, vmem_limit_bytes=VMEM_LIMIT)


def _rms(x, g):
    return x * lax.rsqrt(jnp.mean(x * x, axis=-1, keepdims=True) + EPS) * g


def _sigmoid(x):
    return 0.5 * jnp.tanh(0.5 * x) + 0.5


def _rmsnorm_kernel(x_ref, g_ref, o_ref):
    o_ref[...] = _rms(x_ref[...], g_ref[...]).astype(o_ref.dtype)


def rmsnorm(x, g, tm=NORM_ROWS):
    m, d = x.shape
    return pl.pallas_call(
        _rmsnorm_kernel,
        grid=(m // tm,),
        in_specs=[pl.BlockSpec((tm, d), lambda i: (i, 0)),
                  pl.BlockSpec((1, d), lambda i: (0, 0))],
        out_specs=pl.BlockSpec((tm, d), lambda i: (i, 0)),
        out_shape=jax.ShapeDtypeStruct((m, d), BF16),
        compiler_params=_params("parallel"),
        name="rmsnorm",
    )(x, g.reshape(1, d))


def _resnorm_kernel(x_ref, y_ref, gp_ref, gn_ref, xo_ref, *xn_ref):
    xnew = x_ref[...].astype(F32) + _rms(y_ref[...].astype(F32), gp_ref[...])
    xo_ref[...] = xnew.astype(xo_ref.dtype)
    if xn_ref:
        xn_ref[0][...] = _rms(xnew, gn_ref[...]).astype(xn_ref[0].dtype)


def resnorm(x, y, g_post, g_next, x_dtype, tm=NORM_ROWS):
    m, d = x.shape
    gn = g_post if g_next is None else g_next
    row = pl.BlockSpec((tm, d), lambda i: (i, 0))
    vec = pl.BlockSpec((1, d), lambda i: (0, 0))
    out_shape = [jax.ShapeDtypeStruct((m, d), x_dtype)]
    if g_next is not None:
        out_shape.append(jax.ShapeDtypeStruct((m, d), BF16))
    return pl.pallas_call(
        _resnorm_kernel,
        grid=(m // tm,),
        in_specs=[row, row, vec, vec],
        out_specs=[row] * len(out_shape),
        out_shape=out_shape,
        compiler_params=_params("parallel"),
        name="resnorm",
    )(x, y, g_post.reshape(1, d), gn.reshape(1, d))


def _panel_dots_kernel(*refs, kinds, n_extra, side, epilogue, ck):
    n_dots = len(kinds)
    n_in = 2 * n_dots + n_extra + side
    out_ref = refs[n_in]
    scratch = list(refs[n_in + 1 + side:])
    bufs = [(scratch.pop(0), scratch.pop(0)) if kinds[p] != "small" else None for p in range(n_dots)]
    g = pl.program_id(0)
    i = pl.program_id(1)
    rows = pl.ds(pl.multiple_of(i * ck, ck), ck)

    def stage(parity):
        for p in range(n_dots):
            w_ref = refs[2 * p + 1]
            if kinds[p] == "rows":
                bufs[p][parity][rows, :] = w_ref[...].astype(BF16)
            elif kinds[p] == "cols":
                bufs[p][parity][rows, :] = w_ref[0].T.astype(BF16)
        if side:
            refs[n_in + 1][...] = refs[n_in - 1][...].astype(refs[n_in + 1].dtype)

    def compute(parity):
        accs = []
        for p in range(n_dots):
            w = refs[2 * p + 1][...].astype(BF16) if kinds[p] == "small" else bufs[p][parity][...]
            accs.append(jnp.dot(refs[2 * p][...].astype(BF16), w, preferred_element_type=F32))
        extras = [r[...] for r in refs[2 * n_dots:2 * n_dots + n_extra]]
        out_ref[...] = epilogue(accs, extras).astype(out_ref.dtype)

    @pl.when(g == 0)
    def _():
        stage(0)

    for parity in (0, 1):
        @pl.when((g > 0) & (g % 2 == parity))
        def _(parity=parity):
            stage(parity)
            compute(1 - parity)


def panel_dots(pairs, extras, epilogue, out_shape, out_dtype, tm, tn, name, side_cast=None):
    m, n = out_shape
    n_i, n_j = m // tm, n // tn
    ck = None
    in_specs, args, scratch, kinds = [], [], [], []

    def jj(g):
        return jnp.maximum(g - 1, 0)

    def ii(g, i):
        return jnp.where(g == 0, 0, i)

    def pj(g):
        return jnp.minimum(g, n_j - 1)

    for a, w, kind, layer, where in pairs:
        k = a.shape[1]
        lead = () if layer is None else (layer,)
        lead_blk = () if layer is None else (None,)
        in_specs.append(pl.BlockSpec((tm, k), lambda g, i: (ii(g, i), 0)))
        if kind == "small":
            in_specs.append(pl.BlockSpec(lead_blk + (k, tn),
                                         lambda g, i, lead=lead, c=where: lead + (0, c + jj(g))))
        else:
            assert ck in (None, k // n_i) and k % n_i == 0
            ck = k // n_i
            scratch += [pltpu.VMEM((k, tn), BF16), pltpu.VMEM((k, tn), BF16)]
            if kind == "rows":
                in_specs.append(pl.BlockSpec(lead_blk + (ck, tn),
                                             lambda g, i, lead=lead, c=where: lead + (i, c + pj(g))))
            else:
                in_specs.append(pl.BlockSpec(
                    (pl.Element(1), pl.Element(tn), pl.Element(ck)),
                    lambda g, i, l=layer, f=where, ck=ck: (l, pl.multiple_of(f(pj(g)), 8), i * ck)))
        args += [a, w]
        kinds.append(kind)
    for arr, kind, cb0 in extras:
        if kind == "row":
            in_specs.append(pl.BlockSpec((1, tn), lambda g, i, cb0=cb0: (0, cb0 + jj(g))))
        else:
            in_specs.append(pl.BlockSpec((tm, tn), lambda g, i, cb0=cb0: (ii(g, i), cb0 + jj(g))))
        args.append(arr)
    out_specs = [pl.BlockSpec((tm, tn), lambda g, i: (ii(g, i), jj(g)))]
    out_shapes = [jax.ShapeDtypeStruct((m, n), out_dtype)]
    if side_cast is not None:
        sw, s_layer, s_rows = side_cast
        _, s_k, s_n = sw.shape
        n_blk = s_k // s_rows
        assert s_k % s_rows == 0 and n_blk <= (n_j + 1) * n_i

        def blk(g, i):
            return jnp.minimum(g * n_i + i, n_blk - 1)

        in_specs.append(pl.BlockSpec((None, s_rows, s_n), lambda g, i: (s_layer, blk(g, i), 0)))
        args.append(sw)
        out_specs.append(pl.BlockSpec((s_rows, s_n), lambda g, i: (blk(g, i), 0)))
        out_shapes.append(jax.ShapeDtypeStruct((s_k, s_n), BF16))
    res = pl.pallas_call(
        functools.partial(_panel_dots_kernel, kinds=tuple(kinds), n_extra=len(extras),
                          side=int(side_cast is not None), epilogue=epilogue, ck=ck),
        grid=(n_j + 1, n_i),
        in_specs=in_specs,
        out_specs=out_specs,
        out_shape=out_shapes,
        scratch_shapes=scratch,
        compiler_params=_params("arbitrary", "arbitrary"),
        name=name,
    )(*args)
    return res[0] if side_cast is None else res


def _ep_identity(accs, extras):
    return accs[0]


def _ep_bias_sigmoid(accs, extras):
    return _sigmoid(accs[0] + extras[0])


def _ep_relu2(accs, extras):
    return jnp.square(jnp.maximum(accs[0], 0.0))


def _ep_gate_mul(accs, extras):
    return extras[0].astype(F32) * accs[0]


def _ep_gate_mul_add(accs, extras):
    return extras[0].astype(F32) * accs[0] + extras[1].astype(F32)


def _ep_sigmoid_mul(accs, extras):
    return _sigmoid(accs[0]) * accs[1]


def _matmul_ktiled_kernel(a_ref, w_ref, o_ref, acc_ref):
    kk = pl.program_id(2)

    @pl.when(kk == 0)
    def _():
        acc_ref[...] = jnp.zeros_like(acc_ref)

    acc_ref[...] += jnp.dot(a_ref[...], w_ref[...], preferred_element_type=F32)

    @pl.when(kk == pl.num_programs(2) - 1)
    def _():
        o_ref[...] = acc_ref[...].astype(o_ref.dtype)


def matmul_ktiled(a, w, out_dtype, tm, tn, tk, name):
    m, k = a.shape
    n = w.shape[1]
    return pl.pallas_call(
        _matmul_ktiled_kernel,
        grid=(m // tm, n // tn, k // tk),
        in_specs=[pl.BlockSpec((tm, tk), lambda i, j, kk: (i, kk)),
                  pl.BlockSpec((tk, tn), lambda i, j, kk: (kk, j))],
        out_specs=pl.BlockSpec((tm, tn), lambda i, j, kk: (i, j)),
        out_shape=jax.ShapeDtypeStruct((m, n), out_dtype),
        scratch_shapes=[pltpu.VMEM((tm, tn), F32)],
        compiler_params=_params("parallel", "parallel", "arbitrary"),
        name=name,
    )(a, w)


def _split3(x):
    hi = x.astype(BF16)
    r1 = x - hi.astype(F32)
    mid = r1.astype(BF16)
    lo = (r1 - mid.astype(F32)).astype(BF16)
    return hi, mid, lo


def _gate_prep_kernel(gi_ref, gf_ref, bi_ref, bf_ref, b_ref, e_ref, r_ref, *, chunk):
    s, w = gi_ref.shape
    nh = w // 2
    log_i = GATE_CAP * jnp.tanh((gi_ref[...] + bi_ref[...]) * (1.0 / GATE_CAP))
    af = GATE_CAP * jnp.tanh((gf_ref[...] + bf_ref[...]) * (1.0 / GATE_CAP))
    log_f = jnp.minimum(af, 0.0) - jnp.log1p(jnp.exp(-jnp.abs(af)))
    ri = lax.broadcasted_iota(jnp.int32, (chunk, chunk), 0)
    ci = lax.broadcasted_iota(jnp.int32, (chunk, chunk), 1)
    tril = (ci <= ri).astype(BF16)
    triu = (ci >= ri).astype(BF16)
    is_fwd = lax.broadcasted_iota(jnp.int32, (chunk, w), 1) < nh
    for c in range(s // chunk):
        rows = slice(c * chunk, (c + 1) * chunk)
        parts = _split3(log_f[rows, :])
        cum_f = sum(jnp.dot(tril, p, preferred_element_type=F32) for p in parts)
        cum_b = sum(jnp.dot(triu, p, preferred_element_type=F32) for p in parts)
        b = jnp.where(is_fwd, cum_f, cum_b)
        tot = jnp.where(is_fwd[:1], cum_f[chunk - 1:chunk, :], cum_b[0:1, :])
        li = log_i[rows, :]
        b_ref[rows, :] = b
        e_ref[rows, :] = tot - b + li
        r_ref[rows, :] = li - b


def gate_prep(gi, gf, bi, bf, chunk):
    bsz, s, w = gi.shape
    blk = pl.BlockSpec((None, s, w), lambda b: (b, 0, 0))
    vec = pl.BlockSpec((1, w), lambda b: (0, 0))
    out = jax.ShapeDtypeStruct((bsz, s, w), F32)
    return pl.pallas_call(
        functools.partial(_gate_prep_kernel, chunk=chunk),
        grid=(bsz,),
        in_specs=[blk, blk, vec, vec],
        out_specs=[blk, blk, blk],
        out_shape=[out, out, out],
        compiler_params=_params("parallel"),
        name="gate_prep",
    )(gi, gf, bi, bf)


def _mlstm_kernel(q_ref, k_ref, v_ref, o_ref, gcol_ref, grow_ref, gh_ref, out_ref,
                  h_ref, c_ref, n_ref, *, chunk, q_scale):
    s = q_ref.shape[0]
    nc = s // chunk
    assert nc % 2 == 0
    ri = lax.broadcasted_iota(jnp.int32, (chunk, chunk), 0)
    ci = lax.broadcasted_iota(jnp.int32, (chunk, chunk), 1)
    masks = (ci <= ri, ci >= ri)
    c_ref[...] = jnp.zeros_like(c_ref)
    n_ref[...] = jnp.zeros_like(n_ref)

    def rows_of(c):
        return pl.ds(pl.multiple_of(c * chunk, chunk), chunk)

    def scan_pair(cf, cb):
        dirs = (0, 1)
        rows = [rows_of(c) for c in (cf, cb)]
        q = [q_ref[r, :] * q_scale for r in rows]
        k = [k_ref[r, :] for r in rows]
        v = [v_ref[r, :] for r in rows]
        gc = [gcol_ref[r, :] for r in rows]
        bcol = [gc[d][:, 2 * d:2 * d + 1] for d in dirs]
        ecol = [gc[d][:, 2 * d + 1:2 * d + 2] for d in dirs]
        rrow = [grow_ref[0, cf], grow_ref[1, cb]]
        last = (chunk - 1, 0)
        decay = [jnp.exp(bcol[d][last[d]:last[d] + 1, :]) for d in dirs]
        c_prev = [c_ref[d] for d in dirs]
        n_prev = [n_ref[d] for d in dirs]

        qk = [lax.dot_general(q[d], k[d], (((1,), (1,)), ((), ())), preferred_element_type=F32)
              for d in dirs]
        w_intra = [jnp.exp(jnp.where(masks[d], bcol[d] + rrow[d], -jnp.inf)) for d in dirs]
        w_inter = [jnp.exp(bcol[d]) for d in dirs]
        sc = [qk[d] * w_intra[d] for d in dirs]
        qf = [q[d].astype(F32) for d in dirs]
        lhs = [jnp.concatenate([sc[d].astype(BF16), (qf[d] * w_inter[d]).astype(BF16)], axis=1) for d in dirs]
        rhs = [jnp.concatenate([v[d], c_prev[d].astype(BF16)], axis=0) for d in dirs]
        num = [jnp.dot(lhs[d], rhs[d], preferred_element_type=F32) for d in dirs]
        qn = [jnp.sum(qf[d] * n_prev[d], axis=1, keepdims=True) for d in dirs]
        den = [jnp.sum(sc[d], axis=1, keepdims=True) + w_inter[d] * qn[d] for d in dirs]
        h = [num[d] * (1.0 / jnp.maximum(jnp.abs(den[d]), 1.0)) for d in dirs]

        kw = [k[d].astype(F32) * jnp.exp(ecol[d]) for d in dirs]
        upd = [lax.dot_general(kw[d].astype(BF16), v[d], (((0,), (0,)), ((), ())),
                               preferred_element_type=F32) for d in dirs]
        for d in dirs:
            c_ref[d] = decay[d] * c_prev[d] + upd[d]
            n_ref[d] = decay[d] * n_prev[d] + jnp.sum(kw[d], axis=0, keepdims=True)
        return h

    def finish(c, tot):
        rows = rows_of(c)
        y = _rms(tot, gh_ref[...]) * _sigmoid(o_ref[rows, :].astype(F32))
        out_ref[rows, :] = y.astype(out_ref.dtype)

    def first_half(t, carry):
        hf, hb = scan_pair(t, nc - 1 - t)
        h_ref[rows_of(t), :] = hf
        h_ref[rows_of(nc - 1 - t), :] = hb
        return carry

    def second_half(t, carry):
        cf, cb = t, nc - 1 - t
        hf, hb = scan_pair(cf, cb)
        finish(cf, hf + h_ref[rows_of(cf), :])
        finish(cb, hb + h_ref[rows_of(cb), :])
        return carry

    lax.fori_loop(0, nc // 2, first_half, 0)
    lax.fori_loop(nc // 2, nc, second_half, 0)


def mlstm(u3, gcol, grow, g_head, *, dk, dv, q_cb0, k_cb0, v_cb0, o_cb0, chunk):
    bsz, s, _ = u3.shape
    nh = gcol.shape[1]
    nc = s // chunk

    def col(width, cb0):
        return pl.BlockSpec((None, s, width), lambda b, h, cb0=cb0: (b, 0, cb0 + h))

    return pl.pallas_call(
        functools.partial(_mlstm_kernel, chunk=chunk, q_scale=dk ** -0.5),
        grid=(bsz, nh),
        in_specs=[col(dk, q_cb0), col(dk, k_cb0), col(dv, v_cb0), col(dv, o_cb0),
                  pl.BlockSpec((None, None, s, 4), lambda b, h: (b, h, 0, 0)),
                  pl.BlockSpec((None, None, 2, nc, 1, chunk), lambda b, h: (b, h, 0, 0, 0, 0)),
                  pl.BlockSpec((1, dv), lambda b, h: (0, h))],
        out_specs=pl.BlockSpec((None, s, dv), lambda b, h: (b, 0, h)),
        out_shape=jax.ShapeDtypeStruct((bsz, s, nh * dv), BF16),
        scratch_shapes=[pltpu.VMEM((s, dv), F32), pltpu.VMEM((2, dk, dv), F32), pltpu.VMEM((2, 1, dk), F32)],
        compiler_params=_params("parallel", "parallel"),
        name="mlstm",
    )(u3, u3, u3, u3, gcol, grow, g_head.reshape(1, nh * dv))


def _cos_sin(rows, n_cols, period):
    k = jnp.arange(n_cols, dtype=jnp.int32)
    ang = ((rows[:, None] * k[None, :]) % period).astype(F32) * (2.0 * math.pi / period)
    return jnp.cos(ang), jnp.sin(ang)


def _dft_mats(n, scale):
    n1 = 1 << ((n.bit_length() - 1) // 2)
    n2 = n // n1
    ca, sa = _cos_sin(jnp.arange(n1, dtype=jnp.int32), n, n1)
    cb, sb = _cos_sin(jnp.arange(n2, dtype=jnp.int32), n, n)
    ca, sa, cb, sb = ca[:, None, :], sa[:, None, :], cb[None, :, :], sb[None, :, :]
    c = (ca * cb - sa * sb) * scale
    s = (sa * cb + ca * sb) * scale
    return c.reshape(n, n).astype(BF16), s.reshape(n, n).astype(BF16)


FLIP_ROWS = 256
SUBLANES = 8
BF16_SUBLANES = 16
DFT_SEQ_TILES = 3


def _flip_matrix(r):
    t = jnp.arange(r, dtype=jnp.int32)
    return ((t[:, None] + t[None, :]) == r).astype(BF16)


def _shifted_flip(j_ref, blk_ref, nxt_ref, use_next):
    rev = jnp.dot(j_ref[...], blk_ref[...], preferred_element_type=F32)
    first = lax.broadcasted_iota(jnp.int32, rev.shape, 0) == 0
    row0 = jnp.where(use_next, nxt_ref[0:1, :].astype(F32), 0.0)
    return jnp.where(first, row0, rev)


def _dft_chan_kernel(j_ref, zlo_ref, za_ref, zb_ref, cc_ref, sc_ref, p_ref, q_ref, *, n_fold):
    i = pl.program_id(2)
    a = zlo_ref[...]
    r = _shifted_flip(j_ref, za_ref, zb_ref, i > 0)
    r = jnp.where(i < n_fold, r, 0.0).astype(BF16)
    p_ref[...] = jnp.dot(a + r, cc_ref[...], preferred_element_type=F32).astype(p_ref.dtype)
    q_ref[...] = jnp.dot(a - r, sc_ref[...], preferred_element_type=F32).astype(q_ref.dtype)


def dft_chan(u3, z_cb0, cc, sc, hp, r):
    bsz, n, _ = u3.shape
    gd = cc.shape[0]
    nb = n // r
    out = jax.ShapeDtypeStruct((bsz, hp, F_GROUPS * gd), BF16)
    dst = pl.BlockSpec((None, r, gd), lambda b, g, i: (b, i, g))
    mat = pl.BlockSpec((gd, gd), lambda b, g, i: (0, 0))
    return pl.pallas_call(
        functools.partial(_dft_chan_kernel, n_fold=nb // 2),
        grid=(bsz, F_GROUPS, hp // r),
        in_specs=[pl.BlockSpec((r, r), lambda b, g, i: (0, 0)),
                  pl.BlockSpec((None, r, gd), lambda b, g, i: (b, i, z_cb0 + g)),
                  pl.BlockSpec((None, r, gd), lambda b, g, i: (b, nb - 1 - i, z_cb0 + g)),
                  pl.BlockSpec((None, SUBLANES, gd),
                               lambda b, g, i: (b, jnp.minimum(nb - i, nb - 1) * (r // SUBLANES), z_cb0 + g)),
                  mat, mat],
        out_specs=[dst, dst],
        out_shape=[out, out],
        compiler_params=_params("parallel", "parallel", "arbitrary"),
        name="dft_chan",
    )(_flip_matrix(r), u3, u3, u3, cc, sc)


def _dft_unfold_kernel(j_ref, lo_ref, hi_ref, nxt_ref, o_ref, *, n_fold):
    t = pl.program_id(1)

    @pl.when(t < n_fold)
    def _():
        o_ref[...] = lo_ref[...]

    @pl.when(t >= n_fold)
    def _():
        o_ref[...] = _shifted_flip(j_ref, hi_ref, nxt_ref, True).astype(o_ref.dtype)


def dft_unfold(lo, hi, n, r):
    bsz, _, d = lo.shape
    nb = n // r
    n_fold = nb // 2
    return pl.pallas_call(
        functools.partial(_dft_unfold_kernel, n_fold=n_fold),
        grid=(bsz, nb),
        in_specs=[pl.BlockSpec((r, r), lambda b, t: (0, 0)),
                  pl.BlockSpec((None, r, d), lambda b, t: (b, jnp.minimum(t, n_fold - 1), 0)),
                  pl.BlockSpec((None, r, d), lambda b, t: (b, jnp.minimum(nb - 1 - t, n_fold - 1), 0)),
                  pl.BlockSpec((None, SUBLANES, d),
                               lambda b, t: (b, jnp.minimum(nb - t, n_fold) * (r // SUBLANES), 0))],
        out_specs=pl.BlockSpec((None, r, d), lambda b, t: (b, t, 0)),
        out_shape=jax.ShapeDtypeStruct((bsz, n, d), BF16),
        compiler_params=_params("parallel", "arbitrary"),
        name="dft_unfold",
    )(_flip_matrix(r), lo, hi, hi)


def _dft_seq_kernel(cs_ref, ss_ref, p_ref, q_ref, lo_ref, hi_ref):
    a = jnp.dot(cs_ref[...], p_ref[...], preferred_element_type=F32)
    b = jnp.dot(ss_ref[...], q_ref[...], preferred_element_type=F32)
    lo_ref[...] = (a - b).astype(lo_ref.dtype)
    hi_ref[...] = (a + b).astype(hi_ref.dtype)


def dft_seq(cs, ss, p, q, gd, tm):
    bsz, hp, w = p.shape
    rows = cs.shape[0]
    out = jax.ShapeDtypeStruct((bsz, rows, w), BF16)
    mat = pl.BlockSpec((tm, hp), lambda b, g, i: (i, 0))
    src = pl.BlockSpec((None, hp, gd), lambda b, g, i: (b, 0, g))
    dst = pl.BlockSpec((None, tm, gd), lambda b, g, i: (b, i, g))
    return pl.pallas_call(
        _dft_seq_kernel,
        grid=(bsz, w // gd, rows // tm),
        in_specs=[mat, mat, src, src],
        out_specs=[dst, dst],
        out_shape=[out, out],
        compiler_params=_params("parallel", "parallel", "arbitrary"),
        name="dft_seq",
    )(cs, ss, p, q)


def fourier_mix(u3, z_cb0, gd):
    bsz, n, _ = u3.shape
    h = n // 2
    r = min(FLIP_ROWS, h)
    hp = h + r
    cc, sc = _dft_mats(gd, gd ** -0.5)
    cs, ss = _dft_mats(n, n ** -0.5)
    tm = -(-(h + SUBLANES) // (DFT_SEQ_TILES * BF16_SUBLANES)) * BF16_SUBLANES
    pad = ((0, DFT_SEQ_TILES * tm - h - 1), (0, hp - h - 1))
    cs = jnp.pad(cs[:h + 1, :h + 1], pad)
    ss = jnp.pad(ss[:h + 1, :h + 1], pad)
    p, q = dft_chan(u3, z_cb0, cc, sc, hp, r)
    lo, hi = dft_seq(cs, ss, p, q, gd, tm)
    return dft_unfold(lo, hi, n, r)


def _tile(n, pref):
    return pref if n % pref == 0 else n


def _layer(x, xn, p_i, lw, g_next, *, bsz, seq):
    m, d = x.shape
    nh = M_HEADS
    dv = d // nh
    dk = dv // 2
    qk_w, v_w = nh * dk, nh * dv
    gd = d // F_GROUPS
    n_main = 2 * qk_w + 2 * v_w
    g0 = n_main
    z0 = n_main + 4 * nh
    chunk = min(MLSTM_CHUNK, seq)
    tm = _tile(m, MM_ROWS)

    li = lw["layer"]
    tn = _tile(d, MM_PANEL)
    w_in_t = lw["w_in_t"]
    u = panel_dots([(xn, w_in_t, "cols", li, lambda j: j * tn + jnp.where(j * tn >= g0, z0 - g0, 0))],
                   [], _ep_identity, (m, n_main + d), BF16, tm, tn, "in_proj")
    graw = panel_dots([(xn, w_in_t, "cols", li, lambda j: g0)], [], _ep_identity, (m, LANES), F32,
                      tm, LANES, "gate_proj")

    graw = graw[:, :4 * nh].reshape(bsz, seq, 2, 2, nh)
    gi = graw[:, :, :, 0].reshape(bsz, seq, 2 * nh)
    gf = graw[:, :, :, 1].reshape(bsz, seq, 2 * nh)
    b_arr, e_arr, r_arr = gate_prep(gi, gf, lw["b_igate"][li].reshape(1, 2 * nh),
                                    lw["b_fgate"][li].reshape(1, 2 * nh), chunk)
    b4 = b_arr.reshape(bsz, seq, 2, nh)
    e4 = e_arr.reshape(bsz, seq, 2, nh)
    gcol = jnp.stack([b4[:, :, 0], e4[:, :, 0], b4[:, :, 1], e4[:, :, 1]], axis=-1)
    gcol = gcol.transpose(0, 2, 1, 3)
    grow = r_arr.reshape(bsz, seq // chunk, chunk, 2, nh).transpose(0, 4, 3, 1, 2)
    grow = grow[:, :, :, :, None, :]
    u3 = u.reshape(bsz, seq, n_main + d)
    hg = mlstm(u3, gcol, grow, lw["g_head"][li], dk=dk, dv=dv, q_cb0=0, k_cb0=qk_w // dk,
               v_cb0=2 * qk_w // dv, o_cb0=(2 * qk_w + v_w) // dv, chunk=chunk)
    hg = hg.reshape(m, v_w)

    zf = fourier_mix(u3, n_main // gd, gd).reshape(m, d)

    gates = panel_dots([(xn, lw["w_merge"], "rows", li, 0)],
                       [(lw["b_merge"][li].reshape(1, 2 * d), "row", 0)],
                       _ep_bias_sigmoid, (m, 2 * d), BF16, tm, tn, "merge_gate")
    ym = panel_dots([(hg, lw["w_branch_m"], "rows", li, 0)], [(gates, "tile", 0)],
                    _ep_gate_mul, (m, d), BF16, tm, tn, "branch_m")
    mix = panel_dots([(zf, lw["w_branch_f"], "rows", li, 0)], [(gates, "tile", d // tn), (ym, "tile", 0)],
                     _ep_gate_mul_add, (m, d), BF16, tm, tn, "branch_f")
    y = panel_dots([(mix, lw["w_out"], "rows", li, 0)], [], _ep_identity, (m, d), BF16, tm, tn, "out_proj")
    x, hn = resnorm(x, y, lw["g_mix_post"][li], lw["g_mlp_pre"][li], BF16)

    d_ff = lw["w_up"].shape[-1]
    n_steps = (d_ff // tn + 1) * (m // tm)
    cast_rows_per_step = next(r for r in (LANES, 2 * LANES, 4 * LANES, 8 * LANES, d_ff)
                              if d_ff % r == 0 and d_ff // r <= n_steps)
    up, w_down = panel_dots([(hn, lw["w_up"], "rows", li, 0)], [], _ep_relu2, (m, d_ff), BF16, tm, tn,
                            "mlp_up", side_cast=(lw["w_down"], li, cast_rows_per_step))
    y = matmul_ktiled(up, w_down, BF16, tm, tn, _tile(d_ff, MM_K_TILE), "mlp_down")
    x, = resnorm(x, y, lw["g_mlp_post"][li], None, BF16)

    ple = panel_dots([(x, lw["w_ple_gate"], "rows", li, 0), (p_i, lw["w_ple_in"], "small", li, 0)],
                     [], _ep_sigmoid_mul, (m, d), BF16, tm, tn, "ple")
    if g_next is None:
        x, = resnorm(x, ple, lw["g_ple_post"][li], None, F32)
        return x, None
    return resnorm(x, ple, lw["g_ple_post"][li], g_next, BF16)


def kernel(x, p, g_mix_pre, g_mix_post, g_mlp_pre, g_mlp_post, g_ple_post, w_in, b_igate, b_fgate,
           g_head, w_branch_m, w_branch_f, w_merge, b_merge, w_out, w_up, w_down, w_ple_in, w_ple_gate):
    bsz, seq, d = x.shape
    depth = w_in.shape[0]
    m = bsz * seq
    xf = x.reshape(m, d)
    xn = rmsnorm(xf, g_mix_pre[0])
    lw = dict(w_in_t=jnp.swapaxes(w_in, 1, 2), b_igate=b_igate, b_fgate=b_fgate, g_head=g_head, w_branch_m=w_branch_m,
              w_branch_f=w_branch_f, w_merge=w_merge, b_merge=b_merge, w_out=w_out, w_up=w_up,
              w_down=w_down, w_ple_in=w_ple_in, w_ple_gate=w_ple_gate, g_mix_post=g_mix_post,
              g_mlp_pre=g_mlp_pre, g_mlp_post=g_mlp_post, g_ple_post=g_ple_post)
    for i in range(depth):
        g_next = g_mix_pre[i + 1] if i + 1 < depth else None
        xf, xn = _layer(xf, xn, p[i].reshape(m, -1), dict(lw, layer=i), g_next, bsz=bsz, seq=seq)
    return xf.reshape(bsz, seq, d)
```

```python
import functools
import math

import jax
import jax.numpy as jnp
from jax import lax
from jax.experimental import pallas as pl
from jax.experimental.pallas import tpu as pltpu

F32 = jnp.float32
BF16 = jnp.bfloat16

M_HEADS = 8
F_GROUPS = 4
GATE_CAP = 15.0
EPS = 1e-6
MLSTM_CHUNK = 256
V7X_VMEM_BYTES = 64 * 1024 * 1024
VMEM_LIMIT = V7X_VMEM_BYTES - 4 * 1024 * 1024
LANES = 128
MM_ROWS = 1024
MM_PANEL = 1024
MM_K_TILE = 4096
NORM_ROWS = 256


def _params(*semantics):
    return pltpu.CompilerParams(dimension_semantics=semantics, vmem_limit_bytes=VMEM_LIMIT)


def _rms(x, g):
    return x * lax.rsqrt(jnp.mean(x * x, axis=-1, keepdims=True) + EPS) * g


def _sigmoid(x):
    return 0.5 * jnp.tanh(0.5 * x) + 0.5


def _rmsnorm_kernel(x_ref, g_ref, o_ref):
    o_ref[...] = _rms(x_ref[...], g_ref[...]).astype(o_ref.dtype)


def rmsnorm(x, g, tm=NORM_ROWS):
    m, d = x.shape
    return pl.pallas_call(
        _rmsnorm_kernel,
        grid=(m // tm,),
        in_specs=[pl.BlockSpec((tm, d), lambda i: (i, 0)),
                  pl.BlockSpec((1, d), lambda i: (0, 0))],
        out_specs=pl.BlockSpec((tm, d), lambda i: (i, 0)),
        out_shape=jax.ShapeDtypeStruct((m, d), BF16),
        compiler_params=_params("parallel"),
        name="rmsnorm",
    )(x, g.reshape(1, d))


def _resnorm_kernel(x_ref, y_ref, gp_ref, gn_ref, xo_ref, *xn_ref):
    xnew = x_ref[...].astype(F32) + _rms(y_ref[...].astype(F32), gp_ref[...])
    xo_ref[...] = xnew.astype(xo_ref.dtype)
    if xn_ref:
        xn_ref[0][...] = _rms(xnew, gn_ref[...]).astype(xn_ref[0].dtype)


def resnorm(x, y, g_post, g_next, x_dtype, tm=NORM_ROWS):
    m, d = x.shape
    gn = g_post if g_next is None else g_next
    row = pl.BlockSpec((tm, d), lambda i: (i, 0))
    vec = pl.BlockSpec((1, d), lambda i: (0, 0))
    out_shape = [jax.ShapeDtypeStruct((m, d), x_dtype)]
    if g_next is not None:
        out_shape.append(jax.ShapeDtypeStruct((m, d), BF16))
    return pl.pallas_call(
        _resnorm_kernel,
        grid=(m // tm,),
        in_specs=[row, row, vec, vec],
        out_specs=[row] * len(out_shape),
        out_shape=out_shape,
        compiler_params=_params("parallel"),
        name="resnorm",
    )(x, y, g_post.reshape(1, d), gn.reshape(1, d))


def _panel_dots_kernel(*refs, kinds, n_extra, side, epilogue, ck):
    n_dots = len(kinds)
    n_in = 2 * n_dots + n_extra + side
    out_ref = refs[n_in]
    scratch = list(refs[n_in + 1 + side:])
    bufs = [(scratch.pop(0), scratch.pop(0)) if kinds[p] != "small" else None for p in range(n_dots)]
    g = pl.program_id(0)
    i = pl.program_id(1)
    rows = pl.ds(pl.multiple_of(i * ck, ck), ck)

    def stage(parity):
        for p in range(n_dots):
            w_ref = refs[2 * p + 1]
            if kinds[p] == "rows":
                bufs[p][parity][rows, :] = w_ref[...].astype(BF16)
            elif kinds[p] == "cols":
                bufs[p][parity][rows, :] = w_ref[0].T.astype(BF16)
        if side:
            refs[n_in + 1][...] = refs[n_in - 1][...].astype(refs[n_in + 1].dtype)

    def compute(parity):
        accs = []
        for p in range(n_dots):
            w = refs[2 * p + 1][...].astype(BF16) if kinds[p] == "small" else bufs[p][parity][...]
            accs.append(jnp.dot(refs[2 * p][...].astype(BF16), w, preferred_element_type=F32))
        extras = [r[...] for r in refs[2 * n_dots:2 * n_dots + n_extra]]
        out_ref[...] = epilogue(accs, extras).astype(out_ref.dtype)

    @pl.when(g == 0)
    def _():
        stage(0)

    for parity in (0, 1):
        @pl.when((g > 0) & (g % 2 == parity))
        def _(parity=parity):
            stage(parity)
            compute(1 - parity)


def panel_dots(pairs, extras, epilogue, out_shape, out_dtype, tm, tn, name, side_cast=None):
    m, n = out_shape
    n_i, n_j = m // tm, n // tn
    ck = None
    in_specs, args, scratch, kinds = [], [], [], []

    def jj(g):
        return jnp.maximum(g - 1, 0)

    def ii(g, i):
        return jnp.where(g == 0, 0, i)

    def pj(g):
        return jnp.minimum(g, n_j - 1)

    for a, w, kind, layer, where in pairs:
        k = a.shape[1]
        lead = () if layer is None else (layer,)
        lead_blk = () if layer is None else (None,)
        in_specs.append(pl.BlockSpec((tm, k), lambda g, i: (ii(g, i), 0)))
        if kind == "small":
            in_specs.append(pl.BlockSpec(lead_blk + (k, tn),
                                         lambda g, i, lead=lead, c=where: lead + (0, c + jj(g))))
        else:
            assert ck in (None, k // n_i) and k % n_i == 0
            ck = k // n_i
            scratch += [pltpu.VMEM((k, tn), BF16), pltpu.VMEM((k, tn), BF16)]
            if kind == "rows":
                in_specs.append(pl.BlockSpec(lead_blk + (ck, tn),
                                             lambda g, i, lead=lead, c=where: lead + (i, c + pj(g))))
            else:
                in_specs.append(pl.BlockSpec(
                    (pl.Element(1), pl.Element(tn), pl.Element(ck)),
                    lambda g, i, l=layer, f=where, ck=ck: (l, pl.multiple_of(f(pj(g)), 8), i * ck)))
        args += [a, w]
        kinds.append(kind)
    for arr, kind, cb0 in extras:
        if kind == "row":
            in_specs.append(pl.BlockSpec((1, tn), lambda g, i, cb0=cb0: (0, cb0 + jj(g))))
        else:
            in_specs.append(pl.BlockSpec((tm, tn), lambda g, i, cb0=cb0: (ii(g, i), cb0 + jj(g))))
        args.append(arr)
    out_specs = [pl.BlockSpec((tm, tn), lambda g, i: (ii(g, i), jj(g)))]
    out_shapes = [jax.ShapeDtypeStruct((m, n), out_dtype)]
    if side_cast is not None:
        sw, s_layer, s_rows = side_cast
        _, s_k, s_n = sw.shape
        n_blk = s_k // s_rows
        assert s_k % s_rows == 0 and n_blk <= (n_j + 1) * n_i

        def blk(g, i):
            return jnp.minimum(g * n_i + i, n_blk - 1)

        in_specs.append(pl.BlockSpec((None, s_rows, s_n), lambda g, i: (s_layer, blk(g, i), 0)))
        args.append(sw)
        out_specs.append(pl.BlockSpec((s_rows, s_n), lambda g, i: (blk(g, i), 0)))
        out_shapes.append(jax.ShapeDtypeStruct((s_k, s_n), BF16))
    res = pl.pallas_call(
        functools.partial(_panel_dots_kernel, kinds=tuple(kinds), n_extra=len(extras),
                          side=int(side_cast is not None), epilogue=epilogue, ck=ck),
        grid=(n_j + 1, n_i),
        in_specs=in_specs,
        out_specs=out_specs,
        out_shape=out_shapes,
        scratch_shapes=scratch,
        compiler_params=_params("arbitrary", "arbitrary"),
        name=name,
    )(*args)
    return res[0] if side_cast is None else res


def _ep_identity(accs, extras):
    return accs[0]


def _ep_bias_sigmoid(accs, extras):
    return _sigmoid(accs[0] + extras[0])


def _ep_relu2(accs, extras):
    return jnp.square(jnp.maximum(accs[0], 0.0))


def _ep_gate_mul(accs, extras):
    return extras[0].astype(F32) * accs[0]


def _ep_gate_mul_add(accs, extras):
    return extras[0].astype(F32) * accs[0] + extras[1].astype(F32)


def _ep_sigmoid_mul(accs, extras):
    return _sigmoid(accs[0]) * accs[1]


def _matmul_ktiled_kernel(a_ref, w_ref, o_ref, acc_ref):
    kk = pl.program_id(2)

    @pl.when(kk == 0)
    def _():
        acc_ref[...] = jnp.zeros_like(acc_ref)

    acc_ref[...] += jnp.dot(a_ref[...], w_ref[...], preferred_element_type=F32)

    @pl.when(kk == pl.num_programs(2) - 1)
    def _():
        o_ref[...] = acc_ref[...].astype(o_ref.dtype)


def matmul_ktiled(a, w, out_dtype, tm, tn, tk, name):
    m, k = a.shape
    n = w.shape[1]
    return pl.pallas_call(
        _matmul_ktiled_kernel,
        grid=(m // tm, n // tn, k // tk),
        in_specs=[pl.BlockSpec((tm, tk), lambda i, j, kk: (i, kk)),
                  pl.BlockSpec((tk, tn), lambda i, j, kk: (kk, j))],
        out_specs=pl.BlockSpec((tm, tn), lambda i, j, kk: (i, j)),
        out_shape=jax.ShapeDtypeStruct((m, n), out_dtype),
        scratch_shapes=[pltpu.VMEM((tm, tn), F32)],
        compiler_params=_params("parallel", "parallel", "arbitrary"),
        name=name,
    )(a, w)


def _split3(x):
    hi = x.astype(BF16)
    r1 = x - hi.astype(F32)
    mid = r1.astype(BF16)
    lo = (r1 - mid.astype(F32)).astype(BF16)
    return hi, mid, lo


def _gate_prep_kernel(gi_ref, gf_ref, bi_ref, bf_ref, b_ref, e_ref, r_ref, *, chunk):
    s, w = gi_ref.shape
    nh = w // 2
    log_i = GATE_CAP * jnp.tanh((gi_ref[...] + bi_ref[...]) * (1.0 / GATE_CAP))
    af = GATE_CAP * jnp.tanh((gf_ref[...] + bf_ref[...]) * (1.0 / GATE_CAP))
    log_f = jnp.minimum(af, 0.0) - jnp.log1p(jnp.exp(-jnp.abs(af)))
    ri = lax.broadcasted_iota(jnp.int32, (chunk, chunk), 0)
    ci = lax.broadcasted_iota(jnp.int32, (chunk, chunk), 1)
    tril = (ci <= ri).astype(BF16)
    triu = (ci >= ri).astype(BF16)
    is_fwd = lax.broadcasted_iota(jnp.int32, (chunk, w), 1) < nh
    for c in range(s // chunk):
        rows = slice(c * chunk, (c + 1) * chunk)
        parts = _split3(log_f[rows, :])
        cum_f = sum(jnp.dot(tril, p, preferred_element_type=F32) for p in parts)
        cum_b = sum(jnp.dot(triu, p, preferred_element_type=F32) for p in parts)
        b = jnp.where(is_fwd, cum_f, cum_b)
        tot = jnp.where(is_fwd[:1], cum_f[chunk - 1:chunk, :], cum_b[0:1, :])
        li = log_i[rows, :]
        b_ref[rows, :] = b
        e_ref[rows, :] = tot - b + li
        r_ref[rows, :] = li - b


def gate_prep(gi, gf, bi, bf, chunk):
    bsz, s, w = gi.shape
    blk = pl.BlockSpec((None, s, w), lambda b: (b, 0, 0))
    vec = pl.BlockSpec((1, w), lambda b: (0, 0))
    out = jax.ShapeDtypeStruct((bsz, s, w), F32)
    return pl.pallas_call(
        functools.partial(_gate_prep_kernel, chunk=chunk),
        grid=(bsz,),
        in_specs=[blk, blk, vec, vec],
        out_specs=[blk, blk, blk],
        out_shape=[out, out, out],
        compiler_params=_params("parallel"),
        name="gate_prep",
    )(gi, gf, bi, bf)


def _mlstm_kernel(q_ref, k_ref, v_ref, o_ref, gcol_ref, grow_ref, gh_ref, out_ref,
                  h_ref, c_ref, n_ref, *, chunk, q_scale):
    s = q_ref.shape[0]
    nc = s // chunk
    assert nc % 2 == 0
    ri = lax.broadcasted_iota(jnp.int32, (chunk, chunk), 0)
    ci = lax.broadcasted_iota(jnp.int32, (chunk, chunk), 1)
    masks = (ci <= ri, ci >= ri)
    c_ref[...] = jnp.zeros_like(c_ref)
    n_ref[...] = jnp.zeros_like(n_ref)

    def rows_of(c):
        return pl.ds(pl.multiple_of(c * chunk, chunk), chunk)

    def scan_pair(cf, cb):
        dirs = (0, 1)
        rows = [rows_of(c) for c in (cf, cb)]
        q = [q_ref[r, :] * q_scale for r in rows]
        k = [k_ref[r, :] for r in rows]
        v = [v_ref[r, :] for r in rows]
        gc = [gcol_ref[r, :] for r in rows]
        bcol = [gc[d][:, 2 * d:2 * d + 1] for d in dirs]
        ecol = [gc[d][:, 2 * d + 1:2 * d + 2] for d in dirs]
        rrow = [grow_ref[0, cf], grow_ref[1, cb]]
        last = (chunk - 1, 0)
        decay = [jnp.exp(bcol[d][last[d]:last[d] + 1, :]) for d in dirs]
        c_prev = [c_ref[d] for d in dirs]
        n_prev = [n_ref[d] for d in dirs]

        qk = [lax.dot_general(q[d], k[d], (((1,), (1,)), ((), ())), preferred_element_type=F32)
              for d in dirs]
        w_intra = [jnp.exp(jnp.where(masks[d], bcol[d] + rrow[d], -jnp.inf)) for d in dirs]
        w_inter = [jnp.exp(bcol[d]) for d in dirs]
        sc = [qk[d] * w_intra[d] for d in dirs]
        qf = [q[d].astype(F32) for d in dirs]
        lhs = [jnp.concatenate([sc[d].astype(BF16), (qf[d] * w_inter[d]).astype(BF16)], axis=1) for d in dirs]
        rhs = [jnp.concatenate([v[d], c_prev[d].astype(BF16)], axis=0) for d in dirs]
        num = [jnp.dot(lhs[d], rhs[d], preferred_element_type=F32) for d in dirs]
        qn = [jnp.sum(qf[d] * n_prev[d], axis=1, keepdims=True) for d in dirs]
        den = [jnp.sum(sc[d], axis=1, keepdims=True) + w_inter[d] * qn[d] for d in dirs]
        h = [num[d] * (1.0 / jnp.maximum(jnp.abs(den[d]), 1.0)) for d in dirs]

        kw = [k[d].astype(F32) * jnp.exp(ecol[d]) for d in dirs]
        upd = [lax.dot_general(kw[d].astype(BF16), v[d], (((0,), (0,)), ((), ())),
                               preferred_element_type=F32) for d in dirs]
        for d in dirs:
            c_ref[d] = decay[d] * c_prev[d] + upd[d]
            n_ref[d] = decay[d] * n_prev[d] + jnp.sum(kw[d], axis=0, keepdims=True)
        return h

    def finish(c, tot):
        rows = rows_of(c)
        y = _rms(tot, gh_ref[...]) * _sigmoid(o_ref[rows, :].astype(F32))
        out_ref[rows, :] = y.astype(out_ref.dtype)

    def first_half(t, carry):
        hf, hb = scan_pair(t, nc - 1 - t)
        h_ref[rows_of(t), :] = hf
        h_ref[rows_of(nc - 1 - t), :] = hb
        return carry

    def second_half(t, carry):
        cf, cb = t, nc - 1 - t
        hf, hb = scan_pair(cf, cb)
        finish(cf, hf + h_ref[rows_of(cf), :])
        finish(cb, hb + h_ref[rows_of(cb), :])
        return carry

    lax.fori_loop(0, nc // 2, first_half, 0)
    lax.fori_loop(nc // 2, nc, second_half, 0)


def mlstm(u3, gcol, grow, g_head, *, dk, dv, q_cb0, k_cb0, v_cb0, o_cb0, chunk):
    bsz, s, _ = u3.shape
    nh = gcol.shape[1]
    nc = s // chunk

    def col(width, cb0):
        return pl.BlockSpec((None, s, width), lambda b, h, cb0=cb0: (b, 0, cb0 + h))

    return pl.pallas_call(
        functools.partial(_mlstm_kernel, chunk=chunk, q_scale=dk ** -0.5),
        grid=(bsz, nh),
        in_specs=[col(dk, q_cb0), col(dk, k_cb0), col(dv, v_cb0), col(dv, o_cb0),
                  pl.BlockSpec((None, None, s, 4), lambda b, h: (b, h, 0, 0)),
                  pl.BlockSpec((None, None, 2, nc, 1, chunk), lambda b, h: (b, h, 0, 0, 0, 0)),
                  pl.BlockSpec((1, dv), lambda b, h: (0, h))],
        out_specs=pl.BlockSpec((None, s, dv), lambda b, h: (b, 0, h)),
        out_shape=jax.ShapeDtypeStruct((bsz, s, nh * dv), BF16),
        scratch_shapes=[pltpu.VMEM((s, dv), F32), pltpu.VMEM((2, dk, dv), F32), pltpu.VMEM((2, 1, dk), F32)],
        compiler_params=_params("parallel", "parallel"),
        name="mlstm",
    )(u3, u3, u3, u3, gcol, grow, g_head.reshape(1, nh * dv))


def _cos_sin(rows, n_cols, period):
    k = jnp.arange(n_cols, dtype=jnp.int32)
    ang = ((rows[:, None] * k[None, :]) % period).astype(F32) * (2.0 * math.pi / period)
    return jnp.cos(ang), jnp.sin(ang)


def _dft_mats(n, scale):
    n1 = 1 << ((n.bit_length() - 1) // 2)
    n2 = n // n1
    ca, sa = _cos_sin(jnp.arange(n1, dtype=jnp.int32), n, n1)
    cb, sb = _cos_sin(jnp.arange(n2, dtype=jnp.int32), n, n)
    ca, sa, cb, sb = ca[:, None, :], sa[:, None, :], cb[None, :, :], sb[None, :, :]
    c = (ca * cb - sa * sb) * scale
    s = (sa * cb + ca * sb) * scale
    return c.reshape(n, n).astype(BF16), s.reshape(n, n).astype(BF16)


FLIP_ROWS = 256
SUBLANES = 8
BF16_SUBLANES = 16
DFT_SEQ_TILES = 3


def _flip_matrix(r):
    t = jnp.arange(r, dtype=jnp.int32)
    return ((t[:, None] + t[None, :]) == r).astype(BF16)


def _shifted_flip(j, blk, row0):
    rev = jnp.dot(j, blk, preferred_element_type=F32)
    first = lax.broadcasted_iota(jnp.int32, rev.shape, 0) == 0
    return jnp.where(first, row0, rev)


def _fold_cols(x, jc, sign):
    h2 = jc.shape[0]
    rev = jnp.dot(x[:, h2:], jc, preferred_element_type=F32).astype(x.dtype)
    tail = x[:, h2:h2 + LANES]
    if sign > 0:
        lane0 = lax.broadcasted_iota(jnp.int32, tail.shape, 1) == 0
        return jnp.concatenate([x[:, :h2] + rev, jnp.where(lane0, tail, jnp.zeros_like(tail))], axis=1)
    return jnp.concatenate([x[:, :h2] - rev, jnp.zeros_like(tail)], axis=1)


def _unfold_cols(left, mirror, jc):
    h2 = jc.shape[0]
    right = jnp.dot(mirror[:, :h2], jc, preferred_element_type=F32).astype(left.dtype)
    head = right[:, :LANES]
    lane0 = lax.broadcasted_iota(jnp.int32, head.shape, 1) == 0
    head = jnp.where(lane0, left[:, h2:h2 + LANES], head)
    return jnp.concatenate([left[:, :h2], head, right[:, LANES:]], axis=1)


def _dft_chan_kernel(j_ref, jc_ref, zlo_ref, za_ref, zb_ref, cc_ref, sc_ref, p_ref, q_ref, *, n_fold):
    i = pl.program_id(2)
    a = zlo_ref[...]
    row0 = jnp.where(i > 0, zb_ref[0:1, :].astype(F32), 0.0)
    r = _shifted_flip(j_ref[...], za_ref[...], row0)
    r = jnp.where(i < n_fold, r, 0.0).astype(BF16)
    jc = jc_ref[...]
    p_ref[...] = jnp.dot(_fold_cols(a + r, jc, 1), cc_ref[...], preferred_element_type=F32).astype(p_ref.dtype)
    q_ref[...] = jnp.dot(_fold_cols(a - r, jc, -1), sc_ref[...], preferred_element_type=F32).astype(q_ref.dtype)


def dft_chan(u3, z_cb0, gd, cc, sc, jc, hp, r):
    bsz, n, _ = u3.shape
    w2 = cc.shape[0]
    nb = n // r
    out = jax.ShapeDtypeStruct((bsz, hp, F_GROUPS * w2), BF16)
    dst = pl.BlockSpec((None, r, w2), lambda b, g, i: (b, i, g))
    mat = pl.BlockSpec((w2, w2), lambda b, g, i: (0, 0))
    return pl.pallas_call(
        functools.partial(_dft_chan_kernel, n_fold=nb // 2),
        grid=(bsz, F_GROUPS, hp // r),
        in_specs=[pl.BlockSpec((r, r), lambda b, g, i: (0, 0)),
                  pl.BlockSpec(jc.shape, lambda b, g, i: (0, 0)),
                  pl.BlockSpec((None, r, gd), lambda b, g, i: (b, i, z_cb0 + g)),
                  pl.BlockSpec((None, r, gd), lambda b, g, i: (b, nb - 1 - i, z_cb0 + g)),
                  pl.BlockSpec((None, SUBLANES, gd),
                               lambda b, g, i: (b, jnp.minimum(nb - i, nb - 1) * (r // SUBLANES), z_cb0 + g)),
                  mat, mat],
        out_specs=[dst, dst],
        out_shape=[out, out],
        compiler_params=_params("parallel", "parallel", "arbitrary"),
        name="dft_chan",
    )(_flip_matrix(r), jc, u3, u3, u3, cc, sc)


def _dft_unfold_kernel(j_ref, jc_ref, lo_ref, hi_ref, nlo_ref, nhi_ref, o_ref, *, n_fold, w2):
    t = pl.program_id(1)
    jc = jc_ref[...]
    groups = lo_ref.shape[1] // w2

    def rows_of(left_ref, mirror_ref):
        return jnp.concatenate([_unfold_cols(left_ref[:, g * w2:(g + 1) * w2],
                                             mirror_ref[:, g * w2:(g + 1) * w2], jc)
                                for g in range(groups)], axis=1)

    @pl.when(t < n_fold)
    def _():
        o_ref[...] = rows_of(lo_ref, hi_ref)

    @pl.when(t >= n_fold)
    def _():
        row0 = rows_of(nhi_ref, nlo_ref)[0:1, :].astype(F32)
        o_ref[...] = _shifted_flip(j_ref[...], rows_of(hi_ref, lo_ref), row0).astype(o_ref.dtype)


def dft_unfold(lo, hi, jc, n, r, gd):
    bsz, _, w = lo.shape
    w2 = gd // 2 + LANES
    d = w // w2 * gd
    nb = n // r
    n_fold = nb // 2

    def src_blk(b, t):
        return (b, jnp.where(t < n_fold, t, nb - 1 - t), 0)

    def nxt_blk(b, t):
        return (b, jnp.minimum(nb - t, n_fold) * (r // SUBLANES), 0)

    return pl.pallas_call(
        functools.partial(_dft_unfold_kernel, n_fold=n_fold, w2=w2),
        grid=(bsz, nb),
        in_specs=[pl.BlockSpec((r, r), lambda b, t: (0, 0)),
                  pl.BlockSpec(jc.shape, lambda b, t: (0, 0)),
                  pl.BlockSpec((None, r, w), src_blk),
                  pl.BlockSpec((None, r, w), src_blk),
                  pl.BlockSpec((None, SUBLANES, w), nxt_blk),
                  pl.BlockSpec((None, SUBLANES, w), nxt_blk)],
        out_specs=pl.BlockSpec((None, r, d), lambda b, t: (b, t, 0)),
        out_shape=jax.ShapeDtypeStruct((bsz, n, d), BF16),
        compiler_params=_params("parallel", "arbitrary"),
        name="dft_unfold",
    )(_flip_matrix(r), jc, lo, hi, lo, hi)


def _dft_seq_kernel(cs_ref, ss_ref, p_ref, q_ref, lo_ref, hi_ref):
    a = jnp.dot(cs_ref[...], p_ref[...], preferred_element_type=F32)
    b = jnp.dot(ss_ref[...], q_ref[...], preferred_element_type=F32)
    lo_ref[...] = (a - b).astype(lo_ref.dtype)
    hi_ref[...] = (a + b).astype(hi_ref.dtype)


def dft_seq(cs, ss, p, q, gd, tm):
    bsz, hp, w = p.shape
    rows = cs.shape[0]
    out = jax.ShapeDtypeStruct((bsz, rows, w), BF16)
    mat = pl.BlockSpec((tm, hp), lambda b, g, i: (i, 0))
    src = pl.BlockSpec((None, hp, gd), lambda b, g, i: (b, 0, g))
    dst = pl.BlockSpec((None, tm, gd), lambda b, g, i: (b, i, g))
    return pl.pallas_call(
        _dft_seq_kernel,
        grid=(bsz, w // gd, rows // tm),
        in_specs=[mat, mat, src, src],
        out_specs=[dst, dst],
        out_shape=[out, out],
        compiler_params=_params("parallel", "parallel", "arbitrary"),
        name="dft_seq",
    )(cs, ss, p, q)


def fourier_mix(u3, z_cb0, gd):
    bsz, n, _ = u3.shape
    h = n // 2
    r = min(FLIP_ROWS, h)
    hp = h + r
    h2 = gd // 2
    w2 = h2 + LANES
    cc, sc = _dft_mats(gd, gd ** -0.5)
    cpad = ((0, w2 - h2 - 1), (0, w2 - h2 - 1))
    cc = jnp.pad(cc[:h2 + 1, :h2 + 1], cpad)
    sc = jnp.pad(sc[:h2 + 1, :h2 + 1], cpad)
    cs, ss = _dft_mats(n, n ** -0.5)
    tm = -(-(h + SUBLANES) // (DFT_SEQ_TILES * BF16_SUBLANES)) * BF16_SUBLANES
    pad = ((0, DFT_SEQ_TILES * tm - h - 1), (0, hp - h - 1))
    cs = jnp.pad(cs[:h + 1, :h + 1], pad)
    ss = jnp.pad(ss[:h + 1, :h + 1], pad)
    jc = _flip_matrix(h2)
    p, q = dft_chan(u3, z_cb0, gd, cc, sc, jc, hp, r)
    lo, hi = dft_seq(cs, ss, p, q, w2, tm)
    return dft_unfold(lo, hi, jc, n, r, gd)


def _tile(n, pref):
    return pref if n % pref == 0 else n


def _layer(x, xn, p_i, lw, g_next, *, bsz, seq):
    m, d = x.shape
    nh = M_HEADS
    dv = d // nh
    dk = dv // 2
    qk_w, v_w = nh * dk, nh * dv
    gd = d // F_GROUPS
    n_main = 2 * qk_w + 2 * v_w
    g0 = n_main
    z0 = n_main + 4 * nh
    chunk = min(MLSTM_CHUNK, seq)
    tm = _tile(m, MM_ROWS)

    li = lw["layer"]
    tn = _tile(d, MM_PANEL)
    w_in_t = lw["w_in_t"]
    u = panel_dots([(xn, w_in_t, "cols", li, lambda j: j * tn + jnp.where(j * tn >= g0, z0 - g0, 0))],
                   [], _ep_identity, (m, n_main + d), BF16, tm, tn, "in_proj")
    graw = panel_dots([(xn, w_in_t, "cols", li, lambda j: g0)], [], _ep_identity, (m, LANES), F32,
                      tm, LANES, "gate_proj")

    graw = graw[:, :4 * nh].reshape(bsz, seq, 2, 2, nh)
    gi = graw[:, :, :, 0].reshape(bsz, seq, 2 * nh)
    gf = graw[:, :, :, 1].reshape(bsz, seq, 2 * nh)
    b_arr, e_arr, r_arr = gate_prep(gi, gf, lw["b_igate"][li].reshape(1, 2 * nh),
                                    lw["b_fgate"][li].reshape(1, 2 * nh), chunk)
    b4 = b_arr.reshape(bsz, seq, 2, nh)
    e4 = e_arr.reshape(bsz, seq, 2, nh)
    gcol = jnp.stack([b4[:, :, 0], e4[:, :, 0], b4[:, :, 1], e4[:, :, 1]], axis=-1)
    gcol = gcol.transpose(0, 2, 1, 3)
    grow = r_arr.reshape(bsz, seq // chunk, chunk, 2, nh).transpose(0, 4, 3, 1, 2)
    grow = grow[:, :, :, :, None, :]
    u3 = u.reshape(bsz, seq, n_main + d)
    hg = mlstm(u3, gcol, grow, lw["g_head"][li], dk=dk, dv=dv, q_cb0=0, k_cb0=qk_w // dk,
               v_cb0=2 * qk_w // dv, o_cb0=(2 * qk_w + v_w) // dv, chunk=chunk)
    hg = hg.reshape(m, v_w)

    zf = fourier_mix(u3, n_main // gd, gd).reshape(m, d)

    gates = panel_dots([(xn, lw["w_merge"], "rows", li, 0)],
                       [(lw["b_merge"][li].reshape(1, 2 * d), "row", 0)],
                       _ep_bias_sigmoid, (m, 2 * d), BF16, tm, tn, "merge_gate")
    ym = panel_dots([(hg, lw["w_branch_m"], "rows", li, 0)], [(gates, "tile", 0)],
                    _ep_gate_mul, (m, d), BF16, tm, tn, "branch_m")
    mix = panel_dots([(zf, lw["w_branch_f"], "rows", li, 0)], [(gates, "tile", d // tn), (ym, "tile", 0)],
                     _ep_gate_mul_add, (m, d), BF16, tm, tn, "branch_f")
    y = panel_dots([(mix, lw["w_out"], "rows", li, 0)], [], _ep_identity, (m, d), BF16, tm, tn, "out_proj")
    x, hn = resnorm(x, y, lw["g_mix_post"][li], lw["g_mlp_pre"][li], BF16)

    d_ff = lw["w_up"].shape[-1]
    n_steps = (d_ff // tn + 1) * (m // tm)
    cast_rows_per_step = next(r for r in (LANES, 2 * LANES, 4 * LANES, 8 * LANES, d_ff)
                              if d_ff % r == 0 and d_ff // r <= n_steps)
    up, w_down = panel_dots([(hn, lw["w_up"], "rows", li, 0)], [], _ep_relu2, (m, d_ff), BF16, tm, tn,
                            "mlp_up", side_cast=(lw["w_down"], li, cast_rows_per_step))
    y = matmul_ktiled(up, w_down, BF16, tm, tn, _tile(d_ff, MM_K_TILE), "mlp_down")
    x, = resnorm(x, y, lw["g_mlp_post"][li], None, BF16)

    ple = panel_dots([(x, lw["w_ple_gate"], "rows", li, 0), (p_i, lw["w_ple_in"], "small", li, 0)],
                     [], _ep_sigmoid_mul, (m, d), BF16, tm, tn, "ple")
    if g_next is None:
        x, = resnorm(x, ple, lw["g_ple_post"][li], None, F32)
        return x, None
    return resnorm(x, ple, lw["g_ple_post"][li], g_next, BF16)


def kernel(x, p, g_mix_pre, g_mix_post, g_mlp_pre, g_mlp_post, g_ple_post, w_in, b_igate, b_fgate,
           g_head, w_branch_m, w_branch_f, w_merge, b_merge, w_out, w_up, w_down, w_ple_in, w_ple_gate):
    bsz, seq, d = x.shape
    depth = w_in.shape[0]
    m = bsz * seq
    xf = x.reshape(m, d)
    xn = rmsnorm(xf, g_mix_pre[0])
    lw = dict(w_in_t=jnp.swapaxes(w_in, 1, 2), b_igate=b_igate, b_fgate=b_fgate, g_head=g_head, w_branch_m=w_branch_m,
              w_branch_f=w_branch_f, w_merge=w_merge, b_merge=b_merge, w_out=w_out, w_up=w_up,
              w_down=w_down, w_ple_in=w_ple_in, w_ple_gate=w_ple_gate, g_mix_post=g_mix_post,
              g_mlp_pre=g_mlp_pre, g_mlp_post=g_mlp_post, g_ple_post=g_ple_post)
    for i in range(depth):
        g_next = g_mix_pre[i + 1] if i + 1 < depth else None
        xf, xn = _layer(xf, xn, p[i].reshape(m, -1), dict(lw, layer=i), g_next, bsz=bsz, seq=seq)
    return xf.reshape(bsz, seq, d)
```

```python
import functools
import math

import jax
import jax.numpy as jnp
from jax import lax
from jax.experimental import pallas as pl
from jax.experimental.pallas import tpu as pltpu

F32 = jnp.float32
BF16 = jnp.bfloat16

M_HEADS = 8
F_GROUPS = 4
GATE_CAP = 15.0
EPS = 1e-6
MLSTM_CHUNK = 256
V7X_VMEM_BYTES = 64 * 1024 * 1024
VMEM_LIMIT = V7X_VMEM_BYTES - 4 * 1024 * 1024
LANES = 128
MM_ROWS = 1024
MM_PANEL = 1024
MM_K_TILE = 4096
NORM_ROWS = 256


def _params(*semantics):
    return pltpu.CompilerParams(dimension_semantics=semantics, vmem_limit_bytes=VMEM_LIMIT)


def _rms(x, g):
    return x * lax.rsqrt(jnp.mean(x * x, axis=-1, keepdims=True) + EPS) * g


def _sigmoid(x):
    return 0.5 * jnp.tanh(0.5 * x) + 0.5


def _rmsnorm_kernel(x_ref, g_ref, o_ref):
    o_ref[...] = _rms(x_ref[...], g_ref[...]).astype(o_ref.dtype)


def rmsnorm(x, g, tm=NORM_ROWS):
    m, d = x.shape
    return pl.pallas_call(
        _rmsnorm_kernel,
        grid=(m // tm,),
        in_specs=[pl.BlockSpec((tm, d), lambda i: (i, 0)),
                  pl.BlockSpec((1, d), lambda i: (0, 0))],
        out_specs=pl.BlockSpec((tm, d), lambda i: (i, 0)),
        out_shape=jax.ShapeDtypeStruct((m, d), BF16),
        compiler_params=_params("parallel"),
        name="rmsnorm",
    )(x, g.reshape(1, d))


def _resnorm_kernel(x_ref, y_ref, gp_ref, gn_ref, xo_ref, *xn_ref):
    xnew = x_ref[...].astype(F32) + _rms(y_ref[...].astype(F32), gp_ref[...])
    xo_ref[...] = xnew.astype(xo_ref.dtype)
    if xn_ref:
        xn_ref[0][...] = _rms(xnew, gn_ref[...]).astype(xn_ref[0].dtype)


def resnorm(x, y, g_post, g_next, x_dtype, tm=NORM_ROWS):
    m, d = x.shape
    gn = g_post if g_next is None else g_next
    row = pl.BlockSpec((tm, d), lambda i: (i, 0))
    vec = pl.BlockSpec((1, d), lambda i: (0, 0))
    out_shape = [jax.ShapeDtypeStruct((m, d), x_dtype)]
    if g_next is not None:
        out_shape.append(jax.ShapeDtypeStruct((m, d), BF16))
    return pl.pallas_call(
        _resnorm_kernel,
        grid=(m // tm,),
        in_specs=[row, row, vec, vec],
        out_specs=[row] * len(out_shape),
        out_shape=out_shape,
        compiler_params=_params("parallel"),
        name="resnorm",
    )(x, y, g_post.reshape(1, d), gn.reshape(1, d))


def _panel_dots_kernel(*refs, kinds, n_extra, side, epilogue, ck):
    n_dots = len(kinds)
    n_in = 2 * n_dots + n_extra + side
    out_ref = refs[n_in]
    scratch = list(refs[n_in + 1 + side:])
    bufs = [(scratch.pop(0), scratch.pop(0)) if kinds[p] != "small" else None for p in range(n_dots)]
    g = pl.program_id(0)
    i = pl.program_id(1)
    rows = pl.ds(pl.multiple_of(i * ck, ck), ck)

    def stage(parity):
        for p in range(n_dots):
            w_ref = refs[2 * p + 1]
            if kinds[p] == "rows":
                bufs[p][parity][rows, :] = w_ref[...].astype(BF16)
            elif kinds[p] == "cols":
                bufs[p][parity][rows, :] = w_ref[0].T.astype(BF16)
        if side:
            refs[n_in + 1][...] = refs[n_in - 1][...].astype(refs[n_in + 1].dtype)

    def compute(parity):
        accs = []
        for p in range(n_dots):
            w = refs[2 * p + 1][...].astype(BF16) if kinds[p] == "small" else bufs[p][parity][...]
            accs.append(jnp.dot(refs[2 * p][...].astype(BF16), w, preferred_element_type=F32))
        extras = [r[...] for r in refs[2 * n_dots:2 * n_dots + n_extra]]
        out_ref[...] = epilogue(accs, extras).astype(out_ref.dtype)

    @pl.when(g == 0)
    def _():
        stage(0)

    for parity in (0, 1):
        @pl.when((g > 0) & (g % 2 == parity))
        def _(parity=parity):
            stage(parity)
            compute(1 - parity)


def panel_dots(pairs, extras, epilogue, out_shape, out_dtype, tm, tn, name, side_cast=None):
    m, n = out_shape
    n_i, n_j = m // tm, n // tn
    ck = None
    in_specs, args, scratch, kinds = [], [], [], []

    def jj(g):
        return jnp.maximum(g - 1, 0)

    def ii(g, i):
        return jnp.where(g == 0, 0, i)

    def pj(g):
        return jnp.minimum(g, n_j - 1)

    for a, w, kind, layer, where in pairs:
        k = a.shape[1]
        lead = () if layer is None else (layer,)
        lead_blk = () if layer is None else (None,)
        in_specs.append(pl.BlockSpec((tm, k), lambda g, i: (ii(g, i), 0)))
        if kind == "small":
            in_specs.append(pl.BlockSpec(lead_blk + (k, tn),
                                         lambda g, i, lead=lead, c=where: lead + (0, c + jj(g))))
        else:
            assert ck in (None, k // n_i) and k % n_i == 0
            ck = k // n_i
            scratch += [pltpu.VMEM((k, tn), BF16), pltpu.VMEM((k, tn), BF16)]
            if kind == "rows":
                in_specs.append(pl.BlockSpec(lead_blk + (ck, tn),
                                             lambda g, i, lead=lead, c=where: lead + (i, c + pj(g))))
            else:
                in_specs.append(pl.BlockSpec(
                    (pl.Element(1), pl.Element(tn), pl.Element(ck)),
                    lambda g, i, l=layer, f=where, ck=ck: (l, pl.multiple_of(f(pj(g)), 8), i * ck)))
        args += [a, w]
        kinds.append(kind)
    for arr, kind, cb0 in extras:
        if kind == "row":
            in_specs.append(pl.BlockSpec((1, tn), lambda g, i, cb0=cb0: (0, cb0 + jj(g))))
        else:
            in_specs.append(pl.BlockSpec((tm, tn), lambda g, i, cb0=cb0: (ii(g, i), cb0 + jj(g))))
        args.append(arr)
    out_specs = [pl.BlockSpec((tm, tn), lambda g, i: (ii(g, i), jj(g)))]
    out_shapes = [jax.ShapeDtypeStruct((m, n), out_dtype)]
    if side_cast is not None:
        sw, s_layer, s_rows = side_cast
        _, s_k, s_n = sw.shape
        n_blk = s_k // s_rows
        assert s_k % s_rows == 0 and n_blk <= (n_j + 1) * n_i

        def blk(g, i):
            return jnp.minimum(g * n_i + i, n_blk - 1)

        in_specs.append(pl.BlockSpec((None, s_rows, s_n), lambda g, i: (s_layer, blk(g, i), 0)))
        args.append(sw)
        out_specs.append(pl.BlockSpec((s_rows, s_n), lambda g, i: (blk(g, i), 0)))
        out_shapes.append(jax.ShapeDtypeStruct((s_k, s_n), BF16))
    res = pl.pallas_call(
        functools.partial(_panel_dots_kernel, kinds=tuple(kinds), n_extra=len(extras),
                          side=int(side_cast is not None), epilogue=epilogue, ck=ck),
        grid=(n_j + 1, n_i),
        in_specs=in_specs,
        out_specs=out_specs,
        out_shape=out_shapes,
        scratch_shapes=scratch,
        compiler_params=_params("arbitrary", "arbitrary"),
        name=name,
    )(*args)
    return res[0] if side_cast is None else res


def _ep_identity(accs, extras):
    return accs[0]


def _ep_bias_sigmoid(accs, extras):
    return _sigmoid(accs[0] + extras[0])


def _ep_relu2(accs, extras):
    return jnp.square(jnp.maximum(accs[0], 0.0))


def _ep_gate_mul(accs, extras):
    return extras[0].astype(F32) * accs[0]


def _ep_gate_mul_add(accs, extras):
    return extras[0].astype(F32) * accs[0] + extras[1].astype(F32)


def _ep_sigmoid_mul(accs, extras):
    return _sigmoid(accs[0]) * accs[1]


def _matmul_ktiled_kernel(a_ref, w_ref, o_ref, acc_ref):
    kk = pl.program_id(2)

    @pl.when(kk == 0)
    def _():
        acc_ref[...] = jnp.zeros_like(acc_ref)

    acc_ref[...] += jnp.dot(a_ref[...], w_ref[...], preferred_element_type=F32)

    @pl.when(kk == pl.num_programs(2) - 1)
    def _():
        o_ref[...] = acc_ref[...].astype(o_ref.dtype)


def matmul_ktiled(a, w, out_dtype, tm, tn, tk, name):
    m, k = a.shape
    n = w.shape[1]
    return pl.pallas_call(
        _matmul_ktiled_kernel,
        grid=(m // tm, n // tn, k // tk),
        in_specs=[pl.BlockSpec((tm, tk), lambda i, j, kk: (i, kk)),
                  pl.BlockSpec((tk, tn), lambda i, j, kk: (kk, j))],
        out_specs=pl.BlockSpec((tm, tn), lambda i, j, kk: (i, j)),
        out_shape=jax.ShapeDtypeStruct((m, n), out_dtype),
        scratch_shapes=[pltpu.VMEM((tm, tn), F32)],
        compiler_params=_params("parallel", "parallel", "arbitrary"),
        name=name,
    )(a, w)


def _split3(x):
    hi = x.astype(BF16)
    r1 = x - hi.astype(F32)
    mid = r1.astype(BF16)
    lo = (r1 - mid.astype(F32)).astype(BF16)
    return hi, mid, lo


def _gate_prep_kernel(gi_ref, gf_ref, bi_ref, bf_ref, b_ref, e_ref, r_ref, *, chunk):
    s, w = gi_ref.shape
    nh = w // 2
    log_i = GATE_CAP * jnp.tanh((gi_ref[...] + bi_ref[...]) * (1.0 / GATE_CAP))
    af = GATE_CAP * jnp.tanh((gf_ref[...] + bf_ref[...]) * (1.0 / GATE_CAP))
    log_f = jnp.minimum(af, 0.0) - jnp.log1p(jnp.exp(-jnp.abs(af)))
    ri = lax.broadcasted_iota(jnp.int32, (chunk, chunk), 0)
    ci = lax.broadcasted_iota(jnp.int32, (chunk, chunk), 1)
    tril = (ci <= ri).astype(BF16)
    triu = (ci >= ri).astype(BF16)
    is_fwd = lax.broadcasted_iota(jnp.int32, (chunk, w), 1) < nh
    for c in range(s // chunk):
        rows = slice(c * chunk, (c + 1) * chunk)
        parts = _split3(log_f[rows, :])
        cum_f = sum(jnp.dot(tril, p, preferred_element_type=F32) for p in parts)
        cum_b = sum(jnp.dot(triu, p, preferred_element_type=F32) for p in parts)
        b = jnp.where(is_fwd, cum_f, cum_b)
        tot = jnp.where(is_fwd[:1], cum_f[chunk - 1:chunk, :], cum_b[0:1, :])
        li = log_i[rows, :]
        b_ref[rows, :] = b
        e_ref[rows, :] = tot - b + li
        r_ref[rows, :] = li - b


def gate_prep(gi, gf, bi, bf, chunk):
    bsz, s, w = gi.shape
    blk = pl.BlockSpec((None, s, w), lambda b: (b, 0, 0))
    vec = pl.BlockSpec((1, w), lambda b: (0, 0))
    out = jax.ShapeDtypeStruct((bsz, s, w), F32)
    return pl.pallas_call(
        functools.partial(_gate_prep_kernel, chunk=chunk),
        grid=(bsz,),
        in_specs=[blk, blk, vec, vec],
        out_specs=[blk, blk, blk],
        out_shape=[out, out, out],
        compiler_params=_params("parallel"),
        name="gate_prep",
    )(gi, gf, bi, bf)


def _mlstm_kernel(q_ref, k_ref, v_ref, o_ref, gcol_ref, grow_ref, gh_ref, out_ref,
                  h_ref, c_ref, n_ref, *, chunk, q_scale):
    s = q_ref.shape[0]
    nc = s // chunk
    assert nc % 2 == 0
    ri = lax.broadcasted_iota(jnp.int32, (chunk, chunk), 0)
    ci = lax.broadcasted_iota(jnp.int32, (chunk, chunk), 1)
    masks = (ci <= ri, ci >= ri)
    c_ref[...] = jnp.zeros_like(c_ref)
    n_ref[...] = jnp.zeros_like(n_ref)

    def rows_of(c):
        return pl.ds(pl.multiple_of(c * chunk, chunk), chunk)

    def scan_pair(cf, cb):
        dirs = (0, 1)
        rows = [rows_of(c) for c in (cf, cb)]
        q = [q_ref[r, :] * q_scale for r in rows]
        k = [k_ref[r, :] for r in rows]
        v = [v_ref[r, :] for r in rows]
        gc = [gcol_ref[r, :] for r in rows]
        bcol = [gc[d][:, 2 * d:2 * d + 1] for d in dirs]
        ecol = [gc[d][:, 2 * d + 1:2 * d + 2] for d in dirs]
        rrow = [grow_ref[0, cf], grow_ref[1, cb]]
        last = (chunk - 1, 0)
        decay = [jnp.exp(bcol[d][last[d]:last[d] + 1, :]) for d in dirs]
        c_prev = [c_ref[d] for d in dirs]
        n_prev = [n_ref[d] for d in dirs]

        qk = [lax.dot_general(q[d], k[d], (((1,), (1,)), ((), ())), preferred_element_type=F32)
              for d in dirs]
        w_intra = [jnp.exp(jnp.where(masks[d], bcol[d] + rrow[d], -jnp.inf)) for d in dirs]
        w_inter = [jnp.exp(bcol[d]) for d in dirs]
        sc = [qk[d] * w_intra[d] for d in dirs]
        qf = [q[d].astype(F32) for d in dirs]
        lhs = [jnp.concatenate([sc[d].astype(BF16), (qf[d] * w_inter[d]).astype(BF16)], axis=1) for d in dirs]
        rhs = [jnp.concatenate([v[d], c_prev[d].astype(BF16)], axis=0) for d in dirs]
        num = [jnp.dot(lhs[d], rhs[d], preferred_element_type=F32) for d in dirs]
        qn = [jnp.sum(qf[d] * n_prev[d], axis=1, keepdims=True) for d in dirs]
        den = [jnp.sum(sc[d], axis=1, keepdims=True) + w_inter[d] * qn[d] for d in dirs]
        h = [num[d] * (1.0 / jnp.maximum(jnp.abs(den[d]), 1.0)) for d in dirs]

        kw = [k[d].astype(F32) * jnp.exp(ecol[d]) for d in dirs]
        upd = [lax.dot_general(kw[d].astype(BF16), v[d], (((0,), (0,)), ((), ())),
                               preferred_element_type=F32) for d in dirs]
        for d in dirs:
            c_ref[d] = decay[d] * c_prev[d] + upd[d]
            n_ref[d] = decay[d] * n_prev[d] + jnp.sum(kw[d], axis=0, keepdims=True)
        return h

    def finish(c, tot):
        rows = rows_of(c)
        y = _rms(tot, gh_ref[...]) * _sigmoid(o_ref[rows, :].astype(F32))
        out_ref[rows, :] = y.astype(out_ref.dtype)

    def first_half(t, carry):
        hf, hb = scan_pair(t, nc - 1 - t)
        h_ref[rows_of(t), :] = hf
        h_ref[rows_of(nc - 1 - t), :] = hb
        return carry

    def second_half(t, carry):
        cf, cb = t, nc - 1 - t
        hf, hb = scan_pair(cf, cb)
        finish(cf, hf + h_ref[rows_of(cf), :])
        finish(cb, hb + h_ref[rows_of(cb), :])
        return carry

    lax.fori_loop(0, nc // 2, first_half, 0)
    lax.fori_loop(nc // 2, nc, second_half, 0)


def mlstm(u3, gcol, grow, g_head, *, dk, dv, q_cb0, k_cb0, v_cb0, o_cb0, chunk):
    bsz, s, _ = u3.shape
    nh = gcol.shape[1]
    nc = s // chunk

    def col(width, cb0):
        return pl.BlockSpec((None, s, width), lambda b, h, cb0=cb0: (b, 0, cb0 + h))

    return pl.pallas_call(
        functools.partial(_mlstm_kernel, chunk=chunk, q_scale=dk ** -0.5),
        grid=(bsz, nh),
        in_specs=[col(dk, q_cb0), col(dk, k_cb0), col(dv, v_cb0), col(dv, o_cb0),
                  pl.BlockSpec((None, None, s, 4), lambda b, h: (b, h, 0, 0)),
                  pl.BlockSpec((None, None, 2, nc, 1, chunk), lambda b, h: (b, h, 0, 0, 0, 0)),
                  pl.BlockSpec((1, dv), lambda b, h: (0, h))],
        out_specs=pl.BlockSpec((None, s, dv), lambda b, h: (b, 0, h)),
        out_shape=jax.ShapeDtypeStruct((bsz, s, nh * dv), BF16),
        scratch_shapes=[pltpu.VMEM((s, dv), F32), pltpu.VMEM((2, dk, dv), F32), pltpu.VMEM((2, 1, dk), F32)],
        compiler_params=_params("parallel", "parallel"),
        name="mlstm",
    )(u3, u3, u3, u3, gcol, grow, g_head.reshape(1, nh * dv))


def _cos_sin(rows, n_cols, period):
    k = jnp.arange(n_cols, dtype=jnp.int32)
    ang = ((rows[:, None] * k[None, :]) % period).astype(F32) * (2.0 * math.pi / period)
    return jnp.cos(ang), jnp.sin(ang)


def _dft_mats(n, scale):
    n1 = 1 << ((n.bit_length() - 1) // 2)
    n2 = n // n1
    ca, sa = _cos_sin(jnp.arange(n1, dtype=jnp.int32), n, n1)
    cb, sb = _cos_sin(jnp.arange(n2, dtype=jnp.int32), n, n)
    ca, sa, cb, sb = ca[:, None, :], sa[:, None, :], cb[None, :, :], sb[None, :, :]
    c = (ca * cb - sa * sb) * scale
    s = (sa * cb + ca * sb) * scale
    return c.reshape(n, n).astype(BF16), s.reshape(n, n).astype(BF16)


FLIP_ROWS = 256
SUBLANES = 8
BF16_SUBLANES = 16
DFT_SEQ_TILES = 3


def _flip_matrix(r):
    t = jnp.arange(r, dtype=jnp.int32)
    return ((t[:, None] + t[None, :]) == r).astype(BF16)


def _shifted_flip(j, blk, row0):
    rev = jnp.dot(j, blk, preferred_element_type=F32)
    first = lax.broadcasted_iota(jnp.int32, rev.shape, 0) == 0
    return jnp.where(first, row0, rev)


def _fold_cols(x, jc, sign):
    h2 = jc.shape[0]
    rev = jnp.dot(x[:, h2:], jc, preferred_element_type=F32).astype(x.dtype)
    tail = x[:, h2:h2 + LANES]
    if sign > 0:
        lane0 = lax.broadcasted_iota(jnp.int32, tail.shape, 1) == 0
        return jnp.concatenate([x[:, :h2] + rev, jnp.where(lane0, tail, jnp.zeros_like(tail))], axis=1)
    return jnp.concatenate([x[:, :h2] - rev, jnp.zeros_like(tail)], axis=1)


def _unfold_cols(left, mirror, jc):
    h2 = jc.shape[0]
    right = jnp.dot(mirror[:, :h2], jc, preferred_element_type=F32).astype(left.dtype)
    head = right[:, :LANES]
    lane0 = lax.broadcasted_iota(jnp.int32, head.shape, 1) == 0
    head = jnp.where(lane0, left[:, h2:h2 + LANES], head)
    return jnp.concatenate([left[:, :h2], head, right[:, LANES:]], axis=1)


def _dft_chan_kernel(j_ref, jc_ref, zlo_ref, za_ref, zb_ref, cc_ref, sc_ref, p_ref, q_ref, *, n_fold, gd):
    i = pl.program_id(1)
    a = zlo_ref[...]
    row0 = jnp.where(i > 0, zb_ref[0:1, :].astype(F32), 0.0)
    r = _shifted_flip(j_ref[...], za_ref[...], row0)
    r = jnp.where(i < n_fold, r, 0.0).astype(BF16)
    jc = jc_ref[...]
    w2 = cc_ref.shape[0]
    for g in range(p_ref.shape[1] // w2):
        ag, rg = a[:, g * gd:(g + 1) * gd], r[:, g * gd:(g + 1) * gd]
        p_ref[:, g * w2:(g + 1) * w2] = jnp.dot(_fold_cols(ag + rg, jc, 1), cc_ref[...],
                                                preferred_element_type=F32).astype(p_ref.dtype)
        q_ref[:, g * w2:(g + 1) * w2] = jnp.dot(_fold_cols(ag - rg, jc, -1), sc_ref[...],
                                                preferred_element_type=F32).astype(q_ref.dtype)


def dft_chan(u3, z_blk, gd, cc, sc, jc, hp, r):
    bsz, n, _ = u3.shape
    w2 = cc.shape[0]
    d = F_GROUPS * gd
    nb = n // r
    out = jax.ShapeDtypeStruct((bsz, hp, F_GROUPS * w2), BF16)
    dst = pl.BlockSpec((None, r, F_GROUPS * w2), lambda b, i: (b, i, 0))
    mat = pl.BlockSpec((w2, w2), lambda b, i: (0, 0))
    return pl.pallas_call(
        functools.partial(_dft_chan_kernel, n_fold=nb // 2, gd=gd),
        grid=(bsz, hp // r),
        in_specs=[pl.BlockSpec((r, r), lambda b, i: (0, 0)),
                  pl.BlockSpec(jc.shape, lambda b, i: (0, 0)),
                  pl.BlockSpec((None, r, d), lambda b, i: (b, i, z_blk)),
                  pl.BlockSpec((None, r, d), lambda b, i: (b, nb - 1 - i, z_blk)),
                  pl.BlockSpec((None, SUBLANES, d),
                               lambda b, i: (b, jnp.minimum(nb - i, nb - 1) * (r // SUBLANES), z_blk)),
                  mat, mat],
        out_specs=[dst, dst],
        out_shape=[out, out],
        compiler_params=_params("parallel", "arbitrary"),
        name="dft_chan",
    )(_flip_matrix(r), jc, u3, u3, u3, cc, sc)


def _dft_unfold_kernel(j_ref, jc_ref, lo_ref, hi_ref, nlo_ref, nhi_ref, o_ref, *, n_fold, w2):
    t = pl.program_id(1)
    jc = jc_ref[...]
    groups = lo_ref.shape[1] // w2

    def rows_of(left_ref, mirror_ref):
        return jnp.concatenate([_unfold_cols(left_ref[:, g * w2:(g + 1) * w2],
                                             mirror_ref[:, g * w2:(g + 1) * w2], jc)
                                for g in range(groups)], axis=1)

    @pl.when(t < n_fold)
    def _():
        o_ref[...] = rows_of(lo_ref, hi_ref)

    @pl.when(t >= n_fold)
    def _():
        row0 = rows_of(nhi_ref, nlo_ref)[0:1, :].astype(F32)
        o_ref[...] = _shifted_flip(j_ref[...], rows_of(hi_ref, lo_ref), row0).astype(o_ref.dtype)


def dft_unfold(lo, hi, jc, n, r, gd):
    bsz, _, w = lo.shape
    w2 = gd // 2 + LANES
    d = w // w2 * gd
    nb = n // r
    n_fold = nb // 2

    def src_blk(b, t):
        return (b, jnp.where(t < n_fold, t, nb - 1 - t), 0)

    def nxt_blk(b, t):
        return (b, jnp.minimum(nb - t, n_fold) * (r // SUBLANES), 0)

    return pl.pallas_call(
        functools.partial(_dft_unfold_kernel, n_fold=n_fold, w2=w2),
        grid=(bsz, nb),
        in_specs=[pl.BlockSpec((r, r), lambda b, t: (0, 0)),
                  pl.BlockSpec(jc.shape, lambda b, t: (0, 0)),
                  pl.BlockSpec((None, r, w), src_blk),
                  pl.BlockSpec((None, r, w), src_blk),
                  pl.BlockSpec((None, SUBLANES, w), nxt_blk),
                  pl.BlockSpec((None, SUBLANES, w), nxt_blk)],
        out_specs=pl.BlockSpec((None, r, d), lambda b, t: (b, t, 0)),
        out_shape=jax.ShapeDtypeStruct((bsz, n, d), BF16),
        compiler_params=_params("parallel", "arbitrary"),
        name="dft_unfold",
    )(_flip_matrix(r), jc, lo, hi, lo, hi)


def _dft_seq_kernel(cs_ref, ss_ref, p_ref, q_ref, lo_ref, hi_ref):
    a = jnp.dot(cs_ref[...], p_ref[...], preferred_element_type=F32)
    b = jnp.dot(ss_ref[...], q_ref[...], preferred_element_type=F32)
    lo_ref[...] = (a - b).astype(lo_ref.dtype)
    hi_ref[...] = (a + b).astype(hi_ref.dtype)


def dft_seq(cs, ss, p, q, gd, tm):
    bsz, hp, w = p.shape
    rows = cs.shape[0]
    out = jax.ShapeDtypeStruct((bsz, rows, w), BF16)
    mat = pl.BlockSpec((tm, hp), lambda b, g, i: (i, 0))
    src = pl.BlockSpec((None, hp, gd), lambda b, g, i: (b, 0, g))
    dst = pl.BlockSpec((None, tm, gd), lambda b, g, i: (b, i, g))
    return pl.pallas_call(
        _dft_seq_kernel,
        grid=(bsz, w // gd, rows // tm),
        in_specs=[mat, mat, src, src],
        out_specs=[dst, dst],
        out_shape=[out, out],
        compiler_params=_params("parallel", "parallel", "arbitrary"),
        name="dft_seq",
    )(cs, ss, p, q)


def fourier_mix(u3, z_cb0, gd):
    bsz, n, _ = u3.shape
    h = n // 2
    r = min(FLIP_ROWS, h)
    hp = h + r
    h2 = gd // 2
    w2 = h2 + LANES
    cc, sc = _dft_mats(gd, gd ** -0.5)
    cpad = ((0, w2 - h2 - 1), (0, w2 - h2 - 1))
    cc = jnp.pad(cc[:h2 + 1, :h2 + 1], cpad)
    sc = jnp.pad(sc[:h2 + 1, :h2 + 1], cpad)
    cs, ss = _dft_mats(n, n ** -0.5)
    tm = -(-(h + SUBLANES) // (DFT_SEQ_TILES * BF16_SUBLANES)) * BF16_SUBLANES
    pad = ((0, DFT_SEQ_TILES * tm - h - 1), (0, hp - h - 1))
    cs = jnp.pad(cs[:h + 1, :h + 1], pad)
    ss = jnp.pad(ss[:h + 1, :h + 1], pad)
    jc = _flip_matrix(h2)
    assert z_cb0 % F_GROUPS == 0
    p, q = dft_chan(u3, z_cb0 // F_GROUPS, gd, cc, sc, jc, hp, r)
    lo, hi = dft_seq(cs, ss, p, q, w2, tm)
    return dft_unfold(lo, hi, jc, n, r, gd)


def _tile(n, pref):
    return pref if n % pref == 0 else n


def _layer(x, xn, p_i, lw, g_next, *, bsz, seq):
    m, d = x.shape
    nh = M_HEADS
    dv = d // nh
    dk = dv // 2
    qk_w, v_w = nh * dk, nh * dv
    gd = d // F_GROUPS
    n_main = 2 * qk_w + 2 * v_w
    g0 = n_main
    z0 = n_main + 4 * nh
    chunk = min(MLSTM_CHUNK, seq)
    tm = _tile(m, MM_ROWS)

    li = lw["layer"]
    tn = _tile(d, MM_PANEL)
    w_in_t = lw["w_in_t"]
    u = panel_dots([(xn, w_in_t, "cols", li, lambda j: j * tn + jnp.where(j * tn >= g0, z0 - g0, 0))],
                   [], _ep_identity, (m, n_main + d), BF16, tm, tn, "in_proj")
    graw = panel_dots([(xn, w_in_t, "cols", li, lambda j: g0)], [], _ep_identity, (m, LANES), F32,
                      tm, LANES, "gate_proj")

    graw = graw[:, :4 * nh].reshape(bsz, seq, 2, 2, nh)
    gi = graw[:, :, :, 0].reshape(bsz, seq, 2 * nh)
    gf = graw[:, :, :, 1].reshape(bsz, seq, 2 * nh)
    b_arr, e_arr, r_arr = gate_prep(gi, gf, lw["b_igate"][li].reshape(1, 2 * nh),
                                    lw["b_fgate"][li].reshape(1, 2 * nh), chunk)
    b4 = b_arr.reshape(bsz, seq, 2, nh)
    e4 = e_arr.reshape(bsz, seq, 2, nh)
    gcol = jnp.stack([b4[:, :, 0], e4[:, :, 0], b4[:, :, 1], e4[:, :, 1]], axis=-1)
    gcol = gcol.transpose(0, 2, 1, 3)
    grow = r_arr.reshape(bsz, seq // chunk, chunk, 2, nh).transpose(0, 4, 3, 1, 2)
    grow = grow[:, :, :, :, None, :]
    u3 = u.reshape(bsz, seq, n_main + d)
    hg = mlstm(u3, gcol, grow, lw["g_head"][li], dk=dk, dv=dv, q_cb0=0, k_cb0=qk_w // dk,
               v_cb0=2 * qk_w // dv, o_cb0=(2 * qk_w + v_w) // dv, chunk=chunk)
    hg = hg.reshape(m, v_w)

    zf = fourier_mix(u3, n_main // gd, gd).reshape(m, d)

    gates = panel_dots([(xn, lw["w_merge"], "rows", li, 0)],
                       [(lw["b_merge"][li].reshape(1, 2 * d), "row", 0)],
                       _ep_bias_sigmoid, (m, 2 * d), BF16, tm, tn, "merge_gate")
    ym = panel_dots([(hg, lw["w_branch_m"], "rows", li, 0)], [(gates, "tile", 0)],
                    _ep_gate_mul, (m, d), BF16, tm, tn, "branch_m")
    mix = panel_dots([(zf, lw["w_branch_f"], "rows", li, 0)], [(gates, "tile", d // tn), (ym, "tile", 0)],
                     _ep_gate_mul_add, (m, d), BF16, tm, tn, "branch_f")
    y = panel_dots([(mix, lw["w_out"], "rows", li, 0)], [], _ep_identity, (m, d), BF16, tm, tn, "out_proj")
    x, hn = resnorm(x, y, lw["g_mix_post"][li], lw["g_mlp_pre"][li], BF16)

    d_ff = lw["w_up"].shape[-1]
    n_steps = (d_ff // tn + 1) * (m // tm)
    cast_rows_per_step = next(r for r in (LANES, 2 * LANES, 4 * LANES, 8 * LANES, d_ff)
                              if d_ff % r == 0 and d_ff // r <= n_steps)
    up, w_down = panel_dots([(hn, lw["w_up"], "rows", li, 0)], [], _ep_relu2, (m, d_ff), BF16, tm, tn,
                            "mlp_up", side_cast=(lw["w_down"], li, cast_rows_per_step))
    y = matmul_ktiled(up, w_down, BF16, tm, tn, _tile(d_ff, MM_K_TILE), "mlp_down")
    x, = resnorm(x, y, lw["g_mlp_post"][li], None, BF16)

    ple = panel_dots([(x, lw["w_ple_gate"], "rows", li, 0), (p_i, lw["w_ple_in"], "small", li, 0)],
                     [], _ep_sigmoid_mul, (m, d), BF16, tm, tn, "ple")
    if g_next is None:
        x, = resnorm(x, ple, lw["g_ple_post"][li], None, F32)
        return x, None
    return resnorm(x, ple, lw["g_ple_post"][li], g_next, BF16)


def kernel(x, p, g_mix_pre, g_mix_post, g_mlp_pre, g_mlp_post, g_ple_post, w_in, b_igate, b_fgate,
           g_head, w_branch_m, w_branch_f, w_merge, b_merge, w_out, w_up, w_down, w_ple_in, w_ple_gate):
    bsz, seq, d = x.shape
    depth = w_in.shape[0]
    m = bsz * seq
    xf = x.reshape(m, d)
    xn = rmsnorm(xf, g_mix_pre[0])
    lw = dict(w_in_t=jnp.swapaxes(w_in, 1, 2), b_igate=b_igate, b_fgate=b_fgate, g_head=g_head, w_branch_m=w_branch_m,
              w_branch_f=w_branch_f, w_merge=w_merge, b_merge=b_merge, w_out=w_out, w_up=w_up,
              w_down=w_down, w_ple_in=w_ple_in, w_ple_gate=w_ple_gate, g_mix_post=g_mix_post,
              g_mlp_pre=g_mlp_pre, g_mlp_post=g_mlp_post, g_ple_post=g_ple_post)
    for i in range(depth):
        g_next = g_mix_pre[i + 1] if i + 1 < depth else None
        xf, xn = _layer(xf, xn, p[i].reshape(m, -1), dict(lw, layer=i), g_next, bsz=bsz, seq=seq)
    return xf.reshape(bsz, seq, d)
```

```python
import functools
import math

import jax
import jax.numpy as jnp
from jax import lax
from jax.experimental import pallas as pl
from jax.experimental.pallas import tpu as pltpu

F32 = jnp.float32
BF16 = jnp.bfloat16

M_HEADS = 8
F_GROUPS = 4
GATE_CAP = 15.0
EPS = 1e-6
MLSTM_CHUNK = 256
V7X_VMEM_BYTES = 64 * 1024 * 1024
VMEM_LIMIT = V7X_VMEM_BYTES - 4 * 1024 * 1024
LANES = 128
MM_ROWS = 1024
MM_PANEL = 1024
MM_K_TILE = 4096
NORM_ROWS = 512


def _params(*semantics):
    return pltpu.CompilerParams(dimension_semantics=semantics, vmem_limit_bytes=VMEM_LIMIT)


def _rms(x, g):
    return x * lax.rsqrt(jnp.mean(x * x, axis=-1, keepdims=True) + EPS) * g


def _sigmoid(x):
    return 0.5 * jnp.tanh(0.5 * x) + 0.5


def _rmsnorm_kernel(x_ref, g_ref, o_ref):
    o_ref[...] = _rms(x_ref[...], g_ref[...]).astype(o_ref.dtype)


def rmsnorm(x, g, tm=NORM_ROWS):
    m, d = x.shape
    return pl.pallas_call(
        _rmsnorm_kernel,
        grid=(m // tm,),
        in_specs=[pl.BlockSpec((tm, d), lambda i: (i, 0)),
                  pl.BlockSpec((1, d), lambda i: (0, 0))],
        out_specs=pl.BlockSpec((tm, d), lambda i: (i, 0)),
        out_shape=jax.ShapeDtypeStruct((m, d), BF16),
        compiler_params=_params("parallel"),
        name="rmsnorm",
    )(x, g.reshape(1, d))


def _resnorm_kernel(x_ref, y_ref, gp_ref, gn_ref, xo_ref, *xn_ref):
    xnew = x_ref[...].astype(F32) + _rms(y_ref[...].astype(F32), gp_ref[...])
    xo_ref[...] = xnew.astype(xo_ref.dtype)
    if xn_ref:
        xn_ref[0][...] = _rms(xnew, gn_ref[...]).astype(xn_ref[0].dtype)


def resnorm(x, y, g_post, g_next, x_dtype, tm=NORM_ROWS):
    m, d = x.shape
    gn = g_post if g_next is None else g_next
    row = pl.BlockSpec((tm, d), lambda i: (i, 0))
    vec = pl.BlockSpec((1, d), lambda i: (0, 0))
    out_shape = [jax.ShapeDtypeStruct((m, d), x_dtype)]
    if g_next is not None:
        out_shape.append(jax.ShapeDtypeStruct((m, d), BF16))
    return pl.pallas_call(
        _resnorm_kernel,
        grid=(m // tm,),
        in_specs=[row, row, vec, vec],
        out_specs=[row] * len(out_shape),
        out_shape=out_shape,
        compiler_params=_params("parallel"),
        name="resnorm",
    )(x, y, g_post.reshape(1, d), gn.reshape(1, d))


def _panel_dots_kernel(*refs, kinds, n_extra, side, epilogue, ck):
    n_dots = len(kinds)
    n_in = 2 * n_dots + n_extra + side
    out_ref = refs[n_in]
    scratch = list(refs[n_in + 1 + side:])
    bufs = [(scratch.pop(0), scratch.pop(0)) if kinds[p] != "small" else None for p in range(n_dots)]
    g = pl.program_id(0)
    i = pl.program_id(1)
    rows = pl.ds(pl.multiple_of(i * ck, ck), ck)

    def stage(parity):
        for p in range(n_dots):
            w_ref = refs[2 * p + 1]
            if kinds[p] == "rows":
                bufs[p][parity][rows, :] = w_ref[...].astype(BF16)
            elif kinds[p] == "cols":
                bufs[p][parity][rows, :] = w_ref[0].T.astype(BF16)
        if side:
            refs[n_in + 1][...] = refs[n_in - 1][...].astype(refs[n_in + 1].dtype)

    def compute(parity):
        accs = []
        for p in range(n_dots):
            w = refs[2 * p + 1][...].astype(BF16) if kinds[p] == "small" else bufs[p][parity][...]
            accs.append(jnp.dot(refs[2 * p][...].astype(BF16), w, preferred_element_type=F32))
        extras = [r[...] for r in refs[2 * n_dots:2 * n_dots + n_extra]]
        out_ref[...] = epilogue(accs, extras).astype(out_ref.dtype)

    @pl.when(g == 0)
    def _():
        stage(0)

    for parity in (0, 1):
        @pl.when((g > 0) & (g % 2 == parity))
        def _(parity=parity):
            stage(parity)
            compute(1 - parity)


def panel_dots(pairs, extras, epilogue, out_shape, out_dtype, tm, tn, name, side_cast=None):
    m, n = out_shape
    n_i, n_j = m // tm, n // tn
    ck = None
    in_specs, args, scratch, kinds = [], [], [], []

    def jj(g):
        return jnp.maximum(g - 1, 0)

    def ii(g, i):
        return jnp.where(g == 0, 0, i)

    def pj(g):
        return jnp.minimum(g, n_j - 1)

    for a, w, kind, layer, where in pairs:
        k = a.shape[1]
        lead = () if layer is None else (layer,)
        lead_blk = () if layer is None else (None,)
        in_specs.append(pl.BlockSpec((tm, k), lambda g, i: (ii(g, i), 0)))
        if kind == "small":
            in_specs.append(pl.BlockSpec(lead_blk + (k, tn),
                                         lambda g, i, lead=lead, c=where: lead + (0, c + jj(g))))
        else:
            assert ck in (None, k // n_i) and k % n_i == 0
            ck = k // n_i
            scratch += [pltpu.VMEM((k, tn), BF16), pltpu.VMEM((k, tn), BF16)]
            if kind == "rows":
                in_specs.append(pl.BlockSpec(lead_blk + (ck, tn),
                                             lambda g, i, lead=lead, c=where: lead + (i, c + pj(g))))
            else:
                in_specs.append(pl.BlockSpec(
                    (pl.Element(1), pl.Element(tn), pl.Element(ck)),
                    lambda g, i, l=layer, f=where, ck=ck: (l, pl.multiple_of(f(pj(g)), 8), i * ck)))
        args += [a, w]
        kinds.append(kind)
    for arr, kind, cb0 in extras:
        if kind == "row":
            in_specs.append(pl.BlockSpec((1, tn), lambda g, i, cb0=cb0: (0, cb0 + jj(g))))
        else:
            in_specs.append(pl.BlockSpec((tm, tn), lambda g, i, cb0=cb0: (ii(g, i), cb0 + jj(g))))
        args.append(arr)
    out_specs = [pl.BlockSpec((tm, tn), lambda g, i: (ii(g, i), jj(g)))]
    out_shapes = [jax.ShapeDtypeStruct((m, n), out_dtype)]
    if side_cast is not None:
        sw, s_layer, s_rows = side_cast
        _, s_k, s_n = sw.shape
        n_blk = s_k // s_rows
        assert s_k % s_rows == 0 and n_blk <= (n_j + 1) * n_i

        def blk(g, i):
            return jnp.minimum(g * n_i + i, n_blk - 1)

        in_specs.append(pl.BlockSpec((None, s_rows, s_n), lambda g, i: (s_layer, blk(g, i), 0)))
        args.append(sw)
        out_specs.append(pl.BlockSpec((s_rows, s_n), lambda g, i: (blk(g, i), 0)))
        out_shapes.append(jax.ShapeDtypeStruct((s_k, s_n), BF16))
    res = pl.pallas_call(
        functools.partial(_panel_dots_kernel, kinds=tuple(kinds), n_extra=len(extras),
                          side=int(side_cast is not None), epilogue=epilogue, ck=ck),
        grid=(n_j + 1, n_i),
        in_specs=in_specs,
        out_specs=out_specs,
        out_shape=out_shapes,
        scratch_shapes=scratch,
        compiler_params=_params("arbitrary", "arbitrary"),
        name=name,
    )(*args)
    return res[0] if side_cast is None else res


def _ep_identity(accs, extras):
    return accs[0]


def _ep_bias_sigmoid(accs, extras):
    return _sigmoid(accs[0] + extras[0])


def _ep_relu2(accs, extras):
    return jnp.square(jnp.maximum(accs[0], 0.0))


def _ep_gate_mul(accs, extras):
    return extras[0].astype(F32) * accs[0]


def _ep_gate_mul_add(accs, extras):
    return extras[0].astype(F32) * accs[0] + extras[1].astype(F32)


def _ep_sigmoid_mul(accs, extras):
    return _sigmoid(accs[0]) * accs[1]


def _matmul_ktiled_kernel(a_ref, w_ref, o_ref, acc_ref):
    kk = pl.program_id(2)

    @pl.when(kk == 0)
    def _():
        acc_ref[...] = jnp.zeros_like(acc_ref)

    acc_ref[...] += jnp.dot(a_ref[...], w_ref[...], preferred_element_type=F32)

    @pl.when(kk == pl.num_programs(2) - 1)
    def _():
        o_ref[...] = acc_ref[...].astype(o_ref.dtype)


def matmul_ktiled(a, w, out_dtype, tm, tn, tk, name):
    m, k = a.shape
    n = w.shape[1]
    return pl.pallas_call(
        _matmul_ktiled_kernel,
        grid=(m // tm, n // tn, k // tk),
        in_specs=[pl.BlockSpec((tm, tk), lambda i, j, kk: (i, kk)),
                  pl.BlockSpec((tk, tn), lambda i, j, kk: (kk, j))],
        out_specs=pl.BlockSpec((tm, tn), lambda i, j, kk: (i, j)),
        out_shape=jax.ShapeDtypeStruct((m, n), out_dtype),
        scratch_shapes=[pltpu.VMEM((tm, tn), F32)],
        compiler_params=_params("parallel", "parallel", "arbitrary"),
        name=name,
    )(a, w)


def _split3(x):
    hi = x.astype(BF16)
    r1 = x - hi.astype(F32)
    mid = r1.astype(BF16)
    lo = (r1 - mid.astype(F32)).astype(BF16)
    return hi, mid, lo


def _gate_prep_kernel(gi_ref, gf_ref, bi_ref, bf_ref, b_ref, e_ref, r_ref, *, chunk):
    s, w = gi_ref.shape
    nh = w // 2
    log_i = GATE_CAP * jnp.tanh((gi_ref[...] + bi_ref[...]) * (1.0 / GATE_CAP))
    af = GATE_CAP * jnp.tanh((gf_ref[...] + bf_ref[...]) * (1.0 / GATE_CAP))
    log_f = jnp.minimum(af, 0.0) - jnp.log1p(jnp.exp(-jnp.abs(af)))
    ri = lax.broadcasted_iota(jnp.int32, (chunk, chunk), 0)
    ci = lax.broadcasted_iota(jnp.int32, (chunk, chunk), 1)
    tril = (ci <= ri).astype(BF16)
    triu = (ci >= ri).astype(BF16)
    is_fwd = lax.broadcasted_iota(jnp.int32, (chunk, w), 1) < nh
    for c in range(s // chunk):
        rows = slice(c * chunk, (c + 1) * chunk)
        parts = _split3(log_f[rows, :])
        cum_f = sum(jnp.dot(tril, p, preferred_element_type=F32) for p in parts)
        cum_b = sum(jnp.dot(triu, p, preferred_element_type=F32) for p in parts)
        b = jnp.where(is_fwd, cum_f, cum_b)
        tot = jnp.where(is_fwd[:1], cum_f[chunk - 1:chunk, :], cum_b[0:1, :])
        li = log_i[rows, :]
        b_ref[rows, :] = b
        e_ref[rows, :] = tot - b + li
        r_ref[rows, :] = li - b


def gate_prep(gi, gf, bi, bf, chunk):
    bsz, s, w = gi.shape
    blk = pl.BlockSpec((None, s, w), lambda b: (b, 0, 0))
    vec = pl.BlockSpec((1, w), lambda b: (0, 0))
    out = jax.ShapeDtypeStruct((bsz, s, w), F32)
    return pl.pallas_call(
        functools.partial(_gate_prep_kernel, chunk=chunk),
        grid=(bsz,),
        in_specs=[blk, blk, vec, vec],
        out_specs=[blk, blk, blk],
        out_shape=[out, out, out],
        compiler_params=_params("parallel"),
        name="gate_prep",
    )(gi, gf, bi, bf)


def _mlstm_kernel(q_ref, k_ref, v_ref, o_ref, gcol_ref, grow_ref, gh_ref, out_ref,
                  h_ref, c_ref, n_ref, *, chunk, q_scale):
    s = q_ref.shape[0]
    nc = s // chunk
    assert nc % 2 == 0
    ri = lax.broadcasted_iota(jnp.int32, (chunk, chunk), 0)
    ci = lax.broadcasted_iota(jnp.int32, (chunk, chunk), 1)
    masks = (ci <= ri, ci >= ri)
    c_ref[...] = jnp.zeros_like(c_ref)
    n_ref[...] = jnp.zeros_like(n_ref)

    def rows_of(c):
        return pl.ds(pl.multiple_of(c * chunk, chunk), chunk)

    def scan_pair(cf, cb):
        dirs = (0, 1)
        rows = [rows_of(c) for c in (cf, cb)]
        q = [q_ref[r, :] * q_scale for r in rows]
        k = [k_ref[r, :] for r in rows]
        v = [v_ref[r, :] for r in rows]
        gc = [gcol_ref[r, :] for r in rows]
        bcol = [gc[d][:, 2 * d:2 * d + 1] for d in dirs]
        ecol = [gc[d][:, 2 * d + 1:2 * d + 2] for d in dirs]
        rrow = [grow_ref[0, cf], grow_ref[1, cb]]
        last = (chunk - 1, 0)
        decay = [jnp.exp(bcol[d][last[d]:last[d] + 1, :]) for d in dirs]
        c_prev = [c_ref[d] for d in dirs]
        n_prev = [n_ref[d] for d in dirs]

        qk = [lax.dot_general(q[d], k[d], (((1,), (1,)), ((), ())), preferred_element_type=F32)
              for d in dirs]
        bb = [jnp.broadcast_to(bcol[d], (chunk, LANES)) for d in dirs]
        wide = lambda x, n: jnp.concatenate([x] * (n // LANES), axis=1)
        w_intra = [jnp.exp(jnp.where(masks[d], wide(bb[d], chunk) + rrow[d], -jnp.inf)) for d in dirs]
        w_inter = [jnp.exp(bb[d]) for d in dirs]
        sc = [qk[d] * w_intra[d] for d in dirs]
        qf = [q[d].astype(F32) for d in dirs]
        lhs = [jnp.concatenate([sc[d].astype(BF16), (qf[d] * wide(w_inter[d], qf[d].shape[1])).astype(BF16)], axis=1) for d in dirs]
        rhs = [jnp.concatenate([v[d], c_prev[d].astype(BF16)], axis=0) for d in dirs]
        num = [jnp.dot(lhs[d], rhs[d], preferred_element_type=F32) for d in dirs]
        qn = [jnp.sum(qf[d] * n_prev[d], axis=1, keepdims=True) for d in dirs]
        den = [jnp.sum(sc[d], axis=1, keepdims=True) + w_inter[d][:, :1] * qn[d] for d in dirs]
        h = [num[d] * (1.0 / jnp.maximum(jnp.abs(den[d]), 1.0)) for d in dirs]

        kw = [k[d].astype(F32) * jnp.exp(ecol[d]) for d in dirs]
        upd = [lax.dot_general(kw[d].astype(BF16), v[d], (((0,), (0,)), ((), ())),
                               preferred_element_type=F32) for d in dirs]
        for d in dirs:
            c_ref[d] = decay[d] * c_prev[d] + upd[d]
            n_ref[d] = decay[d] * n_prev[d] + jnp.sum(kw[d], axis=0, keepdims=True)
        return h

    def finish(c, tot):
        rows = rows_of(c)
        y = _rms(tot, gh_ref[...]) * _sigmoid(o_ref[rows, :].astype(F32))
        out_ref[rows, :] = y.astype(out_ref.dtype)

    def first_half(t, carry):
        hf, hb = scan_pair(t, nc - 1 - t)
        h_ref[rows_of(t), :] = hf
        h_ref[rows_of(nc - 1 - t), :] = hb
        return carry

    def second_half(t, carry):
        cf, cb = t, nc - 1 - t
        hf, hb = scan_pair(cf, cb)
        finish(cf, hf + h_ref[rows_of(cf), :])
        finish(cb, hb + h_ref[rows_of(cb), :])
        return carry

    lax.fori_loop(0, nc // 2, first_half, 0)
    lax.fori_loop(nc // 2, nc, second_half, 0)


def mlstm(u3, gcol, grow, g_head, *, dk, dv, q_cb0, k_cb0, v_cb0, o_cb0, chunk):
    bsz, s, _ = u3.shape
    nh = gcol.shape[1]
    nc = s // chunk

    def col(width, cb0):
        return pl.BlockSpec((None, s, width), lambda b, h, cb0=cb0: (b, 0, cb0 + h))

    return pl.pallas_call(
        functools.partial(_mlstm_kernel, chunk=chunk, q_scale=dk ** -0.5),
        grid=(bsz, nh),
        in_specs=[col(dk, q_cb0), col(dk, k_cb0), col(dv, v_cb0), col(dv, o_cb0),
                  pl.BlockSpec((None, None, s, 4), lambda b, h: (b, h, 0, 0)),
                  pl.BlockSpec((None, None, 2, nc, 1, chunk), lambda b, h: (b, h, 0, 0, 0, 0)),
                  pl.BlockSpec((1, dv), lambda b, h: (0, h))],
        out_specs=pl.BlockSpec((None, s, dv), lambda b, h: (b, 0, h)),
        out_shape=jax.ShapeDtypeStruct((bsz, s, nh * dv), BF16),
        scratch_shapes=[pltpu.VMEM((s, dv), F32), pltpu.VMEM((2, dk, dv), F32), pltpu.VMEM((2, 1, dk), F32)],
        compiler_params=_params("parallel", "parallel"),
        name="mlstm",
    )(u3, u3, u3, u3, gcol, grow, g_head.reshape(1, nh * dv))


def _cos_sin(rows, n_cols, period):
    k = jnp.arange(n_cols, dtype=jnp.int32)
    ang = ((rows[:, None] * k[None, :]) % period).astype(F32) * (2.0 * math.pi / period)
    return jnp.cos(ang), jnp.sin(ang)


def _dft_mats(n, scale):
    n1 = 1 << ((n.bit_length() - 1) // 2)
    n2 = n // n1
    ca, sa = _cos_sin(jnp.arange(n1, dtype=jnp.int32), n, n1)
    cb, sb = _cos_sin(jnp.arange(n2, dtype=jnp.int32), n, n)
    ca, sa, cb, sb = ca[:, None, :], sa[:, None, :], cb[None, :, :], sb[None, :, :]
    c = (ca * cb - sa * sb) * scale
    s = (sa * cb + ca * sb) * scale
    return c.reshape(n, n).astype(BF16), s.reshape(n, n).astype(BF16)


FLIP_ROWS = 256
SUBLANES = 8
BF16_SUBLANES = 16
DFT_SEQ_TILES = 3


def _flip_matrix(r):
    t = jnp.arange(r, dtype=jnp.int32)
    return ((t[:, None] + t[None, :]) == r).astype(BF16)


def _shifted_flip(j, blk, row0):
    rev = jnp.dot(j, blk, preferred_element_type=F32)
    first = lax.broadcasted_iota(jnp.int32, rev.shape, 0) == 0
    return jnp.where(first, row0, rev)


def _fold_cols(x, jc, sign):
    h2 = jc.shape[0]
    rev = jnp.dot(x[:, h2:], jc, preferred_element_type=F32).astype(x.dtype)
    tail = x[:, h2:h2 + LANES]
    if sign > 0:
        lane0 = lax.broadcasted_iota(jnp.int32, tail.shape, 1) == 0
        return jnp.concatenate([x[:, :h2] + rev, jnp.where(lane0, tail, jnp.zeros_like(tail))], axis=1)
    return jnp.concatenate([x[:, :h2] - rev, jnp.zeros_like(tail)], axis=1)


def _unfold_cols(left, mirror, jc):
    h2 = jc.shape[0]
    right = jnp.dot(mirror[:, :h2], jc, preferred_element_type=F32).astype(left.dtype)
    head = right[:, :LANES]
    lane0 = lax.broadcasted_iota(jnp.int32, head.shape, 1) == 0
    head = jnp.where(lane0, left[:, h2:h2 + LANES], head)
    return jnp.concatenate([left[:, :h2], head, right[:, LANES:]], axis=1)


def _dft_chan_kernel(j_ref, jc_ref, zlo_ref, za_ref, zb_ref, cc_ref, sc_ref, p_ref, q_ref, *, n_fold, gd):
    i = pl.program_id(1)
    a = zlo_ref[...]
    row0 = jnp.where(i > 0, zb_ref[0:1, :].astype(F32), 0.0)
    r = _shifted_flip(j_ref[...], za_ref[...], row0)
    r = jnp.where(i < n_fold, r, 0.0).astype(BF16)
    jc = jc_ref[...]
    w2 = cc_ref.shape[0]
    for g in range(p_ref.shape[1] // w2):
        ag, rg = a[:, g * gd:(g + 1) * gd], r[:, g * gd:(g + 1) * gd]
        p_ref[:, g * w2:(g + 1) * w2] = jnp.dot(_fold_cols(ag + rg, jc, 1), cc_ref[...],
                                                preferred_element_type=F32).astype(p_ref.dtype)
        q_ref[:, g * w2:(g + 1) * w2] = jnp.dot(_fold_cols(ag - rg, jc, -1), sc_ref[...],
                                                preferred_element_type=F32).astype(q_ref.dtype)


def dft_chan(u3, z_blk, gd, cc, sc, jc, hp, r):
    bsz, n, _ = u3.shape
    w2 = cc.shape[0]
    d = F_GROUPS * gd
    nb = n // r
    out = jax.ShapeDtypeStruct((bsz, hp, F_GROUPS * w2), BF16)
    dst = pl.BlockSpec((None, r, F_GROUPS * w2), lambda b, i: (b, i, 0))
    mat = pl.BlockSpec((w2, w2), lambda b, i: (0, 0))
    return pl.pallas_call(
        functools.partial(_dft_chan_kernel, n_fold=nb // 2, gd=gd),
        grid=(bsz, hp // r),
        in_specs=[pl.BlockSpec((r, r), lambda b, i: (0, 0)),
                  pl.BlockSpec(jc.shape, lambda b, i: (0, 0)),
                  pl.BlockSpec((None, r, d), lambda b, i: (b, i, z_blk)),
                  pl.BlockSpec((None, r, d), lambda b, i: (b, nb - 1 - i, z_blk)),
                  pl.BlockSpec((None, SUBLANES, d),
                               lambda b, i: (b, jnp.minimum(nb - i, nb - 1) * (r // SUBLANES), z_blk)),
                  mat, mat],
        out_specs=[dst, dst],
        out_shape=[out, out],
        compiler_params=_params("parallel", "arbitrary"),
        name="dft_chan",
    )(_flip_matrix(r), jc, u3, u3, u3, cc, sc)


def _dft_unfold_kernel(j_ref, jc_ref, lo_ref, hi_ref, nlo_ref, nhi_ref, o_ref, *, n_fold, w2):
    t = pl.program_id(1)
    jc = jc_ref[...]
    groups = lo_ref.shape[1] // w2

    def rows_of(left_ref, mirror_ref):
        return jnp.concatenate([_unfold_cols(left_ref[:, g * w2:(g + 1) * w2],
                                             mirror_ref[:, g * w2:(g + 1) * w2], jc)
                                for g in range(groups)], axis=1)

    @pl.when(t < n_fold)
    def _():
        o_ref[...] = rows_of(lo_ref, hi_ref)

    @pl.when(t >= n_fold)
    def _():
        row0 = rows_of(nhi_ref, nlo_ref)[0:1, :].astype(F32)
        o_ref[...] = _shifted_flip(j_ref[...], rows_of(hi_ref, lo_ref), row0).astype(o_ref.dtype)


def dft_unfold(lo, hi, jc, n, r, gd):
    bsz, _, w = lo.shape
    w2 = gd // 2 + LANES
    d = w // w2 * gd
    nb = n // r
    n_fold = nb // 2

    def src_blk(b, t):
        return (b, jnp.where(t < n_fold, t, nb - 1 - t), 0)

    def nxt_blk(b, t):
        return (b, jnp.minimum(nb - t, n_fold) * (r // SUBLANES), 0)

    return pl.pallas_call(
        functools.partial(_dft_unfold_kernel, n_fold=n_fold, w2=w2),
        grid=(bsz, nb),
        in_specs=[pl.BlockSpec((r, r), lambda b, t: (0, 0)),
                  pl.BlockSpec(jc.shape, lambda b, t: (0, 0)),
                  pl.BlockSpec((None, r, w), src_blk),
                  pl.BlockSpec((None, r, w), src_blk),
                  pl.BlockSpec((None, SUBLANES, w), nxt_blk),
                  pl.BlockSpec((None, SUBLANES, w), nxt_blk)],
        out_specs=pl.BlockSpec((None, r, d), lambda b, t: (b, t, 0)),
        out_shape=jax.ShapeDtypeStruct((bsz, n, d), BF16),
        compiler_params=_params("parallel", "arbitrary"),
        name="dft_unfold",
    )(_flip_matrix(r), jc, lo, hi, lo, hi)


def _dft_seq_kernel(cs_ref, ss_ref, p_ref, q_ref, lo_ref, hi_ref):
    a = jnp.dot(cs_ref[...], p_ref[...], preferred_element_type=F32)
    b = jnp.dot(ss_ref[...], q_ref[...], preferred_element_type=F32)
    lo_ref[...] = (a - b).astype(lo_ref.dtype)
    hi_ref[...] = (a + b).astype(hi_ref.dtype)


def dft_seq(cs, ss, p, q, gd, tm):
    bsz, hp, w = p.shape
    rows = cs.shape[0]
    out = jax.ShapeDtypeStruct((bsz, rows, w), BF16)
    mat = pl.BlockSpec((tm, hp), lambda b, g, i: (i, 0))
    src = pl.BlockSpec((None, hp, gd), lambda b, g, i: (b, 0, g))
    dst = pl.BlockSpec((None, tm, gd), lambda b, g, i: (b, i, g))
    return pl.pallas_call(
        _dft_seq_kernel,
        grid=(bsz, w // gd, rows // tm),
        in_specs=[mat, mat, src, src],
        out_specs=[dst, dst],
        out_shape=[out, out],
        compiler_params=_params("parallel", "parallel", "arbitrary"),
        name="dft_seq",
    )(cs, ss, p, q)


def fourier_mix(u3, z_cb0, gd):
    bsz, n, _ = u3.shape
    h = n // 2
    r = min(FLIP_ROWS, h)
    hp = h + r
    h2 = gd // 2
    w2 = h2 + LANES
    cc, sc = _dft_mats(gd, gd ** -0.5)
    cpad = ((0, w2 - h2 - 1), (0, w2 - h2 - 1))
    cc = jnp.pad(cc[:h2 + 1, :h2 + 1], cpad)
    sc = jnp.pad(sc[:h2 + 1, :h2 + 1], cpad)
    cs, ss = _dft_mats(n, n ** -0.5)
    tm = -(-(h + SUBLANES) // (DFT_SEQ_TILES * BF16_SUBLANES)) * BF16_SUBLANES
    pad = ((0, DFT_SEQ_TILES * tm - h - 1), (0, hp - h - 1))
    cs = jnp.pad(cs[:h + 1, :h + 1], pad)
    ss = jnp.pad(ss[:h + 1, :h + 1], pad)
    jc = _flip_matrix(h2)
    assert z_cb0 % F_GROUPS == 0
    p, q = dft_chan(u3, z_cb0 // F_GROUPS, gd, cc, sc, jc, hp, r)
    lo, hi = dft_seq(cs, ss, p, q, w2, tm)
    return dft_unfold(lo, hi, jc, n, r, gd)


def _tile(n, pref):
    return pref if n % pref == 0 else n


def _layer(x, xn, p_i, lw, g_next, *, bsz, seq):
    m, d = x.shape
    nh = M_HEADS
    dv = d // nh
    dk = dv // 2
    qk_w, v_w = nh * dk, nh * dv
    gd = d // F_GROUPS
    n_main = 2 * qk_w + 2 * v_w
    g0 = n_main
    z0 = n_main + 4 * nh
    chunk = min(MLSTM_CHUNK, seq)
    tm = _tile(m, MM_ROWS)

    li = lw["layer"]
    tn = _tile(d, MM_PANEL)
    w_in_t = lw["w_in_t"]
    u = panel_dots([(xn, w_in_t, "cols", li, lambda j: j * tn + jnp.where(j * tn >= g0, z0 - g0, 0))],
                   [], _ep_identity, (m, n_main + d), BF16, tm, tn, "in_proj")
    graw = panel_dots([(xn, w_in_t, "cols", li, lambda j: g0)], [], _ep_identity, (m, LANES), F32,
                      tm, LANES, "gate_proj")

    graw = graw[:, :4 * nh].reshape(bsz, seq, 2, 2, nh)
    gi = graw[:, :, :, 0].reshape(bsz, seq, 2 * nh)
    gf = graw[:, :, :, 1].reshape(bsz, seq, 2 * nh)
    b_arr, e_arr, r_arr = gate_prep(gi, gf, lw["b_igate"][li].reshape(1, 2 * nh),
                                    lw["b_fgate"][li].reshape(1, 2 * nh), chunk)
    b4 = b_arr.reshape(bsz, seq, 2, nh)
    e4 = e_arr.reshape(bsz, seq, 2, nh)
    gcol = jnp.stack([b4[:, :, 0], e4[:, :, 0], b4[:, :, 1], e4[:, :, 1]], axis=-1)
    gcol = gcol.transpose(0, 2, 1, 3)
    grow = r_arr.reshape(bsz, seq // chunk, chunk, 2, nh).transpose(0, 4, 3, 1, 2)
    grow = grow[:, :, :, :, None, :]
    u3 = u.reshape(bsz, seq, n_main + d)
    hg = mlstm(u3, gcol, grow, lw["g_head"][li], dk=dk, dv=dv, q_cb0=0, k_cb0=qk_w // dk,
               v_cb0=2 * qk_w // dv, o_cb0=(2 * qk_w + v_w) // dv, chunk=chunk)
    hg = hg.reshape(m, v_w)

    zf = fourier_mix(u3, n_main // gd, gd).reshape(m, d)

    gates = panel_dots([(xn, lw["w_merge"], "rows", li, 0)],
                       [(lw["b_merge"][li].reshape(1, 2 * d), "row", 0)],
                       _ep_bias_sigmoid, (m, 2 * d), BF16, tm, tn, "merge_gate")
    ym = panel_dots([(hg, lw["w_branch_m"], "rows", li, 0)], [(gates, "tile", 0)],
                    _ep_gate_mul, (m, d), BF16, tm, tn, "branch_m")
    mix = panel_dots([(zf, lw["w_branch_f"], "rows", li, 0)], [(gates, "tile", d // tn), (ym, "tile", 0)],
                     _ep_gate_mul_add, (m, d), BF16, tm, tn, "branch_f")
    y = panel_dots([(mix, lw["w_out"], "rows", li, 0)], [], _ep_identity, (m, d), BF16, tm, tn, "out_proj")
    x, hn = resnorm(x, y, lw["g_mix_post"][li], lw["g_mlp_pre"][li], BF16)

    d_ff = lw["w_up"].shape[-1]
    n_steps = (d_ff // tn + 1) * (m // tm)
    cast_rows_per_step = next(r for r in (LANES, 2 * LANES, 4 * LANES, 8 * LANES, d_ff)
                              if d_ff % r == 0 and d_ff // r <= n_steps)
    up, w_down = panel_dots([(hn, lw["w_up"], "rows", li, 0)], [], _ep_relu2, (m, d_ff), BF16, tm, tn,
                            "mlp_up", side_cast=(lw["w_down"], li, cast_rows_per_step))
    y = matmul_ktiled(up, w_down, BF16, tm, tn, _tile(d_ff, MM_K_TILE), "mlp_down")
    x, = resnorm(x, y, lw["g_mlp_post"][li], None, BF16)

    ple = panel_dots([(x, lw["w_ple_gate"], "rows", li, 0), (p_i, lw["w_ple_in"], "small", li, 0)],
                     [], _ep_sigmoid_mul, (m, d), BF16, tm, tn, "ple")
    if g_next is None:
        x, = resnorm(x, ple, lw["g_ple_post"][li], None, F32)
        return x, None
    return resnorm(x, ple, lw["g_ple_post"][li], g_next, BF16)


def kernel(x, p, g_mix_pre, g_mix_post, g_mlp_pre, g_mlp_post, g_ple_post, w_in, b_igate, b_fgate,
           g_head, w_branch_m, w_branch_f, w_merge, b_merge, w_out, w_up, w_down, w_ple_in, w_ple_gate):
    bsz, seq, d = x.shape
    depth = w_in.shape[0]
    m = bsz * seq
    xf = x.reshape(m, d)
    xn = rmsnorm(xf, g_mix_pre[0])
    lw = dict(w_in_t=jnp.swapaxes(w_in, 1, 2), b_igate=b_igate, b_fgate=b_fgate, g_head=g_head, w_branch_m=w_branch_m,
              w_branch_f=w_branch_f, w_merge=w_merge, b_merge=b_merge, w_out=w_out, w_up=w_up,
              w_down=w_down, w_ple_in=w_ple_in, w_ple_gate=w_ple_gate, g_mix_post=g_mix_post,
              g_mlp_pre=g_mlp_pre, g_mlp_post=g_mlp_post, g_ple_post=g_ple_post)
    for i in range(depth):
        g_next = g_mix_pre[i + 1] if i + 1 < depth else None
        xf, xn = _layer(xf, xn, p[i].reshape(m, -1), dict(lw, layer=i), g_next, bsz=bsz, seq=seq)
    return xf.reshape(bsz, seq, d)
```

```python
import functools
import math

import jax
import jax.numpy as jnp
from jax import lax
from jax.experimental import pallas as pl
from jax.experimental.pallas import tpu as pltpu

F32 = jnp.float32
BF16 = jnp.bfloat16

M_HEADS = 8
F_GROUPS = 4
GATE_CAP = 15.0
EPS = 1e-6
MLSTM_CHUNK = 256
V7X_VMEM_BYTES = 64 * 1024 * 1024
VMEM_LIMIT = V7X_VMEM_BYTES - 4 * 1024 * 1024
LANES = 128
MM_ROWS = 1024
MM_PANEL = 1024
MM_K_TILE = 4096
NORM_ROWS = 512


def _params(*semantics):
    return pltpu.CompilerParams(dimension_semantics=semantics, vmem_limit_bytes=VMEM_LIMIT)


def _rms(x, g):
    return x * lax.rsqrt(jnp.mean(x * x, axis=-1, keepdims=True) + EPS) * g


def _sigmoid(x):
    return 0.5 * jnp.tanh(0.5 * x) + 0.5


def _rmsnorm_kernel(x_ref, g_ref, o_ref):
    o_ref[...] = _rms(x_ref[...], g_ref[...]).astype(o_ref.dtype)


def rmsnorm(x, g, tm=NORM_ROWS):
    m, d = x.shape
    return pl.pallas_call(
        _rmsnorm_kernel,
        grid=(m // tm,),
        in_specs=[pl.BlockSpec((tm, d), lambda i: (i, 0)),
                  pl.BlockSpec((1, d), lambda i: (0, 0))],
        out_specs=pl.BlockSpec((tm, d), lambda i: (i, 0)),
        out_shape=jax.ShapeDtypeStruct((m, d), BF16),
        compiler_params=_params("parallel"),
        name="rmsnorm",
    )(x, g.reshape(1, d))


def _resnorm_kernel(x_ref, y_ref, gp_ref, gn_ref, xo_ref, *xn_ref):
    xnew = x_ref[...].astype(F32) + _rms(y_ref[...].astype(F32), gp_ref[...])
    xo_ref[...] = xnew.astype(xo_ref.dtype)
    if xn_ref:
        xn_ref[0][...] = _rms(xnew, gn_ref[...]).astype(xn_ref[0].dtype)


def resnorm(x, y, g_post, g_next, x_dtype, tm=NORM_ROWS):
    m, d = x.shape
    gn = g_post if g_next is None else g_next
    row = pl.BlockSpec((tm, d), lambda i: (i, 0))
    vec = pl.BlockSpec((1, d), lambda i: (0, 0))
    out_shape = [jax.ShapeDtypeStruct((m, d), x_dtype)]
    if g_next is not None:
        out_shape.append(jax.ShapeDtypeStruct((m, d), BF16))
    return pl.pallas_call(
        _resnorm_kernel,
        grid=(m // tm,),
        in_specs=[row, row, vec, vec],
        out_specs=[row] * len(out_shape),
        out_shape=out_shape,
        compiler_params=_params("parallel"),
        name="resnorm",
    )(x, y, g_post.reshape(1, d), gn.reshape(1, d))


def _panel_dots_kernel(*refs, kinds, n_extra, side, epilogue, ck):
    n_dots = len(kinds)
    n_in = 2 * n_dots + n_extra + side
    out_ref = refs[n_in]
    scratch = list(refs[n_in + 1 + side:])
    bufs = [(scratch.pop(0), scratch.pop(0)) if kinds[p] != "small" else None for p in range(n_dots)]
    g = pl.program_id(0)
    i = pl.program_id(1)
    rows = pl.ds(pl.multiple_of(i * ck, ck), ck)

    def stage(parity):
        for p in range(n_dots):
            w_ref = refs[2 * p + 1]
            if kinds[p] == "rows":
                bufs[p][parity][rows, :] = w_ref[...].astype(BF16)
            elif kinds[p] == "cols":
                bufs[p][parity][rows, :] = w_ref[0].T.astype(BF16)
        if side:
            refs[n_in + 1][...] = refs[n_in - 1][...].astype(refs[n_in + 1].dtype)

    def compute(parity):
        accs = []
        for p in range(n_dots):
            w = refs[2 * p + 1][...].astype(BF16) if kinds[p] == "small" else bufs[p][parity][...]
            accs.append(jnp.dot(refs[2 * p][...].astype(BF16), w, preferred_element_type=F32))
        extras = [r[...] for r in refs[2 * n_dots:2 * n_dots + n_extra]]
        out_ref[...] = epilogue(accs, extras).astype(out_ref.dtype)

    @pl.when(g == 0)
    def _():
        stage(0)

    for parity in (0, 1):
        @pl.when((g > 0) & (g % 2 == parity))
        def _(parity=parity):
            stage(parity)
            compute(1 - parity)


def panel_dots(pairs, extras, epilogue, out_shape, out_dtype, tm, tn, name, side_cast=None):
    m, n = out_shape
    n_i, n_j = m // tm, n // tn
    ck = None
    in_specs, args, scratch, kinds = [], [], [], []

    def jj(g):
        return jnp.maximum(g - 1, 0)

    def ii(g, i):
        return jnp.where(g == 0, 0, i)

    def pj(g):
        return jnp.minimum(g, n_j - 1)

    for a, w, kind, layer, where in pairs:
        k = a.shape[1]
        lead = () if layer is None else (layer,)
        lead_blk = () if layer is None else (None,)
        in_specs.append(pl.BlockSpec((tm, k), lambda g, i: (ii(g, i), 0)))
        if kind == "small":
            in_specs.append(pl.BlockSpec(lead_blk + (k, tn),
                                         lambda g, i, lead=lead, c=where: lead + (0, c + jj(g))))
        else:
            assert ck in (None, k // n_i) and k % n_i == 0
            ck = k // n_i
            scratch += [pltpu.VMEM((k, tn), BF16), pltpu.VMEM((k, tn), BF16)]
            if kind == "rows":
                in_specs.append(pl.BlockSpec(lead_blk + (ck, tn),
                                             lambda g, i, lead=lead, c=where: lead + (i, c + pj(g))))
            else:
                in_specs.append(pl.BlockSpec(
                    (pl.Element(1), pl.Element(tn), pl.Element(ck)),
                    lambda g, i, l=layer, f=where, ck=ck: (l, pl.multiple_of(f(pj(g)), 8), i * ck)))
        args += [a, w]
        kinds.append(kind)
    for arr, kind, cb0 in extras:
        if kind == "row":
            in_specs.append(pl.BlockSpec((1, tn), lambda g, i, cb0=cb0: (0, cb0 + jj(g))))
        else:
            in_specs.append(pl.BlockSpec((tm, tn), lambda g, i, cb0=cb0: (ii(g, i), cb0 + jj(g))))
        args.append(arr)
    out_specs = [pl.BlockSpec((tm, tn), lambda g, i: (ii(g, i), jj(g)))]
    out_shapes = [jax.ShapeDtypeStruct((m, n), out_dtype)]
    if side_cast is not None:
        sw, s_layer, s_rows = side_cast
        _, s_k, s_n = sw.shape
        n_blk = s_k // s_rows
        assert s_k % s_rows == 0 and n_blk <= (n_j + 1) * n_i

        def blk(g, i):
            return jnp.minimum(g * n_i + i, n_blk - 1)

        in_specs.append(pl.BlockSpec((None, s_rows, s_n), lambda g, i: (s_layer, blk(g, i), 0)))
        args.append(sw)
        out_specs.append(pl.BlockSpec((s_rows, s_n), lambda g, i: (blk(g, i), 0)))
        out_shapes.append(jax.ShapeDtypeStruct((s_k, s_n), BF16))
    res = pl.pallas_call(
        functools.partial(_panel_dots_kernel, kinds=tuple(kinds), n_extra=len(extras),
                          side=int(side_cast is not None), epilogue=epilogue, ck=ck),
        grid=(n_j + 1, n_i),
        in_specs=in_specs,
        out_specs=out_specs,
        out_shape=out_shapes,
        scratch_shapes=scratch,
        compiler_params=_params("arbitrary", "arbitrary"),
        name=name,
    )(*args)
    return res[0] if side_cast is None else res


def _ep_identity(accs, extras):
    return accs[0]


def _ep_bias_sigmoid(accs, extras):
    return _sigmoid(accs[0] + extras[0])


def _ep_relu2(accs, extras):
    return jnp.square(jnp.maximum(accs[0], 0.0))


def _ep_gate_mul(accs, extras):
    return extras[0].astype(F32) * accs[0]


def _ep_gate_mul_add(accs, extras):
    return extras[0].astype(F32) * accs[0] + extras[1].astype(F32)


def _ep_sigmoid_mul(accs, extras):
    return _sigmoid(accs[0]) * accs[1]


def _matmul_ktiled_kernel(a_ref, w_ref, o_ref, acc_ref):
    kk = pl.program_id(2)

    @pl.when(kk == 0)
    def _():
        acc_ref[...] = jnp.zeros_like(acc_ref)

    acc_ref[...] += jnp.dot(a_ref[...], w_ref[...], preferred_element_type=F32)

    @pl.when(kk == pl.num_programs(2) - 1)
    def _():
        o_ref[...] = acc_ref[...].astype(o_ref.dtype)


def matmul_ktiled(a, w, out_dtype, tm, tn, tk, name):
    m, k = a.shape
    n = w.shape[1]
    return pl.pallas_call(
        _matmul_ktiled_kernel,
        grid=(m // tm, n // tn, k // tk),
        in_specs=[pl.BlockSpec((tm, tk), lambda i, j, kk: (i, kk)),
                  pl.BlockSpec((tk, tn), lambda i, j, kk: (kk, j))],
        out_specs=pl.BlockSpec((tm, tn), lambda i, j, kk: (i, j)),
        out_shape=jax.ShapeDtypeStruct((m, n), out_dtype),
        scratch_shapes=[pltpu.VMEM((tm, tn), F32)],
        compiler_params=_params("parallel", "parallel", "arbitrary"),
        name=name,
    )(a, w)


def _split3(x):
    hi = x.astype(BF16)
    r1 = x - hi.astype(F32)
    mid = r1.astype(BF16)
    lo = (r1 - mid.astype(F32)).astype(BF16)
    return hi, mid, lo


def _gate_prep_kernel(gi_ref, gf_ref, bi_ref, bf_ref, b_ref, e_ref, r_ref, *, chunk):
    s, w = gi_ref.shape
    nh = w // 2
    log_i = GATE_CAP * jnp.tanh((gi_ref[...] + bi_ref[...]) * (1.0 / GATE_CAP))
    af = GATE_CAP * jnp.tanh((gf_ref[...] + bf_ref[...]) * (1.0 / GATE_CAP))
    log_f = jnp.minimum(af, 0.0) - jnp.log1p(jnp.exp(-jnp.abs(af)))
    ri = lax.broadcasted_iota(jnp.int32, (chunk, chunk), 0)
    ci = lax.broadcasted_iota(jnp.int32, (chunk, chunk), 1)
    tril = (ci <= ri).astype(BF16)
    triu = (ci >= ri).astype(BF16)
    is_fwd = lax.broadcasted_iota(jnp.int32, (chunk, w), 1) < nh
    for c in range(s // chunk):
        rows = slice(c * chunk, (c + 1) * chunk)
        parts = _split3(log_f[rows, :])
        cum_f = sum(jnp.dot(tril, p, preferred_element_type=F32) for p in parts)
        cum_b = sum(jnp.dot(triu, p, preferred_element_type=F32) for p in parts)
        b = jnp.where(is_fwd, cum_f, cum_b)
        tot = jnp.where(is_fwd[:1], cum_f[chunk - 1:chunk, :], cum_b[0:1, :])
        li = log_i[rows, :]
        b_ref[rows, :] = b
        e_ref[rows, :] = tot - b + li
        r_ref[rows, :] = li - b


def gate_prep(gi, gf, bi, bf, chunk):
    bsz, s, w = gi.shape
    blk = pl.BlockSpec((None, s, w), lambda b: (b, 0, 0))
    vec = pl.BlockSpec((1, w), lambda b: (0, 0))
    out = jax.ShapeDtypeStruct((bsz, s, w), F32)
    return pl.pallas_call(
        functools.partial(_gate_prep_kernel, chunk=chunk),
        grid=(bsz,),
        in_specs=[blk, blk, vec, vec],
        out_specs=[blk, blk, blk],
        out_shape=[out, out, out],
        compiler_params=_params("parallel"),
        name="gate_prep",
    )(gi, gf, bi, bf)


def _mlstm_kernel(q_ref, k_ref, v_ref, o_ref, gcol_ref, grow_ref, gh_ref, out_ref,
                  h_ref, c_ref, n_ref, *, chunk, q_scale):
    s = q_ref.shape[0]
    nc = s // chunk
    assert nc % 2 == 0
    ri = lax.broadcasted_iota(jnp.int32, (chunk, chunk), 0)
    ci = lax.broadcasted_iota(jnp.int32, (chunk, chunk), 1)
    masks = (ci <= ri, ci >= ri)
    c_ref[...] = jnp.zeros_like(c_ref)
    n_ref[...] = jnp.zeros_like(n_ref)

    def rows_of(c):
        return pl.ds(pl.multiple_of(c * chunk, chunk), chunk)

    def scan_pair(cf, cb):
        dirs = (0, 1)
        rows = [rows_of(c) for c in (cf, cb)]
        q = [q_ref[r, :] * q_scale for r in rows]
        k = [k_ref[r, :] for r in rows]
        v = [v_ref[r, :] for r in rows]
        gc = [gcol_ref[r, :] for r in rows]
        bcol = [gc[d][:, 2 * d:2 * d + 1] for d in dirs]
        ecol = [gc[d][:, 2 * d + 1:2 * d + 2] for d in dirs]
        rrow = [grow_ref[0, cf], grow_ref[1, cb]]
        last = (chunk - 1, 0)
        decay = [jnp.exp(bcol[d][last[d]:last[d] + 1, :]) for d in dirs]
        c_prev = [c_ref[d] for d in dirs]
        n_prev = [n_ref[d] for d in dirs]

        qk = [lax.dot_general(q[d], k[d], (((1,), (1,)), ((), ())), preferred_element_type=F32)
              for d in dirs]
        bb = [jnp.broadcast_to(bcol[d], (chunk, LANES)) for d in dirs]
        wide = lambda x, n: jnp.concatenate([x] * (n // LANES), axis=1)
        w_intra = [jnp.exp(jnp.where(masks[d], wide(bb[d], chunk) + rrow[d], -jnp.inf)) for d in dirs]
        w_inter = [jnp.exp(bb[d]) for d in dirs]
        sc = [qk[d] * w_intra[d] for d in dirs]
        qf = [q[d].astype(F32) for d in dirs]
        lhs = [jnp.concatenate([sc[d].astype(BF16), (qf[d] * wide(w_inter[d], qf[d].shape[1])).astype(BF16)], axis=1) for d in dirs]
        rhs = [jnp.concatenate([v[d], c_prev[d].astype(BF16)], axis=0) for d in dirs]
        num = [jnp.dot(lhs[d], rhs[d], preferred_element_type=F32) for d in dirs]
        qn = [jnp.sum(qf[d] * n_prev[d], axis=1, keepdims=True) for d in dirs]
        den = [jnp.sum(sc[d], axis=1, keepdims=True) + w_inter[d][:, :1] * qn[d] for d in dirs]
        h = [num[d] * (1.0 / jnp.maximum(jnp.abs(den[d]), 1.0)) for d in dirs]

        kw = [k[d].astype(F32) * jnp.exp(ecol[d]) for d in dirs]
        upd = [lax.dot_general(kw[d].astype(BF16), v[d], (((0,), (0,)), ((), ())),
                               preferred_element_type=F32) for d in dirs]
        for d in dirs:
            c_ref[d] = decay[d] * c_prev[d] + upd[d]
            n_ref[d] = decay[d] * n_prev[d] + jnp.sum(kw[d], axis=0, keepdims=True)
        return h

    def finish(c, tot):
        rows = rows_of(c)
        y = _rms(tot, gh_ref[...]) * _sigmoid(o_ref[rows, :].astype(F32))
        out_ref[rows, :] = y.astype(out_ref.dtype)

    def first_half(t, carry):
        hf, hb = scan_pair(t, nc - 1 - t)
        h_ref[rows_of(t), :] = hf
        h_ref[rows_of(nc - 1 - t), :] = hb
        return carry

    def second_half(t, carry):
        cf, cb = t, nc - 1 - t
        hf, hb = scan_pair(cf, cb)
        finish(cf, hf + h_ref[rows_of(cf), :])
        finish(cb, hb + h_ref[rows_of(cb), :])
        return carry

    lax.fori_loop(0, nc // 2, first_half, 0, unroll=2)
    lax.fori_loop(nc // 2, nc, second_half, 0, unroll=2)


def mlstm(u3, gcol, grow, g_head, *, dk, dv, q_cb0, k_cb0, v_cb0, o_cb0, chunk):
    bsz, s, _ = u3.shape
    nh = gcol.shape[1]
    nc = s // chunk

    def col(width, cb0):
        return pl.BlockSpec((None, s, width), lambda b, h, cb0=cb0: (b, 0, cb0 + h))

    return pl.pallas_call(
        functools.partial(_mlstm_kernel, chunk=chunk, q_scale=dk ** -0.5),
        grid=(bsz, nh),
        in_specs=[col(dk, q_cb0), col(dk, k_cb0), col(dv, v_cb0), col(dv, o_cb0),
                  pl.BlockSpec((None, None, s, 4), lambda b, h: (b, h, 0, 0)),
                  pl.BlockSpec((None, None, 2, nc, 1, chunk), lambda b, h: (b, h, 0, 0, 0, 0)),
                  pl.BlockSpec((1, dv), lambda b, h: (0, h))],
        out_specs=pl.BlockSpec((None, s, dv), lambda b, h: (b, 0, h)),
        out_shape=jax.ShapeDtypeStruct((bsz, s, nh * dv), BF16),
        scratch_shapes=[pltpu.VMEM((s, dv), F32), pltpu.VMEM((2, dk, dv), F32), pltpu.VMEM((2, 1, dk), F32)],
        compiler_params=_params("parallel", "parallel"),
        name="mlstm",
    )(u3, u3, u3, u3, gcol, grow, g_head.reshape(1, nh * dv))


def _cos_sin(rows, n_cols, period):
    k = jnp.arange(n_cols, dtype=jnp.int32)
    ang = ((rows[:, None] * k[None, :]) % period).astype(F32) * (2.0 * math.pi / period)
    return jnp.cos(ang), jnp.sin(ang)


def _dft_mats(n, scale):
    n1 = 1 << ((n.bit_length() - 1) // 2)
    n2 = n // n1
    ca, sa = _cos_sin(jnp.arange(n1, dtype=jnp.int32), n, n1)
    cb, sb = _cos_sin(jnp.arange(n2, dtype=jnp.int32), n, n)
    ca, sa, cb, sb = ca[:, None, :], sa[:, None, :], cb[None, :, :], sb[None, :, :]
    c = (ca * cb - sa * sb) * scale
    s = (sa * cb + ca * sb) * scale
    return c.reshape(n, n).astype(BF16), s.reshape(n, n).astype(BF16)


FLIP_ROWS = 256
SUBLANES = 8
BF16_SUBLANES = 16
DFT_SEQ_TILES = 3


def _flip_matrix(r):
    t = jnp.arange(r, dtype=jnp.int32)
    return ((t[:, None] + t[None, :]) == r).astype(BF16)


def _shifted_flip(j, blk, row0):
    rev = jnp.dot(j, blk, preferred_element_type=F32)
    first = lax.broadcasted_iota(jnp.int32, rev.shape, 0) == 0
    return jnp.where(first, row0, rev)


def _fold_cols(x, jc, sign):
    h2 = jc.shape[0]
    rev = jnp.dot(x[:, h2:], jc, preferred_element_type=F32).astype(x.dtype)
    tail = x[:, h2:h2 + LANES]
    if sign > 0:
        lane0 = lax.broadcasted_iota(jnp.int32, tail.shape, 1) == 0
        return jnp.concatenate([x[:, :h2] + rev, jnp.where(lane0, tail, jnp.zeros_like(tail))], axis=1)
    return jnp.concatenate([x[:, :h2] - rev, jnp.zeros_like(tail)], axis=1)


def _unfold_cols(left, mirror, jc):
    h2 = jc.shape[0]
    right = jnp.dot(mirror[:, :h2], jc, preferred_element_type=F32).astype(left.dtype)
    head = right[:, :LANES]
    lane0 = lax.broadcasted_iota(jnp.int32, head.shape, 1) == 0
    head = jnp.where(lane0, left[:, h2:h2 + LANES], head)
    return jnp.concatenate([left[:, :h2], head, right[:, LANES:]], axis=1)


def _dft_chan_kernel(j_ref, jc_ref, zlo_ref, za_ref, zb_ref, cc_ref, sc_ref, p_ref, q_ref, *, n_fold, gd):
    i = pl.program_id(1)
    a = zlo_ref[...]
    row0 = jnp.where(i > 0, zb_ref[0:1, :].astype(F32), 0.0)
    r = _shifted_flip(j_ref[...], za_ref[...], row0)
    r = jnp.where(i < n_fold, r, 0.0).astype(BF16)
    jc = jc_ref[...]
    w2 = cc_ref.shape[0]
    for g in range(p_ref.shape[1] // w2):
        ag, rg = a[:, g * gd:(g + 1) * gd], r[:, g * gd:(g + 1) * gd]
        p_ref[:, g * w2:(g + 1) * w2] = jnp.dot(_fold_cols(ag + rg, jc, 1), cc_ref[...],
                                                preferred_element_type=F32).astype(p_ref.dtype)
        q_ref[:, g * w2:(g + 1) * w2] = jnp.dot(_fold_cols(ag - rg, jc, -1), sc_ref[...],
                                                preferred_element_type=F32).astype(q_ref.dtype)


def dft_chan(u3, z_blk, gd, cc, sc, jc, hp, r):
    bsz, n, _ = u3.shape
    w2 = cc.shape[0]
    d = F_GROUPS * gd
    nb = n // r
    out = jax.ShapeDtypeStruct((bsz, hp, F_GROUPS * w2), BF16)
    dst = pl.BlockSpec((None, r, F_GROUPS * w2), lambda b, i: (b, i, 0))
    mat = pl.BlockSpec((w2, w2), lambda b, i: (0, 0))
    return pl.pallas_call(
        functools.partial(_dft_chan_kernel, n_fold=nb // 2, gd=gd),
        grid=(bsz, hp // r),
        in_specs=[pl.BlockSpec((r, r), lambda b, i: (0, 0)),
                  pl.BlockSpec(jc.shape, lambda b, i: (0, 0)),
                  pl.BlockSpec((None, r, d), lambda b, i: (b, i, z_blk)),
                  pl.BlockSpec((None, r, d), lambda b, i: (b, nb - 1 - i, z_blk)),
                  pl.BlockSpec((None, SUBLANES, d),
                               lambda b, i: (b, jnp.minimum(nb - i, nb - 1) * (r // SUBLANES), z_blk)),
                  mat, mat],
        out_specs=[dst, dst],
        out_shape=[out, out],
        compiler_params=_params("parallel", "arbitrary"),
        name="dft_chan",
    )(_flip_matrix(r), jc, u3, u3, u3, cc, sc)


def _dft_unfold_kernel(j_ref, jc_ref, lo_ref, hi_ref, nlo_ref, nhi_ref, o_ref, *, n_fold, w2):
    t = pl.program_id(1)
    jc = jc_ref[...]
    groups = lo_ref.shape[1] // w2

    def rows_of(left_ref, mirror_ref):
        return jnp.concatenate([_unfold_cols(left_ref[:, g * w2:(g + 1) * w2],
                                             mirror_ref[:, g * w2:(g + 1) * w2], jc)
                                for g in range(groups)], axis=1)

    @pl.when(t < n_fold)
    def _():
        o_ref[...] = rows_of(lo_ref, hi_ref)

    @pl.when(t >= n_fold)
    def _():
        row0 = rows_of(nhi_ref, nlo_ref)[0:1, :].astype(F32)
        o_ref[...] = _shifted_flip(j_ref[...], rows_of(hi_ref, lo_ref), row0).astype(o_ref.dtype)


def dft_unfold(lo, hi, jc, n, r, gd):
    bsz, _, w = lo.shape
    w2 = gd // 2 + LANES
    d = w // w2 * gd
    nb = n // r
    n_fold = nb // 2

    def src_blk(b, t):
        return (b, jnp.where(t < n_fold, t, nb - 1 - t), 0)

    def nxt_blk(b, t):
        return (b, jnp.minimum(nb - t, n_fold) * (r // SUBLANES), 0)

    return pl.pallas_call(
        functools.partial(_dft_unfold_kernel, n_fold=n_fold, w2=w2),
        grid=(bsz, nb),
        in_specs=[pl.BlockSpec((r, r), lambda b, t: (0, 0)),
                  pl.BlockSpec(jc.shape, lambda b, t: (0, 0)),
                  pl.BlockSpec((None, r, w), src_blk),
                  pl.BlockSpec((None, r, w), src_blk),
                  pl.BlockSpec((None, SUBLANES, w), nxt_blk),
                  pl.BlockSpec((None, SUBLANES, w), nxt_blk)],
        out_specs=pl.BlockSpec((None, r, d), lambda b, t: (b, t, 0)),
        out_shape=jax.ShapeDtypeStruct((bsz, n, d), BF16),
        compiler_params=_params("parallel", "arbitrary"),
        name="dft_unfold",
    )(_flip_matrix(r), jc, lo, hi, lo, hi)


def _dft_seq_kernel(cs_ref, ss_ref, p_ref, q_ref, lo_ref, hi_ref):
    a = jnp.dot(cs_ref[...], p_ref[...], preferred_element_type=F32)
    b = jnp.dot(ss_ref[...], q_ref[...], preferred_element_type=F32)
    lo_ref[...] = (a - b).astype(lo_ref.dtype)
    hi_ref[...] = (a + b).astype(hi_ref.dtype)


def dft_seq(cs, ss, p, q, gd, tm):
    bsz, hp, w = p.shape
    rows = cs.shape[0]
    out = jax.ShapeDtypeStruct((bsz, rows, w), BF16)
    mat = pl.BlockSpec((tm, hp), lambda b, g, i: (i, 0))
    src = pl.BlockSpec((None, hp, gd), lambda b, g, i: (b, 0, g))
    dst = pl.BlockSpec((None, tm, gd), lambda b, g, i: (b, i, g))
    return pl.pallas_call(
        _dft_seq_kernel,
        grid=(bsz, w // gd, rows // tm),
        in_specs=[mat, mat, src, src],
        out_specs=[dst, dst],
        out_shape=[out, out],
        compiler_params=_params("parallel", "parallel", "arbitrary"),
        name="dft_seq",
    )(cs, ss, p, q)


def fourier_mix(u3, z_cb0, gd):
    bsz, n, _ = u3.shape
    h = n // 2
    r = min(FLIP_ROWS, h)
    hp = h + r
    h2 = gd // 2
    w2 = h2 + LANES
    cc, sc = _dft_mats(gd, gd ** -0.5)
    cpad = ((0, w2 - h2 - 1), (0, w2 - h2 - 1))
    cc = jnp.pad(cc[:h2 + 1, :h2 + 1], cpad)
    sc = jnp.pad(sc[:h2 + 1, :h2 + 1], cpad)
    cs, ss = _dft_mats(n, n ** -0.5)
    tm = -(-(h + SUBLANES) // (DFT_SEQ_TILES * BF16_SUBLANES)) * BF16_SUBLANES
    pad = ((0, DFT_SEQ_TILES * tm - h - 1), (0, hp - h - 1))
    cs = jnp.pad(cs[:h + 1, :h + 1], pad)
    ss = jnp.pad(ss[:h + 1, :h + 1], pad)
    jc = _flip_matrix(h2)
    assert z_cb0 % F_GROUPS == 0
    p, q = dft_chan(u3, z_cb0 // F_GROUPS, gd, cc, sc, jc, hp, r)
    lo, hi = dft_seq(cs, ss, p, q, w2, tm)
    return dft_unfold(lo, hi, jc, n, r, gd)


def _tile(n, pref):
    return pref if n % pref == 0 else n


def _layer(x, xn, p_i, lw, g_next, *, bsz, seq):
    m, d = x.shape
    nh = M_HEADS
    dv = d // nh
    dk = dv // 2
    qk_w, v_w = nh * dk, nh * dv
    gd = d // F_GROUPS
    n_main = 2 * qk_w + 2 * v_w
    g0 = n_main
    z0 = n_main + 4 * nh
    chunk = min(MLSTM_CHUNK, seq)
    tm = _tile(m, MM_ROWS)

    li = lw["layer"]
    tn = _tile(d, MM_PANEL)
    w_in_t = lw["w_in_t"]
    u = panel_dots([(xn, w_in_t, "cols", li, lambda j: j * tn + jnp.where(j * tn >= g0, z0 - g0, 0))],
                   [], _ep_identity, (m, n_main + d), BF16, tm, tn, "in_proj")
    graw = panel_dots([(xn, w_in_t, "cols", li, lambda j: g0)], [], _ep_identity, (m, LANES), F32,
                      tm, LANES, "gate_proj")

    graw = graw[:, :4 * nh].reshape(bsz, seq, 2, 2, nh)
    gi = graw[:, :, :, 0].reshape(bsz, seq, 2 * nh)
    gf = graw[:, :, :, 1].reshape(bsz, seq, 2 * nh)
    b_arr, e_arr, r_arr = gate_prep(gi, gf, lw["b_igate"][li].reshape(1, 2 * nh),
                                    lw["b_fgate"][li].reshape(1, 2 * nh), chunk)
    b4 = b_arr.reshape(bsz, seq, 2, nh)
    e4 = e_arr.reshape(bsz, seq, 2, nh)
    gcol = jnp.stack([b4[:, :, 0], e4[:, :, 0], b4[:, :, 1], e4[:, :, 1]], axis=-1)
    gcol = gcol.transpose(0, 2, 1, 3)
    grow = r_arr.reshape(bsz, seq // chunk, chunk, 2, nh).transpose(0, 4, 3, 1, 2)
    grow = grow[:, :, :, :, None, :]
    u3 = u.reshape(bsz, seq, n_main + d)
    hg = mlstm(u3, gcol, grow, lw["g_head"][li], dk=dk, dv=dv, q_cb0=0, k_cb0=qk_w // dk,
               v_cb0=2 * qk_w // dv, o_cb0=(2 * qk_w + v_w) // dv, chunk=chunk)
    hg = hg.reshape(m, v_w)

    zf = fourier_mix(u3, n_main // gd, gd).reshape(m, d)

    gates = panel_dots([(xn, lw["w_merge"], "rows", li, 0)],
                       [(lw["b_merge"][li].reshape(1, 2 * d), "row", 0)],
                       _ep_bias_sigmoid, (m, 2 * d), BF16, tm, tn, "merge_gate")
    ym = panel_dots([(hg, lw["w_branch_m"], "rows", li, 0)], [(gates, "tile", 0)],
                    _ep_gate_mul, (m, d), BF16, tm, tn, "branch_m")
    mix = panel_dots([(zf, lw["w_branch_f"], "rows", li, 0)], [(gates, "tile", d // tn), (ym, "tile", 0)],
                     _ep_gate_mul_add, (m, d), BF16, tm, tn, "branch_f")
    y = panel_dots([(mix, lw["w_out"], "rows", li, 0)], [], _ep_identity, (m, d), BF16, tm, tn, "out_proj")
    x, hn = resnorm(x, y, lw["g_mix_post"][li], lw["g_mlp_pre"][li], BF16)

    d_ff = lw["w_up"].shape[-1]
    n_steps = (d_ff // tn + 1) * (m // tm)
    cast_rows_per_step = next(r for r in (LANES, 2 * LANES, 4 * LANES, 8 * LANES, d_ff)
                              if d_ff % r == 0 and d_ff // r <= n_steps)
    up, w_down = panel_dots([(hn, lw["w_up"], "rows", li, 0)], [], _ep_relu2, (m, d_ff), BF16, tm, tn,
                            "mlp_up", side_cast=(lw["w_down"], li, cast_rows_per_step))
    y = matmul_ktiled(up, w_down, BF16, tm, tn, _tile(d_ff, MM_K_TILE), "mlp_down")
    x, = resnorm(x, y, lw["g_mlp_post"][li], None, BF16)

    ple = panel_dots([(x, lw["w_ple_gate"], "rows", li, 0), (p_i, lw["w_ple_in"], "small", li, 0)],
                     [], _ep_sigmoid_mul, (m, d), BF16, tm, tn, "ple")
    if g_next is None:
        x, = resnorm(x, ple, lw["g_ple_post"][li], None, F32)
        return x, None
    return resnorm(x, ple, lw["g_ple_post"][li], g_next, BF16)


def kernel(x, p, g_mix_pre, g_mix_post, g_mlp_pre, g_mlp_post, g_ple_post, w_in, b_igate, b_fgate,
           g_head, w_branch_m, w_branch_f, w_merge, b_merge, w_out, w_up, w_down, w_ple_in, w_ple_gate):
    bsz, seq, d = x.shape
    depth = w_in.shape[0]
    m = bsz * seq
    xf = x.reshape(m, d)
    xn = rmsnorm(xf, g_mix_pre[0])
    lw = dict(w_in_t=jnp.swapaxes(w_in, 1, 2), b_igate=b_igate, b_fgate=b_fgate, g_head=g_head, w_branch_m=w_branch_m,
              w_branch_f=w_branch_f, w_merge=w_merge, b_merge=b_merge, w_out=w_out, w_up=w_up,
              w_down=w_down, w_ple_in=w_ple_in, w_ple_gate=w_ple_gate, g_mix_post=g_mix_post,
              g_mlp_pre=g_mlp_pre, g_mlp_post=g_mlp_post, g_ple_post=g_ple_post)
    for i in range(depth):
        g_next = g_mix_pre[i + 1] if i + 1 < depth else None
        xf, xn = _layer(xf, xn, p[i].reshape(m, -1), dict(lw, layer=i), g_next, bsz=bsz, seq=seq)
    return xf.reshape(bsz, seq, d)
```
